```python
import math
import jax, jax.numpy as jnp
from jax import lax
import numpy as np

D_MODEL = 1024
BATCH = 1
SEQ = 16384
DEPTH = 2
DEC_BATCH = 128
DEC_SEQ = 1
PAST_LEN = 16384
PAGE_SIZE = 128

F32 = jnp.float32
EPS = 1e-6
ROPE_THETA = 500000.0
N_EVEN = (DEPTH + 1) // 2
N_ODD = DEPTH // 2
S5_WIDTH = D_MODEL // 2
S5_GROUP = 16
S5_GROUPS = S5_WIDTH // S5_GROUP
S5_STATE = 64
MLA_HEADS = 8
MLA_NOPE = 64
MLA_ROPE = 32
MLA_QK = MLA_NOPE + MLA_ROPE
MLA_V = 64
Q_LORA = 384
KV_LORA = 256
MLA_BLOCK = 128
EVEN_IN = S5_WIDTH + Q_LORA + KV_LORA + MLA_ROPE
EVEN_OUT = S5_WIDTH + MLA_HEADS * MLA_V
SWA_HEADS = 16
SWA_KV = 4
SWA_GROUP = SWA_HEADS // SWA_KV
SWA_HD = 64
SWA_ROT = SWA_HD // 4
WINDOW = 128
ODD_IN = (SWA_HEADS + 2 * SWA_KV) * SWA_HD
ODD_OUT = SWA_HEADS * SWA_HD
N_EXPERTS = 16
N_GROUPS = 4
EXPERTS_PER_GROUP = N_EXPERTS // N_GROUPS
TOP_K = 2
D_FF = 256

kernel_name = 'hybrid_s5_mla_swa_moe_step'


def rmsnorm(x, g):
    xf = x.astype(F32)
    y = xf * lax.rsqrt(jnp.mean(xf * xf, axis=-1, keepdims=True) + EPS)
    return (y * g.astype(F32)).astype(x.dtype)


def rope(x, pos, rot_dim):
    half = rot_dim // 2
    inv = ROPE_THETA ** (-jnp.arange(half, dtype=F32) / half)
    ang = pos.astype(F32)[:, None] * inv[None, :]
    cos = jnp.cos(ang)[:, None, :]
    sin = jnp.sin(ang)[:, None, :]
    xf = x.astype(F32)
    x1 = xf[..., :half]
    x2 = xf[..., half:rot_dim]
    out = jnp.concatenate([x1 * cos - x2 * sin, x2 * cos + x1 * sin, xf[..., rot_dim:]], axis=-1)
    return out.astype(x.dtype)


def modulation(c, w_mod, b_mod):
    m = (jax.nn.silu(c) @ w_mod + b_mod)[:, None, :]
    return jnp.split(m, 6, axis=-1)


def adaln(x, g, shift, scale):
    return rmsnorm(x, g) * (1 + scale) + shift


def sink_softmax(s, sink):
    sink = jnp.broadcast_to(sink.astype(F32), s.shape[:-1] + (1,))
    p = jax.nn.softmax(jnp.concatenate([s, sink], axis=-1), axis=-1)
    return p[..., :-1]


def s5_discretize(lam_re, lam_im, log_step, b_re, b_im):
    dt = jnp.exp(log_step.astype(F32))[:, None]
    lr = lam_re.astype(F32)
    li = lam_im.astype(F32)
    mag = jnp.exp(lr * dt)
    ang = li * dt
    a_re = mag * jnp.cos(ang)
    a_im = mag * jnp.sin(ang)
    den = lr * lr + li * li
    k_re = ((a_re - 1) * lr + a_im * li) / den
    k_im = (a_im * lr - (a_re - 1) * li) / den
    br = b_re.astype(F32)
    bi = b_im.astype(F32)
    bb_re = k_re[..., None] * br - k_im[..., None] * bi
    bb_im = k_re[..., None] * bi + k_im[..., None] * br
    return a_re, a_im, bb_re, bb_im


def s5_combine(e1, e2):
    a1r, a1i, b1r, b1i = e1
    a2r, a2i, b2r, b2i = e2
    return (a2r * a1r - a2i * a1i, a2r * a1i + a2i * a1r,
            a2r * b1r - a2i * b1i + b2r, a2r * b1i + a2i * b1r + b2i)


def s5_mixer(u, h0, lam_re, lam_im, log_step, b_re, b_im, c_re, c_im, d_skip, w_glu):
    bsz, t = u.shape[0], u.shape[1]
    uf = u.astype(F32)
    ug = uf.reshape(bsz, t, S5_GROUPS, S5_GROUP)
    a_re, a_im, bb_re, bb_im = s5_discretize(lam_re, lam_im, log_step, b_re, b_im)
    bu_re = jnp.einsum('gnc,btgc->btgn', bb_re, ug)
    bu_im = jnp.einsum('gnc,btgc->btgn', bb_im, ug)
    if h0 is not None:
        h0r = h0[0].astype(F32)
        h0i = h0[1].astype(F32)
        bu_re = bu_re.at[:, 0].add(a_re * h0r - a_im * h0i)
        bu_im = bu_im.at[:, 0].add(a_re * h0i + a_im * h0r)
    ar = jnp.broadcast_to(a_re, bu_re.shape)
    ai = jnp.broadcast_to(a_im, bu_im.shape)
    _, _, h_re, h_im = lax.associative_scan(s5_combine, (ar, ai, bu_re, bu_im), axis=1)
    y = (jnp.einsum('gcn,btgn->btgc', c_re.astype(F32), h_re)
         - jnp.einsum('gcn,btgn->btgc', c_im.astype(F32), h_im))
    y = y.reshape(bsz, t, S5_WIDTH) + d_skip.astype(F32) * uf
    z = jax.nn.gelu(y)
    out = z * jax.nn.sigmoid(z @ w_glu.astype(F32))
    return out.astype(u.dtype), h_re[:, -1], h_im[:, -1]


def mla_project(q_a, kv_a, kr, pos, g_qa, w_uq, g_qh, g_kva, g_kr):
    c_q = rmsnorm(q_a, g_qa)
    q = jnp.einsum('btr,rhd->bthd', c_q, w_uq)
    q = rmsnorm(q, g_qh)
    q_nope = q[..., :MLA_NOPE]
    q_rope = rope(q[..., MLA_NOPE:], pos, MLA_ROPE)
    c_kv = rmsnorm(kv_a, g_kva)
    k_rope = rope(rmsnorm(kr, g_kr)[:, :, None, :], pos, MLA_ROPE)[:, :, 0]
    return q_nope, q_rope, c_kv, k_rope


def mla_prompt_attn(q_nope, q_rope, c_kv, k_rope, w_uk, w_uv):
    bsz, t = q_nope.shape[0], q_nope.shape[1]
    k_nope = jnp.einsum('btc,chd->bthd', c_kv, w_uk)
    v = jnp.einsum('btc,chd->bthd', c_kv, w_uv)
    nb = t // MLA_BLOCK
    qn = q_nope.reshape(bsz, nb, MLA_BLOCK, MLA_HEADS, MLA_NOPE).transpose(1, 0, 2, 3, 4)
    qr = q_rope.reshape(bsz, nb, MLA_BLOCK, MLA_HEADS, MLA_ROPE).transpose(1, 0, 2, 3, 4)
    kpos = jnp.arange(t)
    scale = MLA_QK ** -0.5

    def block(args):
        i, qn_b, qr_b = args
        s = (jnp.einsum('bqhd,bkhd->bhqk', qn_b, k_nope)
             + jnp.einsum('bqhr,bkr->bhqk', qr_b, k_rope)).astype(F32) * scale
        qpos = i * MLA_BLOCK + jnp.arange(MLA_BLOCK)
        s = jnp.where(kpos[None, :] <= qpos[:, None], s, -jnp.inf)
        p = jax.nn.softmax(s, axis=-1).astype(v.dtype)
        return jnp.einsum('bhqk,bkhd->bqhd', p, v)

    o = lax.map(block, (jnp.arange(nb), qn, qr))
    return o.transpose(1, 0, 2, 3, 4).reshape(bsz, t, MLA_HEADS * MLA_V)


def mla_sample_attn(q_nope, q_rope, c_new, kr_new, pool_c, pool_kr, page_table, w_uk, w_uv):
    bsz, s_len = q_nope.shape[0], q_nope.shape[1]
    q_lat = jnp.einsum('bshd,chd->bshc', q_nope, w_uk)
    causal = jnp.arange(s_len)[None, :] <= jnp.arange(s_len)[:, None]
    scale = MLA_QK ** -0.5

    def one(args):
        ql, qr, cn, krn, pages = args
        cp = pool_c[pages].reshape(-1, KV_LORA)
        krp = pool_kr[pages].reshape(-1, MLA_ROPE)
        past = cp.shape[0]
        s_past = jnp.einsum('shc,pc->hsp', ql, cp) + jnp.einsum('shr,pr->hsp', qr, krp)
        s_new = jnp.einsum('shc,kc->hsk', ql, cn) + jnp.einsum('shr,kr->hsk', qr, krn)
        s_new = jnp.where(causal, s_new.astype(F32), -jnp.inf)
        sc = jnp.concatenate([s_past.astype(F32), s_new], axis=-1) * scale
        p = jax.nn.softmax(sc, axis=-1).astype(cp.dtype)
        return (jnp.einsum('hsp,pc->shc', p[..., :past], cp)
                + jnp.einsum('hsk,kc->shc', p[..., past:], cn))

    o_lat = lax.map(one, (q_lat, q_rope, c_new, kr_new, page_table))
    return jnp.einsum('bshc,chd->bshd', o_lat, w_uv).reshape(bsz, s_len, MLA_HEADS * MLA_V)


def swa_project(z, pos, g_q, g_k):
    bsz, t = z.shape[0], z.shape[1]
    nq = SWA_HEADS * SWA_HD
    nk = SWA_KV * SWA_HD
    q = z[..., :nq].reshape(bsz, t, SWA_HEADS, SWA_HD)
    k = z[..., nq:nq + nk].reshape(bsz, t, SWA_KV, SWA_HD)
    v = z[..., nq + nk:].reshape(bsz, t, SWA_KV, SWA_HD)
    q = rope(rmsnorm(q, g_q), pos, SWA_ROT)
    k = rope(rmsnorm(k, g_k), pos, SWA_ROT)
    return q, k, v


def swa_prompt_attn(q, k, v, sinks):
    bsz, t = q.shape[0], q.shape[1]
    nb = t // WINDOW
    qb = q.reshape(bsz, nb, WINDOW, SWA_KV, SWA_GROUP, SWA_HD)
    kb = k.reshape(bsz, nb, WINDOW, SWA_KV, SWA_HD)
    vb = v.reshape(bsz, nb, WINDOW, SWA_KV, SWA_HD)
    kband = jnp.concatenate([jnp.concatenate([jnp.zeros_like(kb[:, :1]), kb[:, :-1]], 1), kb], axis=2)
    vband = jnp.concatenate([jnp.concatenate([jnp.zeros_like(vb[:, :1]), vb[:, :-1]], 1), vb], axis=2)
    s = jnp.einsum('bnqkgd,bnjkd->bnkgqj', qb, kband).astype(F32) * SWA_HD ** -0.5
    qi = jnp.arange(WINDOW)[:, None] + WINDOW
    kj = jnp.arange(2 * WINDOW)[None, :]
    blk = jnp.arange(nb)[:, None, None]
    valid = (kj <= qi) & (qi - kj < WINDOW) & (blk * WINDOW - WINDOW + kj >= 0)
    s = jnp.where(valid[None, :, None, None], s, -jnp.inf)
    p = sink_softmax(s, sinks.reshape(SWA_KV, SWA_GROUP)[:, :, None, None]).astype(v.dtype)
    o = jnp.einsum('bnkgqj,bnjkd->bnqkgd', p, vband)
    return o.reshape(bsz, t, SWA_HEADS * SWA_HD)


def swa_sample_attn(q, k_new, v_new, buf_k, buf_v, sinks, past_len):
    bsz, s_len = q.shape[0], q.shape[1]
    kall = jnp.concatenate([buf_k.astype(k_new.dtype), k_new], axis=1)
    vall = jnp.concatenate([buf_v.astype(v_new.dtype), v_new], axis=1)
    qg = q.reshape(bsz, s_len, SWA_KV, SWA_GROUP, SWA_HD)
    sc = jnp.einsum('bqkgd,bjkd->bkgqj', qg, kall).astype(F32) * SWA_HD ** -0.5
    qpos = past_len + jnp.arange(s_len)
    kpos = past_len - WINDOW + jnp.arange(WINDOW + s_len)
    valid = (kpos[None, :] <= qpos[:, None]) & (qpos[:, None] - kpos[None, :] < WINDOW)
    sc = jnp.where(valid, sc, -jnp.inf)
    p = sink_softmax(sc, sinks.reshape(SWA_KV, SWA_GROUP)[:, :, None, None]).astype(vall.dtype)
    o = jnp.einsum('bkgqj,bjkd->bqkgd', p, vall).reshape(bsz, s_len, SWA_HEADS * SWA_HD)
    return o, kall[:, -WINDOW:], vall[:, -WINDOW:]


def moe(h, w_router, b_router, w_gate, w_up, w_down):
    scores = jax.nn.sigmoid((h @ w_router).astype(F32))
    sel = scores + b_router.astype(F32)
    grp = sel.reshape(sel.shape[:-1] + (N_GROUPS, EXPERTS_PER_GROUP))
    grp_score = lax.top_k(grp, 2)[0].sum(-1)
    g_idx = jnp.argmax(grp_score, axis=-1)
    in_grp = g_idx[..., None] == jnp.arange(N_GROUPS)
    emask = jnp.repeat(in_grp, EXPERTS_PER_GROUP, axis=-1)
    _, idx = lax.top_k(jnp.where(emask, sel, -jnp.inf), TOP_K)
    w = jnp.take_along_axis(scores, idx, axis=-1)
    w = w / jnp.sum(w, axis=-1, keepdims=True)
    gate = jnp.sum(jax.nn.one_hot(idx, N_EXPERTS, dtype=F32) * w[..., None], axis=-2)
    a = jnp.einsum('btd,edf->btef', h, w_gate)
    u = jnp.einsum('btd,edf->btef', h, w_up)
    act = jax.nn.silu(a) * u * gate[..., None].astype(h.dtype)
    return jnp.einsum('btef,efd->btd', act, w_down)


def setup_inputs(seed: int = 0) -> dict:
    key = jax.random.key(seed)
    ks = iter(jax.random.split(key, 64))
    n_pages = PAST_LEN // PAGE_SIZE
    n_used = DEC_BATCH * n_pages
    n_phys = n_used + max(1, n_used // 4)

    def nrm(shape, s=1.0):
        return s * jax.random.normal(next(ks), shape, F32)

    def gain(shape):
        return 1.0 + 0.01 * jax.random.normal(next(ks), shape, F32)

    inp = {}
    inp['x_prompt'] = nrm((BATCH, SEQ, D_MODEL))
    inp['x_sample'] = nrm((DEC_BATCH, DEC_SEQ, D_MODEL))
    inp['c_prompt'] = nrm((BATCH, D_MODEL))
    inp['c_sample'] = nrm((DEC_BATCH, D_MODEL))
    inp['cache_ckv'] = nrm((N_EVEN, n_phys, PAGE_SIZE, KV_LORA))
    inp['cache_krope'] = nrm((N_EVEN, n_phys, PAGE_SIZE, MLA_ROPE))
    perm = jax.random.permutation(next(ks), n_phys)
    inp['page_table'] = perm[:n_used].reshape(DEC_BATCH, n_pages).astype(jnp.int32)
    inp['state_s5_re'] = nrm((N_EVEN, DEC_BATCH, S5_GROUPS, S5_STATE))
    inp['state_s5_im'] = nrm((N_EVEN, DEC_BATCH, S5_GROUPS, S5_STATE))
    inp['cache_win_k'] = nrm((N_ODD, DEC_BATCH, WINDOW, SWA_KV, SWA_HD))
    inp['cache_win_v'] = nrm((N_ODD, DEC_BATCH, WINDOW, SWA_KV, SWA_HD))
    inp['w_mod'] = nrm((DEPTH, D_MODEL, 6 * D_MODEL), 0.5 * D_MODEL ** -0.5)
    inp['b_mod'] = nrm((DEPTH, 6 * D_MODEL), 0.02)
    inp['g_norm_mix'] = gain((DEPTH, D_MODEL))
    inp['g_norm_ffn'] = gain((DEPTH, D_MODEL))
    inp['w_in_even'] = nrm((N_EVEN, D_MODEL, EVEN_IN), D_MODEL ** -0.5)
    inp['w_out_even'] = nrm((N_EVEN, EVEN_OUT, D_MODEL), EVEN_OUT ** -0.5)
    inp['s5_lam_re'] = -0.5 + nrm((N_EVEN, S5_GROUPS, S5_STATE), 0.01)
    inp['s5_lam_im'] = (jnp.pi * jnp.arange(S5_STATE, dtype=F32)
                        + nrm((N_EVEN, S5_GROUPS, S5_STATE), 0.01))
    inp['s5_log_step'] = jax.random.uniform(next(ks), (N_EVEN, S5_GROUPS), F32,
                                            math.log(1e-3), math.log(1e-1))
    inp['s5_b_re'] = nrm((N_EVEN, S5_GROUPS, S5_STATE, S5_GROUP), (2 * S5_GROUP) ** -0.5)
    inp['s5_b_im'] = nrm((N_EVEN, S5_GROUPS, S5_STATE, S5_GROUP), (2 * S5_GROUP) ** -0.5)
    inp['s5_c_re'] = nrm((N_EVEN, S5_GROUPS, S5_GROUP, S5_STATE), S5_STATE ** -0.5)
    inp['s5_c_im'] = nrm((N_EVEN, S5_GROUPS, S5_GROUP, S5_STATE), S5_STATE ** -0.5)
    inp['s5_d'] = nrm((N_EVEN, S5_WIDTH))
    inp['s5_w_glu'] = nrm((N_EVEN, S5_WIDTH, S5_WIDTH), S5_WIDTH ** -0.5)
    inp['mla_g_qa'] = gain((N_EVEN, Q_LORA))
    inp['mla_w_uq'] = nrm((N_EVEN, Q_LORA, MLA_HEADS, MLA_QK), Q_LORA ** -0.5)
    inp['mla_g_qh'] = gain((N_EVEN, MLA_QK))
    inp['mla_g_kva'] = gain((N_EVEN, KV_LORA))
    inp['mla_g_kr'] = gain((N_EVEN, MLA_ROPE))
    inp['mla_w_uk'] = nrm((N_EVEN, KV_LORA, MLA_HEADS, MLA_NOPE), KV_LORA ** -0.5)
    inp['mla_w_uv'] = nrm((N_EVEN, KV_LORA, MLA_HEADS, MLA_V), KV_LORA ** -0.5)
    inp['w_in_odd'] = nrm((N_ODD, D_MODEL, ODD_IN), D_MODEL ** -0.5)
    inp['w_out_odd'] = nrm((N_ODD, ODD_OUT, D_MODEL), ODD_OUT ** -0.5)
    inp['swa_g_q'] = gain((N_ODD, SWA_HD))
    inp['swa_g_k'] = gain((N_ODD, SWA_HD))
    inp['swa_sinks'] = nrm((N_ODD, SWA_HEADS))
    inp['w_router'] = nrm((D_MODEL, N_EXPERTS), D_MODEL ** -0.5)
    inp['b_router'] = nrm((N_EXPERTS,), 0.01)
    inp['moe_w_gate'] = nrm((DEPTH, N_EXPERTS, D_MODEL, D_FF), D_MODEL ** -0.5)
    inp['moe_w_up'] = nrm((DEPTH, N_EXPERTS, D_MODEL, D_FF), D_MODEL ** -0.5)
    inp['moe_w_down'] = nrm((DEPTH, N_EXPERTS, D_FF, D_MODEL), D_FF ** -0.5)
    return inp


def reference(x_prompt, x_sample, c_prompt, c_sample, cache_ckv, cache_krope, page_table,
              state_s5_re, state_s5_im, cache_win_k, cache_win_v,
              w_mod, b_mod, g_norm_mix, g_norm_ffn, w_in_even, w_out_even,
              s5_lam_re, s5_lam_im, s5_log_step, s5_b_re, s5_b_im, s5_c_re, s5_c_im, s5_d, s5_w_glu,
              mla_g_qa, mla_w_uq, mla_g_qh, mla_g_kva, mla_g_kr, mla_w_uk, mla_w_uv,
              w_in_odd, w_out_odd, swa_g_q, swa_g_k, swa_sinks,
              w_router, b_router, moe_w_gate, moe_w_up, moe_w_down):
    past_len = page_table.shape[1] * PAGE_SIZE

    def run(x, c, pos, sample):
        ckv_l, kr_l, sr_l, si_l, wk_l, wv_l = [], [], [], [], [], []
        for i in range(DEPTH):
            j = i // 2
            sh1, sc1, gt1, sh2, sc2, gt2 = modulation(c, w_mod[i], b_mod[i])
            h = adaln(x, g_norm_mix[i], sh1, sc1)
            if i % 2 == 0:
                z = h @ w_in_even[j]
                o1 = S5_WIDTH
                o2 = o1 + Q_LORA
                o3 = o2 + KV_LORA
                u, q_a, kv_a, kr = z[..., :o1], z[..., o1:o2], z[..., o2:o3], z[..., o3:]
                h0 = (state_s5_re[j], state_s5_im[j]) if sample else None
                y_ssm, s_re, s_im = s5_mixer(u, h0, s5_lam_re[j], s5_lam_im[j], s5_log_step[j],
                                             s5_b_re[j], s5_b_im[j], s5_c_re[j], s5_c_im[j],
                                             s5_d[j], s5_w_glu[j])
                q_nope, q_rope, c_kv, k_rope = mla_project(q_a, kv_a, kr, pos, mla_g_qa[j], mla_w_uq[j],
                                                           mla_g_qh[j], mla_g_kva[j], mla_g_kr[j])
                if sample:
                    y_att = mla_sample_attn(q_nope, q_rope, c_kv, k_rope, cache_ckv[j], cache_krope[j],
                                            page_table, mla_w_uk[j], mla_w_uv[j])
                else:
                    y_att = mla_prompt_attn(q_nope, q_rope, c_kv, k_rope, mla_w_uk[j], mla_w_uv[j])
                mix = jnp.concatenate([y_ssm.astype(x.dtype), y_att.astype(x.dtype)], axis=-1) @ w_out_even[j]
                ckv_l.append(c_kv)
                kr_l.append(k_rope)
                sr_l.append(s_re)
                si_l.append(s_im)
            else:
                z = h @ w_in_odd[j]
                q, k, v = swa_project(z, pos, swa_g_q[j], swa_g_k[j])
                if sample:
                    o, kb, vb = swa_sample_attn(q, k, v, cache_win_k[j], cache_win_v[j], swa_sinks[j], past_len)
                else:
                    o = swa_prompt_attn(q, k, v, swa_sinks[j])
                    kb, vb = k[:, -WINDOW:], v[:, -WINDOW:]
                mix = o @ w_out_odd[j]
                wk_l.append(kb)
                wv_l.append(vb)
            x = x + gt1 * mix
            h = adaln(x, g_norm_ffn[i], sh2, sc2)
            x = x + gt2 * moe(h, w_router, b_router, moe_w_gate[i], moe_w_up[i], moe_w_down[i])
        return (x, jnp.stack(ckv_l), jnp.stack(kr_l), jnp.stack(sr_l), jnp.stack(si_l),
                jnp.stack(wk_l), jnp.stack(wv_l))

    y_p, ckv_p, kr_p, sr_p, si_p, wk_p, wv_p = run(x_prompt, c_prompt, jnp.arange(x_prompt.shape[1]), False)
    y_s, ckv_s, kr_s, sr_s, si_s, wk_s, wv_s = run(x_sample, c_sample,
                                                   past_len + jnp.arange(x_sample.shape[1]), True)
    return (y_p, y_s, ckv_p, kr_p, sr_p, si_p, wk_p, wv_p, ckv_s, kr_s, sr_s, si_s, wk_s, wv_s)
```

```python
import functools
import math

import jax
import jax.numpy as jnp
from jax import lax
from jax.experimental import pallas as pl
from jax.experimental.pallas import tpu as pltpu

F32 = jnp.float32
BF16 = jnp.bfloat16
EPS = 1e-6
ROPE_THETA = 500000.0
LANES = 128
SUBLANES = 8
VMEM_LIMIT = 56 * 1024 * 1024

D_MODEL = 1024
PAGE = 128
S5_WIDTH = 512
S5_GROUP = 16
S5_GROUPS = 32
S5_STATE = 64
S5_LANES = S5_GROUPS * S5_STATE
S5_CHUNKS = S5_LANES // LANES
MLA_HEADS = 8
MLA_NOPE = 64
MLA_ROPE = 32
MLA_QK = 96
MLA_V = 64
Q_LORA = 384
KV_LORA = 256
SWA_HEADS = 16
SWA_KV = 4
SWA_GROUP = 4
SWA_HD = 64
SWA_ROT = 16
WINDOW = 128
N_EXPERTS = 16
GROUP_SIZE = 4
D_FF = 256

_NT = (((1,), (1,)), ((), ()))


def _dot(a, b):
    return jnp.dot(a, b, preferred_element_type=F32)


def _dot_nt(a, b):
    return lax.dot_general(a, b, _NT, preferred_element_type=F32)


def _rms(x, g, n=None):
    n = x.shape[-1] if n is None else n
    ss = jnp.sum(x * x, axis=-1, keepdims=True) * (1.0 / n)
    return x * lax.rsqrt(ss + EPS) * g


def _rope(x, shift, rc, rs1, rs2):
    return (x * rc + pltpu.roll(x, LANES - shift, axis=1) * rs1
            + pltpu.roll(x, shift, axis=1) * rs2)


def _cparams(sem=None, vmem=VMEM_LIMIT):
    return pltpu.CompilerParams(dimension_semantics=sem, vmem_limit_bytes=vmem)


def _full(shape):
    nd = len(shape)
    return pl.BlockSpec(shape, lambda *_: (0,) * nd)


def _rows(tile, width, per_row):
    if per_row:
        return pl.BlockSpec((tile, width), lambda i: (i, 0))
    return pl.BlockSpec((1, width), lambda i: (0, 0))


def _mod_kernel(c_ref, w_ref, b_ref, o_ref):
    c = c_ref[...]
    s = (c * jax.nn.sigmoid(c)).astype(BF16)
    o_ref[0] = _dot(s, w_ref[0].astype(BF16)) + b_ref[0]


def _modulation(c_all, w_mod, b_mod):
    depth, _, n = w_mod.shape
    rp = c_all.shape[0]
    tn = 1536
    return pl.pallas_call(
        _mod_kernel,
        grid=(depth, n // tn),
        in_specs=[pl.BlockSpec((rp, D_MODEL), lambda l, j: (0, 0)),
                  pl.BlockSpec((1, D_MODEL, tn), lambda l, j: (l, 0, j)),
                  pl.BlockSpec((1, 1, tn), lambda l, j: (l, 0, j))],
        out_specs=pl.BlockSpec((1, rp, tn), lambda l, j: (l, 0, j)),
        out_shape=jax.ShapeDtypeStruct((depth, rp, n), F32),
        compiler_params=_cparams(("arbitrary", "arbitrary")),
        name="modulation",
    )(c_all, w_mod, b_mod.reshape(depth, 1, n))


def _front_even_kernel(x_ref, sh_ref, sc_ref, g_ref, win_ref, gqa_ref, wuq_ref, gqh_ref, gkva_ref,
                       gkr_ref, wuk_ref, wuv_ref, rc_ref, rs1_ref, rs2_ref,
                       u_ref, q_ref, k_ref, v_ref, ckv_ref, kr_ref):
    x = x_ref[...]
    h = _rms(x, g_ref[...]) * (1.0 + sc_ref[...]) + sh_ref[...]
    z = _dot(h.astype(BF16), win_ref[...])
    u_ref[...] = z[:, :512]
    c_q = _rms(z[:, 512:896], gqa_ref[...]).astype(BF16)
    c_kv = _rms(z[:, 896:1152], gkva_ref[...])
    ckv_ref[...] = c_kv
    rc, rs1, rs2 = rc_ref[...], rs1_ref[...], rs2_ref[...]
    kr = _rope(_rms(z[:, 1152:1280], gkr_ref[...], MLA_ROPE), MLA_ROPE // 2, rc, rs1, rs2)
    kr_ref[...] = kr[:, MLA_NOPE:MLA_QK]
    ckvb = c_kv.astype(BF16)
    kn = _dot(ckvb, wuk_ref[...])
    v_ref[...] = _dot(ckvb, wuv_ref[...]).astype(BF16)
    scale = MLA_QK ** -0.5
    for hd in range(MLA_HEADS):
        sl = slice(hd * LANES, (hd + 1) * LANES)
        qh = _dot(c_q, wuq_ref[hd])
        qn = _rope(_rms(qh, gqh_ref[...], MLA_QK), MLA_ROPE // 2, rc, rs1, rs2)
        q_ref[:, sl] = (qn * scale).astype(BF16)
        k_ref[:, sl] = (kn[:, sl] + kr).astype(BF16)


def _front_even(x, sh, sc, g, wts, tabs, tile):
    r = x.shape[0]
    per_row_mod = sh.shape[0] != 1
    per_row_tab = tabs[0].shape[0] != 1
    win, gqa, wuq, gqh, gkva, gkr, wuk, wuv = wts
    in_specs = [pl.BlockSpec((tile, D_MODEL), lambda i: (i, 0)),
                _rows(tile, D_MODEL, per_row_mod), _rows(tile, D_MODEL, per_row_mod),
                _full(g.shape), _full(win.shape), _full(gqa.shape), _full(wuq.shape), _full(gqh.shape),
                _full(gkva.shape), _full(gkr.shape), _full(wuk.shape), _full(wuv.shape),
                _rows(tile, LANES, per_row_tab), _rows(tile, LANES, per_row_tab), _rows(tile, LANES, per_row_tab)]
    widths = [(S5_WIDTH, F32), (MLA_HEADS * LANES, BF16), (MLA_HEADS * LANES, BF16),
              (MLA_HEADS * MLA_V, BF16), (KV_LORA, F32), (MLA_ROPE, F32)]
    return pl.pallas_call(
        _front_even_kernel,
        grid=(r // tile,),
        in_specs=in_specs,
        out_specs=[pl.BlockSpec((tile, w), lambda i: (i, 0)) for w, _ in widths],
        out_shape=[jax.ShapeDtypeStruct((r, w), dt) for w, dt in widths],
        compiler_params=_cparams(("arbitrary",)),
        name="front_even",
    )(x, sh, sc, g, win, gqa, wuq, gqh, gkva, gkr, wuk, wuv, *tabs)


def _s5_disc_kernel(lr_ref, li_ref, ls_ref, br_ref, bi_ref, are_ref, aim_ref, bbr_ref, bbi_ref,
                    alr_ref, ali_ref, *, log2_len):
    lr, li = lr_ref[...], li_ref[...]
    dt = jnp.exp(ls_ref[...])
    mag = jnp.exp(lr * dt)
    ang = li * dt
    a_re = mag * jnp.cos(ang)
    a_im = mag * jnp.sin(ang)
    den = lr * lr + li * li
    k_re = ((a_re - 1.0) * lr + a_im * li) / den
    k_im = (a_im * lr - (a_re - 1.0) * li) / den
    are_ref[...] = a_re
    aim_ref[...] = a_im
    for c in range(S5_GROUP):
        br, bi = br_ref[c], bi_ref[c]
        bbr_ref[c] = k_re * br - k_im * bi
        bbi_ref[c] = k_re * bi + k_im * br
    pr, pi = a_re, a_im
    for _ in range(log2_len):
        pr, pi = pr * pr - pi * pi, 2.0 * pr * pi
    alr_ref[...] = pr
    ali_ref[...] = pi


def _s5_discretize(lam_re, lam_im, log_step, b_re, b_im, seg_len):
    g, n = lam_re.shape
    outs = [jax.ShapeDtypeStruct((g, n), F32)] * 2 + [jax.ShapeDtypeStruct((S5_GROUP, g, n), F32)] * 2 \
        + [jax.ShapeDtypeStruct((g, n), F32)] * 2
    return pl.pallas_call(
        functools.partial(_s5_disc_kernel, log2_len=int(math.log2(seg_len))),
        out_shape=outs,
        name="s5_discretize",
    )(lam_re, lam_im, log_step.reshape(g, 1), b_re.transpose(2, 0, 1), b_im.transpose(2, 0, 1))


def _s5_layouts(bb_re, bb_im, c_re, c_im):
    eye8 = jnp.eye(8, dtype=F32)

    def bmat(bb):
        b4 = bb.reshape(S5_GROUP, 4, 8, S5_STATE)
        m = jnp.einsum('cjgn,gh->jgchn', b4, eye8)
        return m.reshape(4, LANES, 512).astype(BF16)

    sel = jax.nn.one_hot((2 * jnp.arange(S5_CHUNKS)[:, None] + jnp.arange(2)[None, :]) % 8, 8, dtype=F32)

    def cmat(c):
        c4 = c.reshape(S5_CHUNKS, 2, S5_GROUP, S5_STATE)
        m = jnp.einsum('asck,asg->askgc', c4, sel)
        return m.reshape(S5_CHUNKS, LANES, LANES).astype(BF16)

    return bmat(bb_re), bmat(bb_im), cmat(c_re), cmat(-c_im)


def _gelu_glu(y, wglu_ref):
    z = jax.nn.gelu(y)
    return z * jax.nn.sigmoid(_dot(z.astype(BF16), wglu_ref[...]))


def _s5_scan_kernel(u_ref, bre_ref, bim_ref, are_ref, aim_ref, alr_ref, ali_ref, cre_ref, cim_ref,
                    d_ref, wglu_ref, y_ref, st_ref, bu_ref, hs_ref, carry_ref, *, seg_len):
    i = pl.program_id(0)

    @pl.when(i == 0)
    def _():
        carry_ref[...] = jnp.zeros_like(carry_ref)

    u = u_ref[...]
    ub = u.astype(BF16)
    for j in range(4):
        uj = ub[:, j * LANES:(j + 1) * LANES]
        re = _dot(uj, bre_ref[j])
        im = _dot(uj, bim_ref[j])
        for q in range(4):
            bu_ref[4 * j + q] = re[:, q * LANES:(q + 1) * LANES]
            bu_ref[S5_CHUNKS + 4 * j + q] = im[:, q * LANES:(q + 1) * LANES]

    per = 4
    for grp in range(S5_CHUNKS // per):
        cs = [grp * per + q for q in range(per)]
        ar = [jnp.broadcast_to(are_ref[c], (SUBLANES, LANES)) for c in cs]
        ai = [jnp.broadcast_to(aim_ref[c], (SUBLANES, LANES)) for c in cs]

        def advance(t, hs, store):
            out = []
            for k, c in enumerate(cs):
                hr, hi = hs[2 * k], hs[2 * k + 1]
                rows = pl.ds(t, SUBLANES, stride=seg_len)
                nr = ar[k] * hr - ai[k] * hi + bu_ref[c, rows, :]
                ni = ar[k] * hi + ai[k] * hr + bu_ref[S5_CHUNKS + c, rows, :]
                if store:
                    hs_ref[c, rows, :] = nr
                    hs_ref[S5_CHUNKS + c, rows, :] = ni
                out += [nr, ni]
            return tuple(out)

        zero = tuple(jnp.zeros((SUBLANES, LANES), F32) for _ in range(2 * per))
        ends = lax.fori_loop(0, seg_len, lambda t, hs: advance(t, hs, False), zero, unroll=2)
        init = []
        for k, c in enumerate(cs):
            er, ei = ends[2 * k], ends[2 * k + 1]
            lr, li = alr_ref[c], ali_ref[c]
            hr, hi = carry_ref[c], carry_ref[S5_CHUNKS + c]
            rows_r, rows_i = [], []
            for s in range(SUBLANES):
                rows_r.append(hr)
                rows_i.append(hi)
                hr, hi = (er[s:s + 1] + lr * hr - li * hi, ei[s:s + 1] + lr * hi + li * hr)
            carry_ref[c] = hr
            carry_ref[S5_CHUNKS + c] = hi
            init += [jnp.concatenate(rows_r, axis=0), jnp.concatenate(rows_i, axis=0)]
        lax.fori_loop(0, seg_len, lambda t, hs: advance(t, hs, True), tuple(init), unroll=2)

    ys = []
    for j in range(4):
        acc = None
        for q in range(4):
            c = 4 * j + q
            t = _dot(hs_ref[c].astype(BF16), cre_ref[c]) + _dot(hs_ref[S5_CHUNKS + c].astype(BF16), cim_ref[c])
            acc = t if acc is None else acc + t
        ys.append(acc)
    y = jnp.concatenate(ys, axis=1) + d_ref[...] * u
    y_ref[...] = _gelu_glu(y, wglu_ref).astype(BF16)

    @pl.when(i == pl.num_programs(0) - 1)
    def _():
        st_ref[...] = carry_ref[...]


def _s5_scan(u, bre, bim, a_re, a_im, al_re, al_im, cre, cim, d, wglu, tile, seg_len):
    t = u.shape[0]
    ch = lambda a: a.reshape(S5_CHUNKS, 1, LANES)
    args = (u, bre, bim, ch(a_re), ch(a_im), ch(al_re), ch(al_im), cre, cim, d, wglu)
    in_specs = [pl.BlockSpec((tile, S5_WIDTH), lambda i: (i, 0))] + [_full(a.shape) for a in args[1:]]
    return pl.pallas_call(
        functools.partial(_s5_scan_kernel, seg_len=seg_len),
        grid=(t // tile,),
        in_specs=in_specs,
        out_specs=[pl.BlockSpec((tile, S5_WIDTH), lambda i: (i, 0)), _full((2 * S5_CHUNKS, 1, LANES))],
        out_shape=[jax.ShapeDtypeStruct((t, S5_WIDTH), BF16),
                   jax.ShapeDtypeStruct((2 * S5_CHUNKS, 1, LANES), F32)],
        scratch_shapes=[pltpu.VMEM((2 * S5_CHUNKS, tile, LANES), F32),
                        pltpu.VMEM((2 * S5_CHUNKS, tile, LANES), F32),
                        pltpu.VMEM((2 * S5_CHUNKS, 1, LANES), F32)],
        compiler_params=_cparams(("arbitrary",)),
        name="s5_scan",
    )(*args)


def _s5_step_kernel(u_ref, h0r_ref, h0i_ref, bre_ref, bim_ref, are_ref, aim_ref, cre_ref, cim_ref,
                    d_ref, wglu_ref, y_ref, sr_ref, si_ref):
    u = u_ref[...]
    ub = u.astype(BF16)
    ys = []
    for j in range(4):
        uj = ub[:, j * LANES:(j + 1) * LANES]
        re = _dot(uj, bre_ref[j])
        im = _dot(uj, bim_ref[j])
        acc = None
        for q in range(4):
            sl = slice((4 * j + q) * LANES, (4 * j + q + 1) * LANES)
            ar, ai = are_ref[:, sl], aim_ref[:, sl]
            h0r, h0i = h0r_ref[:, sl], h0i_ref[:, sl]
            hr = ar * h0r - ai * h0i + re[:, q * LANES:(q + 1) * LANES]
            hi = ar * h0i + ai * h0r + im[:, q * LANES:(q + 1) * LANES]
            sr_ref[:, sl] = hr
            si_ref[:, sl] = hi
            t = _dot(hr.astype(BF16), cre_ref[4 * j + q]) + _dot(hi.astype(BF16), cim_ref[4 * j + q])
            acc = t if acc is None else acc + t
        ys.append(acc)
    y = jnp.concatenate(ys, axis=1) + d_ref[...] * u
    y_ref[...] = _gelu_glu(y, wglu_ref).astype(BF16)


def _s5_step(u, h0r, h0i, bre, bim, a_re, a_im, cre, cim, d, wglu):
    b = u.shape[0]
    return pl.pallas_call(
        _s5_step_kernel,
        out_shape=[jax.ShapeDtypeStruct((b, S5_WIDTH), BF16), jax.ShapeDtypeStruct((b, S5_LANES), F32),
                   jax.ShapeDtypeStruct((b, S5_LANES), F32)],
        compiler_params=_cparams(),
        name="s5_step",
    )(u, h0r, h0i, bre, bim, a_re.reshape(1, S5_LANES), a_im.reshape(1, S5_LANES), cre, cim, d, wglu)


def _mla_prompt_kernel(iq_ref, jk_ref, q_ref, k_ref, v_ref, o_ref, m_ref, l_ref, acc_ref):
    p = pl.program_id(1)
    i, j = iq_ref[p], jk_ref[p]
    tq, tk = q_ref.shape[0], k_ref.shape[0]
    low = lax.broadcasted_iota(jnp.int32, (1, LANES), 1) < MLA_V

    @pl.when(j == 0)
    def _():
        m_ref[...] = jnp.full_like(m_ref, -jnp.inf)
        l_ref[...] = jnp.zeros_like(l_ref)
        acc_ref[...] = jnp.zeros_like(acc_ref)

    def block(masked):
        v = v_ref[...]
        parts = []
        for hh in range(2):
            sl = slice(hh * LANES, (hh + 1) * LANES)
            s = _dot_nt(q_ref[:, sl], k_ref[:, sl])
            if masked:
                row = lax.broadcasted_iota(jnp.int32, (tq, tk), 0)
                col = lax.broadcasted_iota(jnp.int32, (tq, tk), 1)
                s = jnp.where(col <= row, s, -jnp.inf)
            m_prev = m_ref[hh]
            m_new = jnp.maximum(m_prev, jnp.max(s, axis=-1, keepdims=True))
            alpha = jnp.exp(m_prev - m_new)
            pr = jnp.exp(s - m_new)
            l_ref[hh] = alpha * l_ref[hh] + jnp.sum(pr, axis=-1, keepdims=True)
            m_ref[hh] = m_new
            parts.append((alpha, _dot(pr.astype(BF16), v)))
        acc = acc_ref[...]
        acc_ref[...] = jnp.where(low, parts[0][0] * acc + parts[0][1], parts[1][0] * acc + parts[1][1])

    @pl.when(j < i)
    def _():
        block(False)

    @pl.when(j == i)
    def _():
        block(True)
        inv = jnp.where(low, 1.0 / l_ref[0], 1.0 / l_ref[1])
        o_ref[...] = (acc_ref[...] * inv).astype(BF16)


def _mla_prompt_attn(q, k, v, tile):
    t = q.shape[0]
    nq = t // tile
    pairs = [(i, j) for i in range(nq) for j in range(i + 1)]
    iq = jnp.asarray([p[0] for p in pairs], jnp.int32)
    jk = jnp.asarray([p[1] for p in pairs], jnp.int32)
    grid_spec = pltpu.PrefetchScalarGridSpec(
        num_scalar_prefetch=2,
        grid=(MLA_HEADS // 2, len(pairs)),
        in_specs=[pl.BlockSpec((tile, 2 * LANES), lambda h, p, iq, jk: (iq[p], h)),
                  pl.BlockSpec((tile, 2 * LANES), lambda h, p, iq, jk: (jk[p], h)),
                  pl.BlockSpec((tile, LANES), lambda h, p, iq, jk: (jk[p], h))],
        out_specs=pl.BlockSpec((tile, LANES), lambda h, p, iq, jk: (iq[p], h)),
        scratch_shapes=[pltpu.VMEM((2, tile, 1), F32), pltpu.VMEM((2, tile, 1), F32),
                        pltpu.VMEM((tile, LANES), F32)],
    )
    return pl.pallas_call(
        _mla_prompt_kernel,
        grid_spec=grid_spec,
        out_shape=jax.ShapeDtypeStruct((t, MLA_HEADS * MLA_V), BF16),
        compiler_params=_cparams(("arbitrary", "arbitrary")),
        name="mla_prompt_attn",
    )(iq, jk, q, k, v)


def _qlat_kernel(q_ref, wukt_ref, o_ref):
    for hd in range(MLA_HEADS):
        o_ref[hd] = _dot(q_ref[:, hd * LANES:(hd + 1) * LANES], wukt_ref[hd]).astype(BF16)


def _uv_kernel(o_ref, w_ref, y_ref):
    y_ref[...] = _dot(o_ref[...], w_ref[...]).astype(BF16)


def _mla_sample_kernel(pt_ref, ql_ref, qr_ref, cn_ref, kn_ref, ckv_hbm, kr_hbm, o_ref,
                       cbuf, kbuf, sem, m_ref, l_ref, acc_ref, *, ppc, nchunks):
    b, c = pl.program_id(0), pl.program_id(1)
    nb = pl.num_programs(0)
    step = b * nchunks + c
    slot = lax.rem(step, 2)

    def copies(bb, cc, sl):
        out = []
        for p in range(ppc):
            page = pt_ref[bb, cc * ppc + p]
            rows = pl.ds(p * PAGE, PAGE)
            out.append(pltpu.make_async_copy(ckv_hbm.at[0, page], cbuf.at[sl, rows], sem.at[sl, 0]))
            out.append(pltpu.make_async_copy(kr_hbm.at[0, page], kbuf.at[sl, rows], sem.at[sl, 1]))
        return out

    @pl.when(step == 0)
    def _():
        for cp in copies(b, c, slot):
            cp.start()

    @pl.when(step + 1 < nb * nchunks)
    def _():
        last = c + 1 == nchunks
        for cp in copies(jnp.where(last, b + 1, b), jnp.where(last, 0, c + 1), 1 - slot):
            cp.start()

    ql = ql_ref[0]
    qr = qr_ref[0]

    @pl.when(c == 0)
    def _():
        cn = cn_ref[0].astype(BF16).astype(F32)
        kn = kn_ref[0].astype(BF16).astype(F32)
        s_new = (jnp.sum(ql.astype(F32) * cn, axis=-1, keepdims=True)
                 + jnp.sum(qr.astype(F32) * kn, axis=-1, keepdims=True))
        m_ref[...] = s_new
        l_ref[...] = jnp.ones_like(l_ref)
        acc_ref[...] = jnp.broadcast_to(cn, acc_ref.shape)

    for cp in copies(b, c, slot):
        cp.wait()

    cpg = cbuf[slot].astype(BF16)
    krp = kbuf[slot].astype(BF16)
    s = _dot_nt(ql, cpg) + _dot_nt(qr, krp)
    m_prev = m_ref[...]
    m_new = jnp.maximum(m_prev, jnp.max(s, axis=-1, keepdims=True))
    alpha = jnp.exp(m_prev - m_new)
    pr = jnp.exp(s - m_new)
    l_ref[...] = alpha * l_ref[...] + jnp.sum(pr, axis=-1, keepdims=True)
    m_ref[...] = m_new
    acc_ref[...] = alpha * acc_ref[...] + _dot(pr.astype(BF16), cpg)

    @pl.when(c == nchunks - 1)
    def _():
        o_ref[0] = (acc_ref[...] / l_ref[...]).astype(BF16)


def _mla_sample_attn(q, c_new, kr_new, cache_ckv, cache_krope, page_table, wukt, wuv_blk, ppc):
    b = q.shape[0]
    npages = page_table.shape[1]
    nchunks = npages // ppc
    qlat = pl.pallas_call(
        _qlat_kernel,
        out_shape=jax.ShapeDtypeStruct((MLA_HEADS, b, KV_LORA), BF16),
        name="mla_qlat",
    )(q, wukt)
    ql = qlat.transpose(1, 0, 2)
    qr = q.reshape(b, MLA_HEADS, LANES)[:, :, MLA_NOPE:MLA_QK]
    n = ppc * PAGE
    grid_spec = pltpu.PrefetchScalarGridSpec(
        num_scalar_prefetch=1,
        grid=(b, nchunks),
        in_specs=[pl.BlockSpec((1, MLA_HEADS, KV_LORA), lambda i, c, pt: (i, 0, 0)),
                  pl.BlockSpec((1, MLA_HEADS, MLA_ROPE), lambda i, c, pt: (i, 0, 0)),
                  pl.BlockSpec((1, 1, KV_LORA), lambda i, c, pt: (i, 0, 0)),
                  pl.BlockSpec((1, 1, MLA_ROPE), lambda i, c, pt: (i, 0, 0)),
                  pl.BlockSpec(memory_space=pl.ANY),
                  pl.BlockSpec(memory_space=pl.ANY)],
        out_specs=pl.BlockSpec((1, MLA_HEADS, KV_LORA), lambda i, c, pt: (i, 0, 0)),
        scratch_shapes=[pltpu.VMEM((2, n, KV_LORA), F32), pltpu.VMEM((2, n, MLA_ROPE), F32),
                        pltpu.SemaphoreType.DMA((2, 2)),
                        pltpu.VMEM((MLA_HEADS, 1), F32), pltpu.VMEM((MLA_HEADS, 1), F32),
                        pltpu.VMEM((MLA_HEADS, KV_LORA), F32)],
    )
    o_lat = pl.pallas_call(
        functools.partial(_mla_sample_kernel, ppc=ppc, nchunks=nchunks),
        grid_spec=grid_spec,
        out_shape=jax.ShapeDtypeStruct((b, MLA_HEADS, KV_LORA), BF16),
        compiler_params=_cparams(("arbitrary", "arbitrary")),
        name="mla_sample_attn",
    )(page_table, ql, qr, c_new.reshape(b, 1, KV_LORA), kr_new.reshape(b, 1, MLA_ROPE), cache_ckv, cache_krope)
    return pl.pallas_call(
        _uv_kernel,
        out_shape=jax.ShapeDtypeStruct((b, MLA_HEADS * MLA_V), BF16),
        name="mla_sample_uv",
    )(o_lat.reshape(b, MLA_HEADS * KV_LORA), wuv_blk)


def _router_gates(h2, wr_ref, br_ref):
    logits = jnp.dot(h2, wr_ref[...], preferred_element_type=F32, precision=lax.Precision.HIGHEST)
    scores = jax.nn.sigmoid(logits)
    sel = scores + br_ref[...]
    lane = lax.broadcasted_iota(jnp.int32, (1, LANES), 1)
    pos = lane % GROUP_SIZE
    others, wrapped = [], []
    for r in range(1, GROUP_SIZE):
        wrap = pos + r >= GROUP_SIZE
        fwd = pltpu.roll(sel, LANES - r, axis=1)
        bwd = pltpu.roll(sel, GROUP_SIZE - r, axis=1)
        others.append(jnp.where(wrap, bwd, fwd))
        wrapped.append(wrap)
    a, b, c, d = sel, others[0], others[1], others[2]
    hi1, lo1 = jnp.maximum(a, b), jnp.minimum(a, b)
    hi2, lo2 = jnp.maximum(c, d), jnp.minimum(c, d)
    gscore = jnp.maximum(hi1, hi2) + jnp.maximum(jnp.minimum(hi1, hi2), jnp.maximum(lo1, lo2))
    real = lane < N_EXPERTS
    gscore = jnp.where(real, gscore, -jnp.inf)
    gmax = jnp.max(gscore, axis=-1, keepdims=True)
    gidx = (lane // GROUP_SIZE).astype(F32)
    chosen = jnp.min(jnp.where(gscore == gmax, gidx, float(LANES)), axis=-1, keepdims=True)
    rank = jnp.zeros(sel.shape, F32)
    for o, wrap in zip(others, wrapped):
        rank = rank + jnp.where(wrap, jnp.where(o >= sel, 1.0, 0.0), jnp.where(o > sel, 1.0, 0.0))
    w = jnp.where(gidx == chosen, jnp.where(rank < 2.0, scores, 0.0), 0.0)
    return w / jnp.sum(w, axis=-1, keepdims=True)


def _post_kernel(*refs, n_mix):
    x_ref = refs[0]
    ys = refs[1:1 + n_mix]
    ws = refs[1 + n_mix:1 + 2 * n_mix]
    gt_ref, sh_ref, sc_ref, g_ref, wr_ref, br_ref, x1_ref, h2_ref, gate_ref = refs[1 + 2 * n_mix:]
    mix = None
    for y_ref, w_ref in zip(ys, ws):
        t = _dot(y_ref[...], w_ref[...])
        mix = t if mix is None else mix + t
    x1 = x_ref[...] + gt_ref[...] * mix
    x1_ref[...] = x1
    h2 = _rms(x1, g_ref[...]) * (1.0 + sc_ref[...]) + sh_ref[...]
    h2_ref[...] = h2.astype(BF16)
    gate_ref[...] = _router_gates(h2, wr_ref, br_ref)


def _post(x, ys, ws, gt, sh, sc, g, wr, br, tile):
    r = x.shape[0]
    per_row = gt.shape[0] != 1
    n_mix = len(ys)
    in_specs = ([pl.BlockSpec((tile, D_MODEL), lambda i: (i, 0))]
                + [pl.BlockSpec((tile, y.shape[1]), lambda i: (i, 0)) for y in ys]
                + [_full(w.shape) for w in ws]
                + [_rows(tile, D_MODEL, per_row)] * 3
                + [_full(g.shape), _full(wr.shape), _full(br.shape)])
    widths = [(D_MODEL, F32), (D_MODEL, BF16), (LANES, F32)]
    return pl.pallas_call(
        functools.partial(_post_kernel, n_mix=n_mix),
        grid=(r // tile,),
        in_specs=in_specs,
        out_specs=[pl.BlockSpec((tile, w), lambda i: (i, 0)) for w, _ in widths],
        out_shape=[jax.ShapeDtypeStruct((r, w), dt) for w, dt in widths],
        compiler_params=_cparams(("arbitrary",)),
        name="post_mixer",
    )(x, *ys, *ws, gt, sh, sc, g, wr, br)


def _moe_kernel(h_ref, gate_ref, x1_ref, gt_ref, wg_ref, wu_ref, wd_ref, o_ref, acc_ref):
    e = pl.program_id(1)

    @pl.when(e == 0)
    def _():
        acc_ref[...] = jnp.zeros_like(acc_ref)

    h = h_ref[...]
    a = _dot(h, wg_ref[0])
    u = _dot(h, wu_ref[0])
    lane = lax.broadcasted_iota(jnp.int32, (1, LANES), 1)
    gcol = jnp.sum(jnp.where(lane == e, gate_ref[...], 0.0), axis=-1, keepdims=True)
    act = (a * jax.nn.sigmoid(a)) * u * gcol
    acc_ref[...] += _dot(act.astype(BF16), wd_ref[0])

    @pl.when(e == pl.num_programs(1) - 1)
    def _():
        o_ref[...] = x1_ref[...] + gt_ref[...] * acc_ref[...]


def _moe(h2, gate, x1, gt, wg, wu, wd, tile):
    r = h2.shape[0]
    per_row = gt.shape[0] != 1
    gt_spec = (pl.BlockSpec((tile, D_MODEL), lambda i, e: (i, 0)) if per_row
               else pl.BlockSpec((1, D_MODEL), lambda i, e: (0, 0)))
    return pl.pallas_call(
        _moe_kernel,
        grid=(r // tile, N_EXPERTS),
        in_specs=[pl.BlockSpec((tile, D_MODEL), lambda i, e: (i, 0)),
                  pl.BlockSpec((tile, LANES), lambda i, e: (i, 0)),
                  pl.BlockSpec((tile, D_MODEL), lambda i, e: (i, 0)),
                  gt_spec,
                  pl.BlockSpec((1, D_MODEL, D_FF), lambda i, e: (e, 0, 0)),
                  pl.BlockSpec((1, D_MODEL, D_FF), lambda i, e: (e, 0, 0)),
                  pl.BlockSpec((1, D_FF, D_MODEL), lambda i, e: (e, 0, 0))],
        out_specs=pl.BlockSpec((tile, D_MODEL), lambda i, e: (i, 0)),
        out_shape=jax.ShapeDtypeStruct((r, D_MODEL), F32),
        scratch_shapes=[pltpu.VMEM((tile, D_MODEL), F32)],
        compiler_params=_cparams(("arbitrary", "arbitrary")),
        name="moe",
    )(h2, gate, x1, gt, wg, wu, wd)


def _front_odd_kernel(x_ref, sh_ref, sc_ref, g_ref, win_ref, gq_ref, gk_ref, rc_ref, rs1_ref, rs2_ref,
                      q_ref, k_ref, v_ref, ku_ref, vu_ref):
    x = x_ref[...]
    h = _rms(x, g_ref[...]) * (1.0 + sc_ref[...]) + sh_ref[...]
    z = _dot(h.astype(BF16), win_ref[...])
    rc, rs1, rs2 = rc_ref[...], rs1_ref[...], rs2_ref[...]
    scale = SWA_HD ** -0.5
    for hd in range(SWA_HEADS):
        sl = slice(hd * LANES, (hd + 1) * LANES)
        qn = _rope(_rms(z[:, sl], gq_ref[...], SWA_HD), SWA_ROT // 2, rc, rs1, rs2)
        q_ref[:, sl] = (qn * scale).astype(BF16)
    ks, vs = [], []
    for kh in range(SWA_KV):
        sl = slice(kh * LANES, (kh + 1) * LANES)
        zk = z[:, (SWA_HEADS + kh) * LANES:(SWA_HEADS + kh + 1) * LANES]
        kn = _rope(_rms(zk, gk_ref[...], SWA_HD), SWA_ROT // 2, rc, rs1, rs2)
        vv = z[:, (SWA_HEADS + SWA_KV + kh) * LANES:(SWA_HEADS + SWA_KV + kh + 1) * LANES]
        k_ref[:, sl] = kn.astype(BF16)
        v_ref[:, sl] = vv.astype(BF16)
        ks.append(kn)
        vs.append(vv)
    for j in range(SWA_KV // 2):
        sl = slice(j * LANES, (j + 1) * LANES)
        ku_ref[:, sl] = ks[2 * j] + pltpu.roll(ks[2 * j + 1], SWA_HD, axis=1)
        vu_ref[:, sl] = vs[2 * j] + pltpu.roll(vs[2 * j + 1], SWA_HD, axis=1)


def _front_odd(x, sh, sc, g, win, gq, gk, tabs, tile):
    r = x.shape[0]
    per_row_mod = sh.shape[0] != 1
    per_row_tab = tabs[0].shape[0] != 1
    in_specs = [pl.BlockSpec((tile, D_MODEL), lambda i: (i, 0)),
                _rows(tile, D_MODEL, per_row_mod), _rows(tile, D_MODEL, per_row_mod),
                _full(g.shape), _full(win.shape), _full(gq.shape), _full(gk.shape),
                _rows(tile, LANES, per_row_tab), _rows(tile, LANES, per_row_tab), _rows(tile, LANES, per_row_tab)]
    widths = [(SWA_HEADS * LANES, BF16), (SWA_KV * LANES, BF16), (SWA_KV * LANES, BF16),
              (SWA_KV * SWA_HD, F32), (SWA_KV * SWA_HD, F32)]
    return pl.pallas_call(
        _front_odd_kernel,
        grid=(r // tile,),
        in_specs=in_specs,
        out_specs=[pl.BlockSpec((tile, w), lambda i: (i, 0)) for w, _ in widths],
        out_shape=[jax.ShapeDtypeStruct((r, w), dt) for w, dt in widths],
        compiler_params=_cparams(("arbitrary",)),
        name="front_odd",
    )(x, sh, sc, g, win, gq, gk, *tabs)


def _swa_prompt_kernel(sink_ref, q_ref, kp_ref, kc_ref, vp_ref, vc_ref, o_ref):
    n = pl.program_id(0)
    w = WINDOW
    qi = lax.broadcasted_iota(jnp.int32, (w, 2 * w), 0) + w
    kj = lax.broadcasted_iota(jnp.int32, (w, 2 * w), 1)
    valid = (kj <= qi) & (qi - kj < w) & (n * w - w + kj >= 0)
    for kh in range(SWA_KV):
        sl = slice(kh * LANES, (kh + 1) * LANES)
        kk = jnp.concatenate([kp_ref[:, sl], kc_ref[:, sl]], axis=0)
        vv = jnp.concatenate([vp_ref[:, sl], vc_ref[:, sl]], axis=0)
        for gi in range(SWA_GROUP):
            hd = kh * SWA_GROUP + gi
            hsl = slice(hd * LANES, (hd + 1) * LANES)
            s = jnp.where(valid, _dot_nt(q_ref[:, hsl], kk), -jnp.inf)
            sink = sink_ref[hd]
            m = jnp.maximum(jnp.max(s, axis=-1, keepdims=True), sink)
            pr = jnp.exp(s - m)
            den = jnp.sum(pr, axis=-1, keepdims=True) + jnp.exp(sink - m)
            o_ref[:, hsl] = (_dot(pr.astype(BF16), vv) / den).astype(BF16)


def _swa_prompt_attn(q, k, v, sinks):
    t = q.shape[0]
    w = WINDOW
    prev = lambda n, s: (jnp.maximum(n - 1, 0), 0)
    cur = lambda n, s: (n, 0)
    grid_spec = pltpu.PrefetchScalarGridSpec(
        num_scalar_prefetch=1,
        grid=(t // w,),
        in_specs=[pl.BlockSpec((w, SWA_HEADS * LANES), cur),
                  pl.BlockSpec((w, SWA_KV * LANES), prev), pl.BlockSpec((w, SWA_KV * LANES), cur),
                  pl.BlockSpec((w, SWA_KV * LANES), prev), pl.BlockSpec((w, SWA_KV * LANES), cur)],
        out_specs=pl.BlockSpec((w, SWA_HEADS * LANES), cur),
    )
    return pl.pallas_call(
        _swa_prompt_kernel,
        grid_spec=grid_spec,
        out_shape=jax.ShapeDtypeStruct((t, SWA_HEADS * LANES), BF16),
        compiler_params=_cparams(("arbitrary",)),
        name="swa_prompt_attn",
    )(sinks, q, k, k, v, v)


def _swa_sample_kernel(q_ref, kc_ref, vc_ref, kn_ref, vn_ref, sink_ref, o_ref, wk_ref, wv_ref, *, sb):
    w = WINDOW
    width = SWA_KV * SWA_HD
    rowgrp = lax.broadcasted_iota(jnp.int32, (SWA_HEADS, 1), 0) // SWA_GROUP
    col = lax.broadcasted_iota(jnp.int32, (1, w), 1)
    low = lax.broadcasted_iota(jnp.int32, (1, LANES), 1) < SWA_HD
    sink = sink_ref[...]
    for b in range(sb):
        qf = q_ref[b].astype(F32)
        a = jnp.concatenate([qf, jnp.zeros_like(qf)], axis=1)
        qx = a
        for kh in range(1, SWA_KV):
            qx = jnp.where(rowgrp == kh, pltpu.roll(a, kh * SWA_HD, axis=1), qx)
        kc = kc_ref[b]
        vc = vc_ref[b]
        kn = kn_ref[b]
        vn = vn_ref[b]
        s = _dot_nt(qx.astype(BF16), kc.astype(BF16))
        s = jnp.where(col >= 1, s, -jnp.inf)
        s_new = jnp.sum(qx.astype(BF16).astype(F32) * kn.astype(BF16).astype(F32), axis=-1, keepdims=True)
        m = jnp.maximum(jnp.maximum(jnp.max(s, axis=-1, keepdims=True), s_new), sink)
        pr = jnp.exp(s - m)
        pn = jnp.exp(s_new - m)
        den = jnp.sum(pr, axis=-1, keepdims=True) + pn + jnp.exp(sink - m)
        o = (_dot(pr.astype(BF16), vc.astype(BF16)) + pn * vn.astype(BF16).astype(F32)) / den
        ox = o
        for kh in range(1, SWA_KV):
            ox = jnp.where(rowgrp == kh, pltpu.roll(o, width - kh * SWA_HD, axis=1), ox)
        o_ref[b] = jnp.where(low, ox[:, :LANES], 0.0).astype(BF16)
        wk_ref[b, pl.ds(0, w - 1), :] = kc_ref[b, pl.ds(1, w - 1), :]
        wk_ref[b, pl.ds(w - 1, 1), :] = kn
        wv_ref[b, pl.ds(0, w - 1), :] = vc_ref[b, pl.ds(1, w - 1), :]
        wv_ref[b, pl.ds(w - 1, 1), :] = vn


def _swa_sample_attn(q, kc, vc, kn, vn, sinks, sb):
    b = q.shape[0]
    w = WINDOW
    width = SWA_KV * SWA_HD
    blk3 = lambda s1, s2: pl.BlockSpec((sb, s1, s2), lambda i: (i, 0, 0))
    return pl.pallas_call(
        functools.partial(_swa_sample_kernel, sb=sb),
        grid=(b // sb,),
        in_specs=[blk3(SWA_HEADS, LANES), blk3(w, width), blk3(w, width), blk3(1, width), blk3(1, width),
                  _full((SWA_HEADS, 1))],
        out_specs=[blk3(SWA_HEADS, LANES), blk3(w, width), blk3(w, width)],
        out_shape=[jax.ShapeDtypeStruct((b, SWA_HEADS, LANES), BF16),
                   jax.ShapeDtypeStruct((b, w, width), F32), jax.ShapeDtypeStruct((b, w, width), F32)],
        compiler_params=_cparams(("arbitrary",)),
        name="swa_sample_attn",
    )(q.reshape(b, SWA_HEADS, LANES), kc, vc, kn.reshape(b, 1, width), vn.reshape(b, 1, width),
      sinks.reshape(SWA_HEADS, 1))


def _rope_tables(pos, rot, offset):
    half = rot // 2
    inv = ROPE_THETA ** (-jnp.arange(half, dtype=F32) / half)
    ang = pos.astype(F32)[:, None] * inv[None, :]
    cos, sin = jnp.cos(ang), jnp.sin(ang)
    n = pos.shape[0]
    rc = jnp.ones((n, LANES), F32).at[:, offset:offset + half].set(cos).at[:, offset + half:offset + rot].set(cos)
    rs1 = jnp.zeros((n, LANES), F32).at[:, offset:offset + half].set(-sin)
    rs2 = jnp.zeros((n, LANES), F32).at[:, offset + half:offset + rot].set(sin)
    return rc, rs1, rs2


def _pad_lanes(v, offset=0):
    return jnp.zeros((1, LANES), F32).at[0, offset:offset + v.shape[0]].set(v)


def _even_weights(w_in, g_qa, w_uq, g_qh, g_kva, g_kr, w_uk, w_uv):
    o3 = S5_WIDTH + Q_LORA + KV_LORA
    win = jnp.zeros((D_MODEL, 1280), F32).at[:, :o3].set(w_in[:, :o3])
    win = win.at[:, o3 + MLA_NOPE:o3 + MLA_QK].set(w_in[:, o3:])
    wuq = jnp.zeros((MLA_HEADS, Q_LORA, LANES), F32).at[:, :, :MLA_QK].set(w_uq.transpose(1, 0, 2))
    wuk = jnp.zeros((KV_LORA, MLA_HEADS, LANES), F32).at[:, :, :MLA_NOPE].set(w_uk)
    front = (win.astype(BF16), g_qa.reshape(1, Q_LORA), wuq.astype(BF16), _pad_lanes(g_qh),
             g_kva.reshape(1, KV_LORA), _pad_lanes(g_kr, MLA_NOPE),
             wuk.reshape(KV_LORA, MLA_HEADS * LANES).astype(BF16),
             w_uv.reshape(KV_LORA, MLA_HEADS * MLA_V).astype(BF16))
    wukt = jnp.zeros((MLA_HEADS, LANES, KV_LORA), F32).at[:, :MLA_NOPE, :].set(w_uk.transpose(1, 2, 0))
    eye = jnp.eye(MLA_HEADS, dtype=F32)
    wuv_blk = jnp.einsum('chd,hg->hcgd', w_uv, eye).reshape(MLA_HEADS * KV_LORA, MLA_HEADS * MLA_V)
    return front, wukt.astype(BF16), wuv_blk.astype(BF16)


def _odd_weights(w_in, w_out):
    nq, nk = SWA_HEADS * SWA_HD, SWA_KV * SWA_HD
    nh = SWA_HEADS + 2 * SWA_KV
    win = jnp.zeros((D_MODEL, nh, LANES), F32).at[:, :, :SWA_HD].set(w_in.reshape(D_MODEL, nh, SWA_HD))
    wout = jnp.zeros((SWA_HEADS, LANES, D_MODEL), F32).at[:, :SWA_HD, :].set(w_out.reshape(SWA_HEADS, SWA_HD, D_MODEL))
    del nq, nk
    return win.reshape(D_MODEL, nh * LANES).astype(BF16), wout.reshape(SWA_HEADS * LANES, D_MODEL).astype(BF16)


def kernel(x_prompt, x_sample, c_prompt, c_sample, cache_ckv, cache_krope, page_table, state_s5_re, state_s5_im, cache_win_k, cache_win_v, w_mod, b_mod, g_norm_mix, g_norm_ffn, w_in_even, w_out_even, s5_lam_re, s5_lam_im, s5_log_step, s5_b_re, s5_b_im, s5_c_re, s5_c_im, s5_d, s5_w_glu, mla_g_qa, mla_w_uq, mla_g_qh, mla_g_kva, mla_g_kr, mla_w_uk, mla_w_uv, w_in_odd, w_out_odd, swa_g_q, swa_g_k, swa_sinks, w_router, b_router, moe_w_gate, moe_w_up, moe_w_down):
    t = x_prompt.shape[1]
    nb = x_sample.shape[0]
    past_len = page_table.shape[1] * PAGE
    row_tile = 512
    s5_tile, s5_seg = 256, 32
    attn_tile = 1024
    moe_tile = 1024
    pages_per_chunk = 16
    swa_seq_block = 8

    n_c = 1 + nb
    rp = -(-n_c // SUBLANES) * SUBLANES
    c_all = jnp.concatenate([c_prompt, c_sample, jnp.zeros((rp - n_c, D_MODEL), F32)], axis=0)
    mod = _modulation(c_all, w_mod, b_mod)

    def mods(layer, sample):
        rows = mod[layer, 1:1 + nb] if sample else mod[layer, 0:1]
        return [rows[:, k * D_MODEL:(k + 1) * D_MODEL] for k in range(6)]

    wr = jnp.zeros((D_MODEL, LANES), F32).at[:, :N_EXPERTS].set(w_router)
    br = _pad_lanes(b_router)
    wg, wu, wd = moe_w_gate.astype(BF16), moe_w_up.astype(BF16), moe_w_down.astype(BF16)

    xp = x_prompt.reshape(t, D_MODEL)
    xs = x_sample.reshape(nb, D_MODEL)
    pos_p = jnp.arange(t)
    pos_s = jnp.full((1,), past_len)

    front_w, wukt, wuv_blk = _even_weights(w_in_even[0], mla_g_qa[0], mla_w_uq[0], mla_g_qh[0], mla_g_kva[0],
                                           mla_g_kr[0], mla_w_uk[0], mla_w_uv[0])
    a_re, a_im, bb_re, bb_im, al_re, al_im = _s5_discretize(s5_lam_re[0], s5_lam_im[0], s5_log_step[0],
                                                            s5_b_re[0], s5_b_im[0], s5_seg)
    bre, bim, cre, cim = _s5_layouts(bb_re, bb_im, s5_c_re[0], s5_c_im[0])
    d_skip = s5_d[0].reshape(1, S5_WIDTH)
    wglu = s5_w_glu[0].astype(BF16)
    wo_ssm = w_out_even[0, :S5_WIDTH].astype(BF16)
    wo_att = w_out_even[0, S5_WIDTH:].astype(BF16)
    g_mix0 = g_norm_mix[0].reshape(1, D_MODEL)
    g_ffn0 = g_norm_ffn[0].reshape(1, D_MODEL)

    sh1, sc1, gt1, sh2, sc2, gt2 = mods(0, False)
    u, q, k, v, ckv_p, kr_p = _front_even(xp, sh1, sc1, g_mix0, front_w, _rope_tables(pos_p, MLA_ROPE, MLA_NOPE),
                                          row_tile)
    y_ssm, st = _s5_scan(u, bre, bim, a_re, a_im, al_re, al_im, cre, cim, d_skip, wglu, s5_tile, s5_seg)
    sr_p = st[:S5_CHUNKS].reshape(1, 1, S5_GROUPS, S5_STATE)
    si_p = st[S5_CHUNKS:].reshape(1, 1, S5_GROUPS, S5_STATE)
    y_att = _mla_prompt_attn(q, k, v, attn_tile)
    x1, h2, gate = _post(xp, [y_ssm, y_att], [wo_ssm, wo_att], gt1, sh2, sc2, g_ffn0, wr, br, row_tile)
    xp = _moe(h2, gate, x1, gt2, wg[0], wu[0], wd[0], moe_tile)

    sh1, sc1, gt1, sh2, sc2, gt2 = mods(0, True)
    u, q, _, _, ckv_s, kr_s = _front_even(xs, sh1, sc1, g_mix0, front_w, _rope_tables(pos_s, MLA_ROPE, MLA_NOPE), nb)
    y_ssm, sr_s, si_s = _s5_step(u, state_s5_re[0].reshape(nb, S5_LANES), state_s5_im[0].reshape(nb, S5_LANES),
                                 bre, bim, a_re, a_im, cre, cim, d_skip, wglu)
    y_att = _mla_sample_attn(q, ckv_s, kr_s, cache_ckv, cache_krope, page_table, wukt, wuv_blk, pages_per_chunk)
    x1, h2, gate = _post(xs, [y_ssm, y_att], [wo_ssm, wo_att], gt1, sh2, sc2, g_ffn0, wr, br, nb)
    xs = _moe(h2, gate, x1, gt2, wg[0], wu[0], wd[0], nb)

    win_odd, wout_odd = _odd_weights(w_in_odd[0], w_out_odd[0])
    gq, gk = _pad_lanes(swa_g_q[0]), _pad_lanes(swa_g_k[0])
    g_mix1 = g_norm_mix[1].reshape(1, D_MODEL)
    g_ffn1 = g_norm_ffn[1].reshape(1, D_MODEL)
    sinks = swa_sinks[0]

    sh1, sc1, gt1, sh2, sc2, gt2 = mods(1, False)
    q, k, v, ku, vu = _front_odd(xp, sh1, sc1, g_mix1, win_odd, gq, gk, _rope_tables(pos_p, SWA_ROT, 0), row_tile)
    o = _swa_prompt_attn(q, k, v, sinks)
    wk_p = ku[t - WINDOW:].reshape(1, 1, WINDOW, SWA_KV, SWA_HD)
    wv_p = vu[t - WINDOW:].reshape(1, 1, WINDOW, SWA_KV, SWA_HD)
    x1, h2, gate = _post(xp, [o], [wout_odd], gt1, sh2, sc2, g_ffn1, wr, br, row_tile)
    xp = _moe(h2, gate, x1, gt2, wg[1], wu[1], wd[1], moe_tile)

    sh1, sc1, gt1, sh2, sc2, gt2 = mods(1, True)
    q, _, _, ku, vu = _front_odd(xs, sh1, sc1, g_mix1, win_odd, gq, gk, _rope_tables(pos_s, SWA_ROT, 0), nb)
    width = SWA_KV * SWA_HD
    o, wk_s, wv_s = _swa_sample_attn(q, cache_win_k[0].reshape(nb, WINDOW, width),
                                     cache_win_v[0].reshape(nb, WINDOW, width), ku, vu, sinks, swa_seq_block)
    x1, h2, gate = _post(xs, [o.reshape(nb, SWA_HEADS * LANES)], [wout_odd], gt1, sh2, sc2, g_ffn1, wr, br, nb)
    xs = _moe(h2, gate, x1, gt2, wg[1], wu[1], wd[1], nb)

    return (xp.reshape(1, t, D_MODEL), xs.reshape(nb, 1, D_MODEL),
            ckv_p.reshape(1, 1, t, KV_LORA), kr_p.reshape(1, 1, t, MLA_ROPE), sr_p, si_p, wk_p, wv_p,
            ckv_s.reshape(1, nb, 1, KV_LORA), kr_s.reshape(1, nb, 1, MLA_ROPE),
            sr_s.reshape(1, nb, S5_GROUPS, S5_STATE), si_s.reshape(1, nb, S5_GROUPS, S5_STATE),
            wk_s.reshape(1, nb, WINDOW, SWA_KV, SWA_HD), wv_s.reshape(1, nb, WINDOW, SWA_KV, SWA_HD))
```

```python
import functools
import math

import jax
import jax.numpy as jnp
from jax import lax
from jax.experimental import pallas as pl
from jax.experimental.pallas import tpu as pltpu

F32 = jnp.float32
BF16 = jnp.bfloat16
EPS = 1e-6
ROPE_THETA = 500000.0
LANES = 128
SUBLANES = 8
VMEM_LIMIT = 56 * 1024 * 1024

D_MODEL = 1024
PAGE = 128
S5_WIDTH = 512
S5_GROUP = 16
S5_GROUPS = 32
S5_STATE = 64
S5_LANES = S5_GROUPS * S5_STATE
S5_CHUNKS = S5_LANES // LANES
MLA_HEADS = 8
MLA_NOPE = 64
MLA_ROPE = 32
MLA_QK = 96
MLA_V = 64
Q_LORA = 384
KV_LORA = 256
SWA_HEADS = 16
SWA_KV = 4
SWA_GROUP = 4
SWA_HD = 64
SWA_ROT = 16
WINDOW = 128
N_EXPERTS = 16
GROUP_SIZE = 4
D_FF = 256

_NT = (((1,), (1,)), ((), ()))


def _dot(a, b):
    return jnp.dot(a, b, preferred_element_type=F32)


def _dot_nt(a, b):
    return lax.dot_general(a, b, _NT, preferred_element_type=F32)


def _rms(x, g, n=None):
    n = x.shape[-1] if n is None else n
    ss = jnp.sum(x * x, axis=-1, keepdims=True) * (1.0 / n)
    return x * lax.rsqrt(ss + EPS) * g


def _rope(x, shift, rc, rs1, rs2):
    return (x * rc + pltpu.roll(x, LANES - shift, axis=1) * rs1
            + pltpu.roll(x, shift, axis=1) * rs2)


def _cparams(sem=None, vmem=VMEM_LIMIT):
    return pltpu.CompilerParams(dimension_semantics=sem, vmem_limit_bytes=vmem)


def _full(shape):
    nd = len(shape)
    return pl.BlockSpec(shape, lambda *_: (0,) * nd)


def _rows(tile, width, per_row):
    if per_row:
        return pl.BlockSpec((tile, width), lambda i: (i, 0))
    return pl.BlockSpec((1, width), lambda i: (0, 0))


def _mod_kernel(c_ref, w_ref, b_ref, o_ref):
    c = c_ref[...]
    s = (c * jax.nn.sigmoid(c)).astype(BF16)
    o_ref[0] = _dot(s, w_ref[0].astype(BF16)) + b_ref[0]


def _modulation(c_all, w_mod, b_mod):
    depth, _, n = w_mod.shape
    rp = c_all.shape[0]
    tn = 1536
    return pl.pallas_call(
        _mod_kernel,
        grid=(depth, n // tn),
        in_specs=[pl.BlockSpec((rp, D_MODEL), lambda l, j: (0, 0)),
                  pl.BlockSpec((1, D_MODEL, tn), lambda l, j: (l, 0, j)),
                  pl.BlockSpec((1, 1, tn), lambda l, j: (l, 0, j))],
        out_specs=pl.BlockSpec((1, rp, tn), lambda l, j: (l, 0, j)),
        out_shape=jax.ShapeDtypeStruct((depth, rp, n), F32),
        compiler_params=_cparams(("arbitrary", "arbitrary")),
        name="modulation",
    )(c_all, w_mod, b_mod.reshape(depth, 1, n))


def _front_even_kernel(x_ref, sh_ref, sc_ref, g_ref, win_ref, gqa_ref, wuq_ref, gqh_ref, gkva_ref,
                       gkr_ref, wuk_ref, wuv_ref, rc_ref, rs1_ref, rs2_ref,
                       u_ref, q_ref, k_ref, v_ref, ckv_ref, kr_ref, *, q_scale):
    x = x_ref[...]
    h = _rms(x, g_ref[...]) * (1.0 + sc_ref[...]) + sh_ref[...]
    z = _dot(h.astype(BF16), win_ref[...])
    u_ref[...] = z[:, :512]
    c_q = _rms(z[:, 512:896], gqa_ref[...]).astype(BF16)
    c_kv = _rms(z[:, 896:1152], gkva_ref[...])
    ckv_ref[...] = c_kv
    rc, rs1, rs2 = rc_ref[...], rs1_ref[...], rs2_ref[...]
    kr = _rope(_rms(z[:, 1152:1280], gkr_ref[...], MLA_ROPE), MLA_ROPE // 2, rc, rs1, rs2)
    kr_ref[...] = kr[:, MLA_NOPE:MLA_QK]
    ckvb = c_kv.astype(BF16)
    kn = _dot(ckvb, wuk_ref[...])
    v_ref[...] = _dot(ckvb, wuv_ref[...]).astype(BF16)
    for hd in range(MLA_HEADS):
        sl = slice(hd * LANES, (hd + 1) * LANES)
        qh = _dot(c_q, wuq_ref[hd])
        qn = _rope(_rms(qh, gqh_ref[...], MLA_QK), MLA_ROPE // 2, rc, rs1, rs2)
        q_ref[:, sl] = (qn * q_scale).astype(BF16)
        k_ref[:, sl] = (kn[:, sl] + kr).astype(BF16)


def _front_even(x, sh, sc, g, wts, tabs, tile, q_scale):
    r = x.shape[0]
    per_row_mod = sh.shape[0] != 1
    per_row_tab = tabs[0].shape[0] != 1
    win, gqa, wuq, gqh, gkva, gkr, wuk, wuv = wts
    in_specs = [pl.BlockSpec((tile, D_MODEL), lambda i: (i, 0)),
                _rows(tile, D_MODEL, per_row_mod), _rows(tile, D_MODEL, per_row_mod),
                _full(g.shape), _full(win.shape), _full(gqa.shape), _full(wuq.shape), _full(gqh.shape),
                _full(gkva.shape), _full(gkr.shape), _full(wuk.shape), _full(wuv.shape),
                _rows(tile, LANES, per_row_tab), _rows(tile, LANES, per_row_tab), _rows(tile, LANES, per_row_tab)]
    widths = [(S5_WIDTH, F32), (MLA_HEADS * LANES, BF16), (MLA_HEADS * LANES, BF16),
              (MLA_HEADS * MLA_V, BF16), (KV_LORA, F32), (MLA_ROPE, F32)]
    return pl.pallas_call(
        functools.partial(_front_even_kernel, q_scale=q_scale),
        grid=(r // tile,),
        in_specs=in_specs,
        out_specs=[pl.BlockSpec((tile, w), lambda i: (i, 0)) for w, _ in widths],
        out_shape=[jax.ShapeDtypeStruct((r, w), dt) for w, dt in widths],
        compiler_params=_cparams(("arbitrary",)),
        name="front_even",
    )(x, sh, sc, g, win, gqa, wuq, gqh, gkva, gkr, wuk, wuv, *tabs)


def _s5_disc_kernel(lr_ref, li_ref, ls_ref, br_ref, bi_ref, are_ref, aim_ref, bbr_ref, bbi_ref,
                    alr_ref, ali_ref, *, log2_len):
    lr, li = lr_ref[...], li_ref[...]
    dt = jnp.exp(ls_ref[...])
    mag = jnp.exp(lr * dt)
    ang = li * dt
    a_re = mag * jnp.cos(ang)
    a_im = mag * jnp.sin(ang)
    den = lr * lr + li * li
    k_re = ((a_re - 1.0) * lr + a_im * li) / den
    k_im = (a_im * lr - (a_re - 1.0) * li) / den
    are_ref[...] = a_re
    aim_ref[...] = a_im
    for c in range(S5_GROUP):
        br, bi = br_ref[c], bi_ref[c]
        bbr_ref[c] = k_re * br - k_im * bi
        bbi_ref[c] = k_re * bi + k_im * br
    pr, pi = a_re, a_im
    for _ in range(log2_len):
        pr, pi = pr * pr - pi * pi, 2.0 * pr * pi
    alr_ref[...] = pr
    ali_ref[...] = pi


def _s5_discretize(lam_re, lam_im, log_step, b_re, b_im, seg_len):
    g, n = lam_re.shape
    outs = [jax.ShapeDtypeStruct((g, n), F32)] * 2 + [jax.ShapeDtypeStruct((S5_GROUP, g, n), F32)] * 2 \
        + [jax.ShapeDtypeStruct((g, n), F32)] * 2
    return pl.pallas_call(
        functools.partial(_s5_disc_kernel, log2_len=int(math.log2(seg_len))),
        out_shape=outs,
        name="s5_discretize",
    )(lam_re, lam_im, log_step.reshape(g, 1), b_re.transpose(2, 0, 1), b_im.transpose(2, 0, 1))


def _s5_layouts(bb_re, bb_im, c_re, c_im):
    eye8 = jnp.eye(8, dtype=F32)

    def bmat(bb):
        b4 = bb.reshape(S5_GROUP, 4, 8, S5_STATE)
        m = jnp.einsum('cjgn,gh->jgchn', b4, eye8)
        return m.reshape(4, LANES, 512).astype(BF16)

    sel = jax.nn.one_hot((2 * jnp.arange(S5_CHUNKS)[:, None] + jnp.arange(2)[None, :]) % 8, 8, dtype=F32)

    def cmat(c):
        c4 = c.reshape(S5_CHUNKS, 2, S5_GROUP, S5_STATE)
        m = jnp.einsum('asck,asg->askgc', c4, sel)
        return m.reshape(S5_CHUNKS, LANES, LANES).astype(BF16)

    return bmat(bb_re), bmat(bb_im), cmat(c_re), cmat(-c_im)


def _gelu_glu(y, wglu_ref):
    z = jax.nn.gelu(y)
    return z * jax.nn.sigmoid(_dot(z.astype(BF16), wglu_ref[...]))


def _s5_scan_kernel(u0_ref, u1_ref, u2_ref, u3_ref, perm_ref, bre_ref, bim_ref, are_ref, aim_ref, alr_ref,
                    ali_ref, cre_ref, cim_ref, d_ref, wglu_ref, y_ref, st_ref, bu_ref, hs_ref, carry_ref,
                    *, seg_len):
    i = pl.program_id(0)

    @pl.when(i == 0)
    def _():
        carry_ref[...] = jnp.zeros_like(carry_ref)

    us = []
    for j, u_ref in enumerate((u0_ref, u1_ref, u2_ref, u3_ref)):
        uj = jnp.concatenate([u_ref[pl.ds(t, SUBLANES, stride=seg_len), :] for t in range(seg_len)], axis=0)
        us.append(uj)
        ub = uj.astype(BF16)
        re = _dot(ub, bre_ref[j])
        im = _dot(ub, bim_ref[j])
        for q in range(4):
            bu_ref[4 * j + q] = re[:, q * LANES:(q + 1) * LANES]
            bu_ref[S5_CHUNKS + 4 * j + q] = im[:, q * LANES:(q + 1) * LANES]

    per = 4
    for grp in range(S5_CHUNKS // per):
        cs = [grp * per + q for q in range(per)]
        ar = [jnp.broadcast_to(are_ref[c], (SUBLANES, LANES)) for c in cs]
        ai = [jnp.broadcast_to(aim_ref[c], (SUBLANES, LANES)) for c in cs]

        def advance(t, hs, store):
            out = []
            for k, c in enumerate(cs):
                hr, hi = hs[2 * k], hs[2 * k + 1]
                rows = pl.ds(pl.multiple_of(t * SUBLANES, SUBLANES), SUBLANES)
                nr = ar[k] * hr - ai[k] * hi + bu_ref[c, rows, :]
                ni = ar[k] * hi + ai[k] * hr + bu_ref[S5_CHUNKS + c, rows, :]
                if store:
                    hs_ref[c, rows, :] = nr
                    hs_ref[S5_CHUNKS + c, rows, :] = ni
                out += [nr, ni]
            return tuple(out)

        zero = tuple(jnp.zeros((SUBLANES, LANES), F32) for _ in range(2 * per))
        ends = lax.fori_loop(0, seg_len, lambda t, hs: advance(t, hs, False), zero, unroll=2)
        init = []
        for k, c in enumerate(cs):
            er, ei = ends[2 * k], ends[2 * k + 1]
            lr, li = alr_ref[c], ali_ref[c]
            hr, hi = carry_ref[c], carry_ref[S5_CHUNKS + c]
            rows_r, rows_i = [], []
            for s in range(SUBLANES):
                rows_r.append(hr)
                rows_i.append(hi)
                hr, hi = (er[s:s + 1] + lr * hr - li * hi, ei[s:s + 1] + lr * hi + li * hr)
            carry_ref[c] = hr
            carry_ref[S5_CHUNKS + c] = hi
            init += [jnp.concatenate(rows_r, axis=0), jnp.concatenate(rows_i, axis=0)]
        lax.fori_loop(0, seg_len, lambda t, hs: advance(t, hs, True), tuple(init), unroll=2)

    ys = []
    for j in range(4):
        acc = None
        for q in range(4):
            c = 4 * j + q
            t = _dot(hs_ref[c].astype(BF16), cre_ref[c]) + _dot(hs_ref[S5_CHUNKS + c].astype(BF16), cim_ref[c])
            acc = t if acc is None else acc + t
        ys.append(acc)
    y = jnp.concatenate(ys, axis=1) + d_ref[...] * jnp.concatenate(us, axis=1)
    gated = _gelu_glu(y, wglu_ref).astype(BF16)
    y_ref[...] = _dot(perm_ref[...], gated).astype(BF16)

    @pl.when(i == pl.num_programs(0) - 1)
    def _():
        st_ref[...] = carry_ref[...]


def _s5_scan(u, bre, bim, a_re, a_im, al_re, al_im, cre, cim, d, wglu, seg_len):
    t = u.shape[0]
    tile = SUBLANES * seg_len
    ch = lambda a: a.reshape(S5_CHUNKS, 1, LANES)
    r = jnp.arange(tile)
    perm = jax.nn.one_hot(SUBLANES * (r % seg_len) + r // seg_len, tile, dtype=BF16)
    args = (u, u, u, u, perm, bre, bim, ch(a_re), ch(a_im), ch(al_re), ch(al_im), cre, cim, d, wglu)
    in_specs = ([pl.BlockSpec((tile, LANES), functools.partial(lambda i, j: (i, j), j=j)) for j in range(4)]
                + [_full(a.shape) for a in args[4:]])
    return pl.pallas_call(
        functools.partial(_s5_scan_kernel, seg_len=seg_len),
        grid=(t // tile,),
        in_specs=in_specs,
        out_specs=[pl.BlockSpec((tile, S5_WIDTH), lambda i: (i, 0)), _full((2 * S5_CHUNKS, 1, LANES))],
        out_shape=[jax.ShapeDtypeStruct((t, S5_WIDTH), BF16),
                   jax.ShapeDtypeStruct((2 * S5_CHUNKS, 1, LANES), F32)],
        scratch_shapes=[pltpu.VMEM((2 * S5_CHUNKS, tile, LANES), F32),
                        pltpu.VMEM((2 * S5_CHUNKS, tile, LANES), F32),
                        pltpu.VMEM((2 * S5_CHUNKS, 1, LANES), F32)],
        compiler_params=_cparams(("arbitrary",)),
        name="s5_scan",
    )(*args)


def _s5_step_kernel(u_ref, h0r_ref, h0i_ref, bre_ref, bim_ref, are_ref, aim_ref, cre_ref, cim_ref,
                    d_ref, wglu_ref, y_ref, sr_ref, si_ref):
    u = u_ref[...]
    ub = u.astype(BF16)
    ys = []
    for j in range(4):
        uj = ub[:, j * LANES:(j + 1) * LANES]
        re = _dot(uj, bre_ref[j])
        im = _dot(uj, bim_ref[j])
        acc = None
        for q in range(4):
            sl = slice((4 * j + q) * LANES, (4 * j + q + 1) * LANES)
            ar, ai = are_ref[:, sl], aim_ref[:, sl]
            h0r, h0i = h0r_ref[:, sl], h0i_ref[:, sl]
            hr = ar * h0r - ai * h0i + re[:, q * LANES:(q + 1) * LANES]
            hi = ar * h0i + ai * h0r + im[:, q * LANES:(q + 1) * LANES]
            sr_ref[:, sl] = hr
            si_ref[:, sl] = hi
            t = _dot(hr.astype(BF16), cre_ref[4 * j + q]) + _dot(hi.astype(BF16), cim_ref[4 * j + q])
            acc = t if acc is None else acc + t
        ys.append(acc)
    y = jnp.concatenate(ys, axis=1) + d_ref[...] * u
    y_ref[...] = _gelu_glu(y, wglu_ref).astype(BF16)


def _s5_step(u, h0r, h0i, bre, bim, a_re, a_im, cre, cim, d, wglu):
    b = u.shape[0]
    return pl.pallas_call(
        _s5_step_kernel,
        out_shape=[jax.ShapeDtypeStruct((b, S5_WIDTH), BF16), jax.ShapeDtypeStruct((b, S5_LANES), F32),
                   jax.ShapeDtypeStruct((b, S5_LANES), F32)],
        compiler_params=_cparams(),
        name="s5_step",
    )(u, h0r, h0i, bre, bim, a_re.reshape(1, S5_LANES), a_im.reshape(1, S5_LANES), cre, cim, d, wglu)


def _mla_prompt_kernel(iq_ref, jk_ref, q_ref, k_ref, v_ref, o_ref, m_ref, acc_ref, *, sub):
    p = pl.program_id(1)
    i, j = iq_ref[p], jk_ref[p]
    tq, tk = q_ref.shape[0], k_ref.shape[0]
    low = lax.broadcasted_iota(jnp.int32, (1, LANES), 1) < MLA_V

    @pl.when(j == 0)
    def _():
        m_ref[...] = jnp.full_like(m_ref, -jnp.inf)
        acc_ref[...] = jnp.zeros_like(acc_ref)

    def block(diagonal):
        v = v_ref[...]
        one = jnp.ones_like(v)
        vs = (jnp.where(low, v, one), jnp.where(low, one, v))
        for hh in range(2):
            sl = slice(hh * LANES, (hh + 1) * LANES)
            for r in range(tq // sub):
                rows = slice(r * sub, (r + 1) * sub)
                nk = (r + 1) * sub if diagonal else tk
                s = _dot_nt(q_ref[rows, sl], k_ref[:nk, sl])
                if diagonal:
                    row = lax.broadcasted_iota(jnp.int32, (sub, nk), 0) + r * sub
                    col = lax.broadcasted_iota(jnp.int32, (sub, nk), 1)
                    s = jnp.where(col <= row, s, -jnp.inf)
                m_prev = m_ref[hh, rows]
                m_new = jnp.maximum(m_prev, jnp.max(s, axis=-1, keepdims=True))
                pr = jnp.exp2(s - m_new).astype(BF16)
                acc_ref[hh, rows] = jnp.exp2(m_prev - m_new) * acc_ref[hh, rows] + _dot(pr, vs[hh][:nk])
                m_ref[hh, rows] = m_new

    @pl.when(j < i)
    def _():
        block(False)

    @pl.when(j == i)
    def _():
        block(True)
        a0, a1 = acc_ref[0], acc_ref[1]
        o = jnp.where(low, a0 / pltpu.roll(a0, MLA_V, axis=1), a1 / pltpu.roll(a1, MLA_V, axis=1))
        o_ref[...] = o.astype(BF16)


def _mla_prompt_attn(q, k, v, tile, sub):
    t = q.shape[0]
    nq = t // tile
    pairs = [(i, j) for i in range(nq) for j in range(i + 1)]
    iq = jnp.asarray([p[0] for p in pairs], jnp.int32)
    jk = jnp.asarray([p[1] for p in pairs], jnp.int32)
    grid_spec = pltpu.PrefetchScalarGridSpec(
        num_scalar_prefetch=2,
        grid=(MLA_HEADS // 2, len(pairs)),
        in_specs=[pl.BlockSpec((tile, 2 * LANES), lambda h, p, iq, jk: (iq[p], h)),
                  pl.BlockSpec((tile, 2 * LANES), lambda h, p, iq, jk: (jk[p], h)),
                  pl.BlockSpec((tile, LANES), lambda h, p, iq, jk: (jk[p], h))],
        out_specs=pl.BlockSpec((tile, LANES), lambda h, p, iq, jk: (iq[p], h)),
        scratch_shapes=[pltpu.VMEM((2, tile, 1), F32), pltpu.VMEM((2, tile, LANES), F32)],
    )
    return pl.pallas_call(
        functools.partial(_mla_prompt_kernel, sub=sub),
        grid_spec=grid_spec,
        out_shape=jax.ShapeDtypeStruct((t, MLA_HEADS * MLA_V), BF16),
        compiler_params=_cparams(("arbitrary", "arbitrary")),
        name="mla_prompt_attn",
    )(iq, jk, q, k, v)


def _qlat_kernel(q_ref, wukt_ref, o_ref):
    for hd in range(MLA_HEADS):
        o_ref[hd] = _dot(q_ref[:, hd * LANES:(hd + 1) * LANES], wukt_ref[hd]).astype(BF16)


def _uv_kernel(o_ref, w_ref, y_ref):
    y_ref[...] = _dot(o_ref[...], w_ref[...]).astype(BF16)


def _mla_sample_kernel(pt_ref, ql_ref, qr_ref, cn_ref, kn_ref, ckv_hbm, krt_hbm, o_ref,
                       cbuf, kbuf, sem, *, npages, sub):
    b = pl.program_id(0)
    nb = pl.num_programs(0)
    slot = lax.rem(b, 2)

    def copies(bb, sl):
        out = []
        for p in range(npages):
            page = pt_ref[bb, p]
            tok = pl.ds(p * PAGE, PAGE)
            out.append(pltpu.make_async_copy(ckv_hbm.at[0, page], cbuf.at[sl, tok], sem.at[sl, 0]))
            out.append(pltpu.make_async_copy(krt_hbm.at[0, page], kbuf.at[sl, :, tok], sem.at[sl, 1]))
        return out

    @pl.when(b == 0)
    def _():
        for cp in copies(b, slot):
            cp.start()

    @pl.when(b + 1 < nb)
    def _():
        for cp in copies(b + 1, 1 - slot):
            cp.start()

    ql = ql_ref[0]
    qr = qr_ref[0]
    cn = cn_ref[0].astype(BF16).astype(F32)
    kn = kn_ref[0].astype(BF16).astype(F32)
    s_new = (jnp.sum(ql.astype(F32) * cn, axis=-1, keepdims=True)
             + jnp.sum(qr.astype(F32) * kn, axis=-1, keepdims=True))

    for cp in copies(b, slot):
        cp.wait()

    parts = []
    for sc in range(npages * PAGE // sub):
        tok = pl.ds(sc * sub, sub)
        cpg = cbuf[slot, tok, :].astype(BF16)
        krp = kbuf[slot, :, tok].astype(BF16)
        s = _dot_nt(ql, cpg) + _dot(qr, krp)
        m = jnp.max(s, axis=-1, keepdims=True)
        pr = jnp.exp(s - m)
        parts.append((m, jnp.sum(pr, axis=-1, keepdims=True), _dot(pr.astype(BF16), cpg)))
    m_all = s_new
    for m, _, _ in parts:
        m_all = jnp.maximum(m_all, m)
    w_new = jnp.exp(s_new - m_all)
    den = w_new
    acc = w_new * cn
    for m, l, o in parts:
        w = jnp.exp(m - m_all)
        den = den + w * l
        acc = acc + w * o
    o_ref[0] = (acc / den).astype(BF16)


def _mla_sample_attn(q, c_new, kr_new, cache_ckv, cache_krope, page_table, wukt, wuv_blk, sub):
    b = q.shape[0]
    npages = page_table.shape[1]
    qlat = pl.pallas_call(
        _qlat_kernel,
        out_shape=jax.ShapeDtypeStruct((MLA_HEADS, b, KV_LORA), BF16),
        name="mla_qlat",
    )(q, wukt)
    ql = qlat.transpose(1, 0, 2)
    qr = q.reshape(b, MLA_HEADS, LANES)[:, :, MLA_NOPE:MLA_QK]
    krt = jnp.swapaxes(cache_krope, 2, 3)
    n = npages * PAGE
    grid_spec = pltpu.PrefetchScalarGridSpec(
        num_scalar_prefetch=1,
        grid=(b,),
        in_specs=[pl.BlockSpec((1, MLA_HEADS, KV_LORA), lambda i, pt: (i, 0, 0)),
                  pl.BlockSpec((1, MLA_HEADS, MLA_ROPE), lambda i, pt: (i, 0, 0)),
                  pl.BlockSpec((1, 1, KV_LORA), lambda i, pt: (i, 0, 0)),
                  pl.BlockSpec((1, 1, MLA_ROPE), lambda i, pt: (i, 0, 0)),
                  pl.BlockSpec(memory_space=pl.ANY),
                  pl.BlockSpec(memory_space=pl.ANY)],
        out_specs=pl.BlockSpec((1, MLA_HEADS, KV_LORA), lambda i, pt: (i, 0, 0)),
        scratch_shapes=[pltpu.VMEM((2, n, KV_LORA), F32), pltpu.VMEM((2, MLA_ROPE, n), F32),
                        pltpu.SemaphoreType.DMA((2, 2))],
    )
    o_lat = pl.pallas_call(
        functools.partial(_mla_sample_kernel, npages=npages, sub=sub),
        grid_spec=grid_spec,
        out_shape=jax.ShapeDtypeStruct((b, MLA_HEADS, KV_LORA), BF16),
        compiler_params=_cparams(("arbitrary",)),
        name="mla_sample_attn",
    )(page_table, ql, qr, c_new.reshape(b, 1, KV_LORA), kr_new.reshape(b, 1, MLA_ROPE), cache_ckv, krt)
    return pl.pallas_call(
        _uv_kernel,
        out_shape=jax.ShapeDtypeStruct((b, MLA_HEADS * MLA_V), BF16),
        name="mla_sample_uv",
    )(o_lat.reshape(b, MLA_HEADS * KV_LORA), wuv_blk)


def _router_gates(h2, wr_ref, br_ref):
    h_hi = h2.astype(BF16)
    h_lo = (h2 - h_hi.astype(F32)).astype(BF16)
    logits = _dot(h_hi, wr_ref[0]) + (_dot(h_lo, wr_ref[0]) + _dot(h_hi, wr_ref[1]))
    scores = jax.nn.sigmoid(logits)
    sel = scores + br_ref[...]
    lane = lax.broadcasted_iota(jnp.int32, (1, LANES), 1)
    pos = lane % GROUP_SIZE
    others, wrapped = [], []
    for r in range(1, GROUP_SIZE):
        wrap = pos + r >= GROUP_SIZE
        fwd = pltpu.roll(sel, LANES - r, axis=1)
        bwd = pltpu.roll(sel, GROUP_SIZE - r, axis=1)
        others.append(jnp.where(wrap, bwd, fwd))
        wrapped.append(wrap)
    a, b, c, d = sel, others[0], others[1], others[2]
    hi1, lo1 = jnp.maximum(a, b), jnp.minimum(a, b)
    hi2, lo2 = jnp.maximum(c, d), jnp.minimum(c, d)
    gscore = jnp.maximum(hi1, hi2) + jnp.maximum(jnp.minimum(hi1, hi2), jnp.maximum(lo1, lo2))
    real = lane < N_EXPERTS
    gscore = jnp.where(real, gscore, -jnp.inf)
    gmax = jnp.max(gscore, axis=-1, keepdims=True)
    gidx = (lane // GROUP_SIZE).astype(F32)
    chosen = jnp.min(jnp.where(gscore == gmax, gidx, float(LANES)), axis=-1, keepdims=True)
    rank = jnp.zeros(sel.shape, F32)
    for o, wrap in zip(others, wrapped):
        rank = rank + jnp.where(wrap, jnp.where(o >= sel, 1.0, 0.0), jnp.where(o > sel, 1.0, 0.0))
    w = jnp.where(gidx == chosen, jnp.where(rank < 2.0, scores, 0.0), 0.0)
    return w / jnp.sum(w, axis=-1, keepdims=True)


def _post_kernel(*refs, n_mix):
    x_ref = refs[0]
    ys = refs[1:1 + n_mix]
    ws = refs[1 + n_mix:1 + 2 * n_mix]
    gt_ref, sh_ref, sc_ref, g_ref, wr_ref, br_ref, x1_ref, h2_ref, gate_ref = refs[1 + 2 * n_mix:]
    mix = None
    for y_ref, w_ref in zip(ys, ws):
        t = _dot(y_ref[...], w_ref[...])
        mix = t if mix is None else mix + t
    x1 = x_ref[...] + gt_ref[...] * mix
    x1_ref[...] = x1
    h2 = _rms(x1, g_ref[...]) * (1.0 + sc_ref[...]) + sh_ref[...]
    h2_ref[...] = h2.astype(BF16)
    gate_ref[...] = _router_gates(h2, wr_ref, br_ref)


def _post(x, ys, ws, gt, sh, sc, g, wr, br, tile):
    r = x.shape[0]
    per_row = gt.shape[0] != 1
    n_mix = len(ys)
    in_specs = ([pl.BlockSpec((tile, D_MODEL), lambda i: (i, 0))]
                + [pl.BlockSpec((tile, y.shape[1]), lambda i: (i, 0)) for y in ys]
                + [_full(w.shape) for w in ws]
                + [_rows(tile, D_MODEL, per_row)] * 3
                + [_full(g.shape), _full(wr.shape), _full(br.shape)])
    widths = [(D_MODEL, F32), (D_MODEL, BF16), (LANES, F32)]
    return pl.pallas_call(
        functools.partial(_post_kernel, n_mix=n_mix),
        grid=(r // tile,),
        in_specs=in_specs,
        out_specs=[pl.BlockSpec((tile, w), lambda i: (i, 0)) for w, _ in widths],
        out_shape=[jax.ShapeDtypeStruct((r, w), dt) for w, dt in widths],
        compiler_params=_cparams(("arbitrary",)),
        name="post_mixer",
    )(x, *ys, *ws, gt, sh, sc, g, wr, br)


def _moe_kernel(h_ref, gate_ref, x1_ref, gt_ref, wg_ref, wu_ref, wd_ref, o_ref, acc_ref):
    e = pl.program_id(1)

    @pl.when(e == 0)
    def _():
        acc_ref[...] = jnp.zeros_like(acc_ref)

    h = h_ref[...]
    a = _dot(h, wg_ref[0])
    u = _dot(h, wu_ref[0])
    lane = lax.broadcasted_iota(jnp.int32, (1, LANES), 1)
    gcol = jnp.sum(jnp.where(lane == e, gate_ref[...], 0.0), axis=-1, keepdims=True)
    act = (a * jax.nn.sigmoid(a)) * u * gcol
    acc_ref[...] += _dot(act.astype(BF16), wd_ref[0])

    @pl.when(e == pl.num_programs(1) - 1)
    def _():
        o_ref[...] = x1_ref[...] + gt_ref[...] * acc_ref[...]


def _moe(h2, gate, x1, gt, wg, wu, wd, tile):
    r = h2.shape[0]
    per_row = gt.shape[0] != 1
    gt_spec = (pl.BlockSpec((tile, D_MODEL), lambda i, e: (i, 0)) if per_row
               else pl.BlockSpec((1, D_MODEL), lambda i, e: (0, 0)))
    return pl.pallas_call(
        _moe_kernel,
        grid=(r // tile, N_EXPERTS),
        in_specs=[pl.BlockSpec((tile, D_MODEL), lambda i, e: (i, 0)),
                  pl.BlockSpec((tile, LANES), lambda i, e: (i, 0)),
                  pl.BlockSpec((tile, D_MODEL), lambda i, e: (i, 0)),
                  gt_spec,
                  pl.BlockSpec((1, D_MODEL, D_FF), lambda i, e: (e, 0, 0)),
                  pl.BlockSpec((1, D_MODEL, D_FF), lambda i, e: (e, 0, 0)),
                  pl.BlockSpec((1, D_FF, D_MODEL), lambda i, e: (e, 0, 0))],
        out_specs=pl.BlockSpec((tile, D_MODEL), lambda i, e: (i, 0)),
        out_shape=jax.ShapeDtypeStruct((r, D_MODEL), F32),
        scratch_shapes=[pltpu.VMEM((tile, D_MODEL), F32)],
        compiler_params=_cparams(("arbitrary", "arbitrary")),
        name="moe",
    )(h2, gate, x1, gt, wg, wu, wd)


def _front_odd_kernel(x_ref, sh_ref, sc_ref, g_ref, win_ref, gq_ref, gk_ref, rc_ref, rs1_ref, rs2_ref,
                      q_ref, k_ref, v_ref, ku_ref, vu_ref):
    x = x_ref[...]
    h = _rms(x, g_ref[...]) * (1.0 + sc_ref[...]) + sh_ref[...]
    z = _dot(h.astype(BF16), win_ref[...])
    rc, rs1, rs2 = rc_ref[...], rs1_ref[...], rs2_ref[...]
    scale = SWA_HD ** -0.5
    for hd in range(SWA_HEADS):
        sl = slice(hd * LANES, (hd + 1) * LANES)
        qn = _rope(_rms(z[:, sl], gq_ref[...], SWA_HD), SWA_ROT // 2, rc, rs1, rs2)
        q_ref[:, sl] = (qn * scale).astype(BF16)
    ks, vs = [], []
    for kh in range(SWA_KV):
        sl = slice(kh * LANES, (kh + 1) * LANES)
        zk = z[:, (SWA_HEADS + kh) * LANES:(SWA_HEADS + kh + 1) * LANES]
        kn = _rope(_rms(zk, gk_ref[...], SWA_HD), SWA_ROT // 2, rc, rs1, rs2)
        vv = z[:, (SWA_HEADS + SWA_KV + kh) * LANES:(SWA_HEADS + SWA_KV + kh + 1) * LANES]
        k_ref[:, sl] = kn.astype(BF16)
        v_ref[:, sl] = vv.astype(BF16)
        ks.append(kn)
        vs.append(vv)
    for j in range(SWA_KV // 2):
        sl = slice(j * LANES, (j + 1) * LANES)
        ku_ref[:, sl] = ks[2 * j] + pltpu.roll(ks[2 * j + 1], SWA_HD, axis=1)
        vu_ref[:, sl] = vs[2 * j] + pltpu.roll(vs[2 * j + 1], SWA_HD, axis=1)


def _front_odd(x, sh, sc, g, win, gq, gk, tabs, tile):
    r = x.shape[0]
    per_row_mod = sh.shape[0] != 1
    per_row_tab = tabs[0].shape[0] != 1
    in_specs = [pl.BlockSpec((tile, D_MODEL), lambda i: (i, 0)),
                _rows(tile, D_MODEL, per_row_mod), _rows(tile, D_MODEL, per_row_mod),
                _full(g.shape), _full(win.shape), _full(gq.shape), _full(gk.shape),
                _rows(tile, LANES, per_row_tab), _rows(tile, LANES, per_row_tab), _rows(tile, LANES, per_row_tab)]
    widths = [(SWA_HEADS * LANES, BF16), (SWA_KV * LANES, BF16), (SWA_KV * LANES, BF16),
              (SWA_KV * SWA_HD, F32), (SWA_KV * SWA_HD, F32)]
    return pl.pallas_call(
        _front_odd_kernel,
        grid=(r // tile,),
        in_specs=in_specs,
        out_specs=[pl.BlockSpec((tile, w), lambda i: (i, 0)) for w, _ in widths],
        out_shape=[jax.ShapeDtypeStruct((r, w), dt) for w, dt in widths],
        compiler_params=_cparams(("arbitrary",)),
        name="front_odd",
    )(x, sh, sc, g, win, gq, gk, *tabs)


def _swa_prompt_kernel(sink_ref, q_ref, kp_ref, kc_ref, vp_ref, vc_ref, o_ref):
    n = pl.program_id(0)
    w = WINDOW
    qi = lax.broadcasted_iota(jnp.int32, (w, 2 * w), 0) + w
    kj = lax.broadcasted_iota(jnp.int32, (w, 2 * w), 1)
    valid = (kj <= qi) & (qi - kj < w) & (n * w - w + kj >= 0)
    for kh in range(SWA_KV):
        sl = slice(kh * LANES, (kh + 1) * LANES)
        kk = jnp.concatenate([kp_ref[:, sl], kc_ref[:, sl]], axis=0)
        vv = jnp.concatenate([vp_ref[:, sl], vc_ref[:, sl]], axis=0)
        for gi in range(SWA_GROUP):
            hd = kh * SWA_GROUP + gi
            hsl = slice(hd * LANES, (hd + 1) * LANES)
            s = jnp.where(valid, _dot_nt(q_ref[:, hsl], kk), -jnp.inf)
            sink = sink_ref[hd]
            m = jnp.maximum(jnp.max(s, axis=-1, keepdims=True), sink)
            pr = jnp.exp(s - m)
            den = jnp.sum(pr, axis=-1, keepdims=True) + jnp.exp(sink - m)
            o_ref[:, hsl] = (_dot(pr.astype(BF16), vv) / den).astype(BF16)


def _swa_prompt_attn(q, k, v, sinks):
    t = q.shape[0]
    w = WINDOW
    prev = lambda n, s: (jnp.maximum(n - 1, 0), 0)
    cur = lambda n, s: (n, 0)
    grid_spec = pltpu.PrefetchScalarGridSpec(
        num_scalar_prefetch=1,
        grid=(t // w,),
        in_specs=[pl.BlockSpec((w, SWA_HEADS * LANES), cur),
                  pl.BlockSpec((w, SWA_KV * LANES), prev), pl.BlockSpec((w, SWA_KV * LANES), cur),
                  pl.BlockSpec((w, SWA_KV * LANES), prev), pl.BlockSpec((w, SWA_KV * LANES), cur)],
        out_specs=pl.BlockSpec((w, SWA_HEADS * LANES), cur),
    )
    return pl.pallas_call(
        _swa_prompt_kernel,
        grid_spec=grid_spec,
        out_shape=jax.ShapeDtypeStruct((t, SWA_HEADS * LANES), BF16),
        compiler_params=_cparams(("arbitrary",)),
        name="swa_prompt_attn",
    )(sinks, q, k, k, v, v)


def _swa_sample_kernel(q_ref, kc_ref, vc_ref, kn_ref, vn_ref, sink_ref, o_ref, wk_ref, wv_ref, *, sb):
    w = WINDOW
    width = SWA_KV * SWA_HD
    rowgrp = lax.broadcasted_iota(jnp.int32, (SWA_HEADS, 1), 0) // SWA_GROUP
    col = lax.broadcasted_iota(jnp.int32, (1, w), 1)
    low = lax.broadcasted_iota(jnp.int32, (1, LANES), 1) < SWA_HD
    sink = sink_ref[...]
    for b in range(sb):
        qf = q_ref[b].astype(F32)
        a = jnp.concatenate([qf, jnp.zeros_like(qf)], axis=1)
        qx = a
        for kh in range(1, SWA_KV):
            qx = jnp.where(rowgrp == kh, pltpu.roll(a, kh * SWA_HD, axis=1), qx)
        kc = kc_ref[b]
        vc = vc_ref[b]
        kn = kn_ref[b]
        vn = vn_ref[b]
        s = _dot_nt(qx.astype(BF16), kc.astype(BF16))
        s = jnp.where(col >= 1, s, -jnp.inf)
        s_new = jnp.sum(qx.astype(BF16).astype(F32) * kn.astype(BF16).astype(F32), axis=-1, keepdims=True)
        m = jnp.maximum(jnp.maximum(jnp.max(s, axis=-1, keepdims=True), s_new), sink)
        pr = jnp.exp(s - m)
        pn = jnp.exp(s_new - m)
        den = jnp.sum(pr, axis=-1, keepdims=True) + pn + jnp.exp(sink - m)
        o = (_dot(pr.astype(BF16), vc.astype(BF16)) + pn * vn.astype(BF16).astype(F32)) / den
        ox = o
        for kh in range(1, SWA_KV):
            ox = jnp.where(rowgrp == kh, pltpu.roll(o, width - kh * SWA_HD, axis=1), ox)
        o_ref[b] = jnp.where(low, ox[:, :LANES], 0.0).astype(BF16)
        wk_ref[b, pl.ds(0, w - 1), :] = kc_ref[b, pl.ds(1, w - 1), :]
        wk_ref[b, pl.ds(w - 1, 1), :] = kn
        wv_ref[b, pl.ds(0, w - 1), :] = vc_ref[b, pl.ds(1, w - 1), :]
        wv_ref[b, pl.ds(w - 1, 1), :] = vn


def _swa_sample_attn(q, kc, vc, kn, vn, sinks, sb):
    b = q.shape[0]
    w = WINDOW
    width = SWA_KV * SWA_HD
    blk3 = lambda s1, s2: pl.BlockSpec((sb, s1, s2), lambda i: (i, 0, 0))
    return pl.pallas_call(
        functools.partial(_swa_sample_kernel, sb=sb),
        grid=(b // sb,),
        in_specs=[blk3(SWA_HEADS, LANES), blk3(w, width), blk3(w, width), blk3(1, width), blk3(1, width),
                  _full((SWA_HEADS, 1))],
        out_specs=[blk3(SWA_HEADS, LANES), blk3(w, width), blk3(w, width)],
        out_shape=[jax.ShapeDtypeStruct((b, SWA_HEADS, LANES), BF16),
                   jax.ShapeDtypeStruct((b, w, width), F32), jax.ShapeDtypeStruct((b, w, width), F32)],
        compiler_params=_cparams(("arbitrary",)),
        name="swa_sample_attn",
    )(q.reshape(b, SWA_HEADS, LANES), kc, vc, kn.reshape(b, 1, width), vn.reshape(b, 1, width),
      sinks.reshape(SWA_HEADS, 1))


def _rope_tables(pos, rot, offset):
    half = rot // 2
    inv = ROPE_THETA ** (-jnp.arange(half, dtype=F32) / half)
    ang = pos.astype(F32)[:, None] * inv[None, :]
    cos, sin = jnp.cos(ang), jnp.sin(ang)
    n = pos.shape[0]
    rc = jnp.ones((n, LANES), F32).at[:, offset:offset + half].set(cos).at[:, offset + half:offset + rot].set(cos)
    rs1 = jnp.zeros((n, LANES), F32).at[:, offset:offset + half].set(-sin)
    rs2 = jnp.zeros((n, LANES), F32).at[:, offset + half:offset + rot].set(sin)
    return rc, rs1, rs2


def _pad_lanes(v, offset=0):
    return jnp.zeros((1, LANES), F32).at[0, offset:offset + v.shape[0]].set(v)


def _even_weights(w_in, g_qa, w_uq, g_qh, g_kva, g_kr, w_uk, w_uv):
    o3 = S5_WIDTH + Q_LORA + KV_LORA
    win = jnp.zeros((D_MODEL, 1280), F32).at[:, :o3].set(w_in[:, :o3])
    win = win.at[:, o3 + MLA_NOPE:o3 + MLA_QK].set(w_in[:, o3:])
    wuq = jnp.zeros((MLA_HEADS, Q_LORA, LANES), F32).at[:, :, :MLA_QK].set(w_uq.transpose(1, 0, 2))
    wuk = jnp.zeros((KV_LORA, MLA_HEADS, LANES), F32).at[:, :, :MLA_NOPE].set(w_uk)
    front = (win.astype(BF16), g_qa.reshape(1, Q_LORA), wuq.astype(BF16), _pad_lanes(g_qh),
             g_kva.reshape(1, KV_LORA), _pad_lanes(g_kr, MLA_NOPE),
             wuk.reshape(KV_LORA, MLA_HEADS * LANES).astype(BF16),
             w_uv.reshape(KV_LORA, MLA_HEADS * MLA_V).astype(BF16))
    wukt = jnp.zeros((MLA_HEADS, LANES, KV_LORA), F32).at[:, :MLA_NOPE, :].set(w_uk.transpose(1, 2, 0))
    eye = jnp.eye(MLA_HEADS, dtype=F32)
    wuv_blk = jnp.einsum('chd,hg->hcgd', w_uv, eye).reshape(MLA_HEADS * KV_LORA, MLA_HEADS * MLA_V)
    return front, wukt.astype(BF16), wuv_blk.astype(BF16)


def _odd_weights(w_in, w_out):
    nq, nk = SWA_HEADS * SWA_HD, SWA_KV * SWA_HD
    nh = SWA_HEADS + 2 * SWA_KV
    win = jnp.zeros((D_MODEL, nh, LANES), F32).at[:, :, :SWA_HD].set(w_in.reshape(D_MODEL, nh, SWA_HD))
    wout = jnp.zeros((SWA_HEADS, LANES, D_MODEL), F32).at[:, :SWA_HD, :].set(w_out.reshape(SWA_HEADS, SWA_HD, D_MODEL))
    del nq, nk
    return win.reshape(D_MODEL, nh * LANES).astype(BF16), wout.reshape(SWA_HEADS * LANES, D_MODEL).astype(BF16)


def kernel(x_prompt, x_sample, c_prompt, c_sample, cache_ckv, cache_krope, page_table, state_s5_re, state_s5_im, cache_win_k, cache_win_v, w_mod, b_mod, g_norm_mix, g_norm_ffn, w_in_even, w_out_even, s5_lam_re, s5_lam_im, s5_log_step, s5_b_re, s5_b_im, s5_c_re, s5_c_im, s5_d, s5_w_glu, mla_g_qa, mla_w_uq, mla_g_qh, mla_g_kva, mla_g_kr, mla_w_uk, mla_w_uv, w_in_odd, w_out_odd, swa_g_q, swa_g_k, swa_sinks, w_router, b_router, moe_w_gate, moe_w_up, moe_w_down):
    t = x_prompt.shape[1]
    nb = x_sample.shape[0]
    past_len = page_table.shape[1] * PAGE
    row_tile = 512
    s5_seg = 64
    attn_tile = attn_sub = 2048
    mla_scale = MLA_QK ** -0.5
    moe_tile = 1024
    sample_sub = past_len
    swa_seq_block = 8

    n_c = 1 + nb
    rp = -(-n_c // SUBLANES) * SUBLANES
    c_all = jnp.concatenate([c_prompt, c_sample, jnp.zeros((rp - n_c, D_MODEL), F32)], axis=0)
    mod = _modulation(c_all, w_mod, b_mod)

    def mods(layer, sample):
        rows = mod[layer, 1:1 + nb] if sample else mod[layer, 0:1]
        return [rows[:, k * D_MODEL:(k + 1) * D_MODEL] for k in range(6)]

    wr = jnp.zeros((D_MODEL, LANES), F32).at[:, :N_EXPERTS].set(w_router)
    wr_hi = wr.astype(BF16)
    wr = jnp.stack([wr_hi, (wr - wr_hi.astype(F32)).astype(BF16)])
    br = _pad_lanes(b_router)
    wg, wu, wd = moe_w_gate.astype(BF16), moe_w_up.astype(BF16), moe_w_down.astype(BF16)

    xp = x_prompt.reshape(t, D_MODEL)
    xs = x_sample.reshape(nb, D_MODEL)
    pos_p = jnp.arange(t)
    pos_s = jnp.full((1,), past_len)

    front_w, wukt, wuv_blk = _even_weights(w_in_even[0], mla_g_qa[0], mla_w_uq[0], mla_g_qh[0], mla_g_kva[0],
                                           mla_g_kr[0], mla_w_uk[0], mla_w_uv[0])
    a_re, a_im, bb_re, bb_im, al_re, al_im = _s5_discretize(s5_lam_re[0], s5_lam_im[0], s5_log_step[0],
                                                            s5_b_re[0], s5_b_im[0], s5_seg)
    bre, bim, cre, cim = _s5_layouts(bb_re, bb_im, s5_c_re[0], s5_c_im[0])
    d_skip = s5_d[0].reshape(1, S5_WIDTH)
    wglu = s5_w_glu[0].astype(BF16)
    wo_ssm = w_out_even[0, :S5_WIDTH].astype(BF16)
    wo_att = w_out_even[0, S5_WIDTH:].astype(BF16)
    g_mix0 = g_norm_mix[0].reshape(1, D_MODEL)
    g_ffn0 = g_norm_ffn[0].reshape(1, D_MODEL)

    sh1, sc1, gt1, sh2, sc2, gt2 = mods(0, False)
    u, q, k, v, ckv_p, kr_p = _front_even(xp, sh1, sc1, g_mix0, front_w, _rope_tables(pos_p, MLA_ROPE, MLA_NOPE),
                                          row_tile, mla_scale * math.log2(math.e))
    y_ssm, st = _s5_scan(u, bre, bim, a_re, a_im, al_re, al_im, cre, cim, d_skip, wglu, s5_seg)
    sr_p = st[:S5_CHUNKS].reshape(1, 1, S5_GROUPS, S5_STATE)
    si_p = st[S5_CHUNKS:].reshape(1, 1, S5_GROUPS, S5_STATE)
    y_att = _mla_prompt_attn(q, k, v, attn_tile, attn_sub)
    x1, h2, gate = _post(xp, [y_ssm, y_att], [wo_ssm, wo_att], gt1, sh2, sc2, g_ffn0, wr, br, row_tile)
    xp = _moe(h2, gate, x1, gt2, wg[0], wu[0], wd[0], moe_tile)

    sh1, sc1, gt1, sh2, sc2, gt2 = mods(0, True)
    u, q, _, _, ckv_s, kr_s = _front_even(xs, sh1, sc1, g_mix0, front_w, _rope_tables(pos_s, MLA_ROPE, MLA_NOPE), nb,
                                          mla_scale)
    y_ssm, sr_s, si_s = _s5_step(u, state_s5_re[0].reshape(nb, S5_LANES), state_s5_im[0].reshape(nb, S5_LANES),
                                 bre, bim, a_re, a_im, cre, cim, d_skip, wglu)
    y_att = _mla_sample_attn(q, ckv_s, kr_s, cache_ckv, cache_krope, page_table, wukt, wuv_blk, sample_sub)
    x1, h2, gate = _post(xs, [y_ssm, y_att], [wo_ssm, wo_att], gt1, sh2, sc2, g_ffn0, wr, br, nb)
    xs = _moe(h2, gate, x1, gt2, wg[0], wu[0], wd[0], nb)

    win_odd, wout_odd = _odd_weights(w_in_odd[0], w_out_odd[0])
    gq, gk = _pad_lanes(swa_g_q[0]), _pad_lanes(swa_g_k[0])
    g_mix1 = g_norm_mix[1].reshape(1, D_MODEL)
    g_ffn1 = g_norm_ffn[1].reshape(1, D_MODEL)
    sinks = swa_sinks[0]

    sh1, sc1, gt1, sh2, sc2, gt2 = mods(1, False)
    q, k, v, ku, vu = _front_odd(xp, sh1, sc1, g_mix1, win_odd, gq, gk, _rope_tables(pos_p, SWA_ROT, 0), row_tile)
    o = _swa_prompt_attn(q, k, v, sinks)
    wk_p = ku[t - WINDOW:].reshape(1, 1, WINDOW, SWA_KV, SWA_HD)
    wv_p = vu[t - WINDOW:].reshape(1, 1, WINDOW, SWA_KV, SWA_HD)
    x1, h2, gate = _post(xp, [o], [wout_odd], gt1, sh2, sc2, g_ffn1, wr, br, row_tile)
    xp = _moe(h2, gate, x1, gt2, wg[1], wu[1], wd[1], moe_tile)

    sh1, sc1, gt1, sh2, sc2, gt2 = mods(1, True)
    q, _, _, ku, vu = _front_odd(xs, sh1, sc1, g_mix1, win_odd, gq, gk, _rope_tables(pos_s, SWA_ROT, 0), nb)
    width = SWA_KV * SWA_HD
    o, wk_s, wv_s = _swa_sample_attn(q, cache_win_k[0].reshape(nb, WINDOW, width),
                                     cache_win_v[0].reshape(nb, WINDOW, width), ku, vu, sinks, swa_seq_block)
    x1, h2, gate = _post(xs, [o.reshape(nb, SWA_HEADS * LANES)], [wout_odd], gt1, sh2, sc2, g_ffn1, wr, br, nb)
    xs = _moe(h2, gate, x1, gt2, wg[1], wu[1], wd[1], nb)

    return (xp.reshape(1, t, D_MODEL), xs.reshape(nb, 1, D_MODEL),
            ckv_p.reshape(1, 1, t, KV_LORA), kr_p.reshape(1, 1, t, MLA_ROPE), sr_p, si_p, wk_p, wv_p,
            ckv_s.reshape(1, nb, 1, KV_LORA), kr_s.reshape(1, nb, 1, MLA_ROPE),
            sr_s.reshape(1, nb, S5_GROUPS, S5_STATE), si_s.reshape(1, nb, S5_GROUPS, S5_STATE),
            wk_s.reshape(1, nb, WINDOW, SWA_KV, SWA_HD), wv_s.reshape(1, nb, WINDOW, SWA_KV, SWA_HD))
```

```python
import functools
import math

import jax
import jax.numpy as jnp
from jax import lax
from jax.experimental import pallas as pl
from jax.experimental.pallas import tpu as pltpu

F32 = jnp.float32
BF16 = jnp.bfloat16
EPS = 1e-6
ROPE_THETA = 500000.0
LANES = 128
SUBLANES = 8
VMEM_LIMIT = 56 * 1024 * 1024

D_MODEL = 1024
PAGE = 128
S5_WIDTH = 512
S5_GROUP = 16
S5_GROUPS = 32
S5_STATE = 64
S5_LANES = S5_GROUPS * S5_STATE
S5_CHUNKS = S5_LANES // LANES
MLA_HEADS = 8
MLA_NOPE = 64
MLA_ROPE = 32
MLA_QK = 96
MLA_V = 64
Q_LORA = 384
KV_LORA = 256
SWA_HEADS = 16
SWA_KV = 4
SWA_GROUP = 4
SWA_HD = 64
SWA_ROT = 16
WINDOW = 128
N_EXPERTS = 16
GROUP_SIZE = 4
D_FF = 256

_NT = (((1,), (1,)), ((), ()))


def _dot(a, b):
    return jnp.dot(a, b, preferred_element_type=F32)


def _dot_nt(a, b):
    return lax.dot_general(a, b, _NT, preferred_element_type=F32)


def _rms(x, g, n=None):
    n = x.shape[-1] if n is None else n
    ss = jnp.sum(x * x, axis=-1, keepdims=True) * (1.0 / n)
    return x * lax.rsqrt(ss + EPS) * g


def _rope(x, shift, rc, rs1, rs2):
    return (x * rc + pltpu.roll(x, LANES - shift, axis=1) * rs1
            + pltpu.roll(x, shift, axis=1) * rs2)


def _cparams(sem=None, vmem=VMEM_LIMIT):
    return pltpu.CompilerParams(dimension_semantics=sem, vmem_limit_bytes=vmem)


def _full(shape):
    nd = len(shape)
    return pl.BlockSpec(shape, lambda *_: (0,) * nd)


def _rows(tile, width, per_row):
    if per_row:
        return pl.BlockSpec((tile, width), lambda i: (i, 0))
    return pl.BlockSpec((1, width), lambda i: (0, 0))


def _mod_kernel(c_ref, w_ref, b_ref, o_ref):
    c = c_ref[...]
    s = (c * jax.nn.sigmoid(c)).astype(BF16)
    o_ref[0] = _dot(s, w_ref[0].astype(BF16)) + b_ref[0]


def _modulation(c_all, w_mod, b_mod):
    depth, _, n = w_mod.shape
    rp = c_all.shape[0]
    tn = 1536
    return pl.pallas_call(
        _mod_kernel,
        grid=(depth, n // tn),
        in_specs=[pl.BlockSpec((rp, D_MODEL), lambda l, j: (0, 0)),
                  pl.BlockSpec((1, D_MODEL, tn), lambda l, j: (l, 0, j)),
                  pl.BlockSpec((1, 1, tn), lambda l, j: (l, 0, j))],
        out_specs=pl.BlockSpec((1, rp, tn), lambda l, j: (l, 0, j)),
        out_shape=jax.ShapeDtypeStruct((depth, rp, n), F32),
        compiler_params=_cparams(("arbitrary", "arbitrary")),
        name="modulation",
    )(c_all, w_mod, b_mod.reshape(depth, 1, n))


def _front_even_kernel(x_ref, sh_ref, sc_ref, g_ref, win_ref, gqa_ref, wuq_ref, gqh_ref, gkva_ref,
                       gkr_ref, wuk_ref, wuv_ref, rc_ref, rs1_ref, rs2_ref,
                       u_ref, q_ref, k_ref, v_ref, ckv_ref, kr_ref, *, q_scale):
    x = x_ref[...]
    h = _rms(x, g_ref[...]) * (1.0 + sc_ref[...]) + sh_ref[...]
    z = _dot(h.astype(BF16), win_ref[...])
    u_ref[...] = z[:, :512]
    c_q = _rms(z[:, 512:896], gqa_ref[...]).astype(BF16)
    c_kv = _rms(z[:, 896:1152], gkva_ref[...])
    ckv_ref[...] = c_kv
    rc, rs1, rs2 = rc_ref[...], rs1_ref[...], rs2_ref[...]
    kr = _rope(_rms(z[:, 1152:1280], gkr_ref[...], MLA_ROPE), MLA_ROPE // 2, rc, rs1, rs2)
    kr_ref[...] = kr[:, MLA_NOPE:MLA_QK]
    ckvb = c_kv.astype(BF16)
    kn = _dot(ckvb, wuk_ref[...])
    v_ref[...] = _dot(ckvb, wuv_ref[...]).astype(BF16)
    for hd in range(MLA_HEADS):
        sl = slice(hd * LANES, (hd + 1) * LANES)
        qh = _dot(c_q, wuq_ref[hd])
        qn = _rope(_rms(qh, gqh_ref[...], MLA_QK), MLA_ROPE // 2, rc, rs1, rs2)
        q_ref[:, sl] = (qn * q_scale).astype(BF16)
        k_ref[:, sl] = (kn[:, sl] + kr).astype(BF16)


def _front_even(x, sh, sc, g, wts, tabs, tile, q_scale):
    r = x.shape[0]
    per_row_mod = sh.shape[0] != 1
    per_row_tab = tabs[0].shape[0] != 1
    win, gqa, wuq, gqh, gkva, gkr, wuk, wuv = wts
    in_specs = [pl.BlockSpec((tile, D_MODEL), lambda i: (i, 0)),
                _rows(tile, D_MODEL, per_row_mod), _rows(tile, D_MODEL, per_row_mod),
                _full(g.shape), _full(win.shape), _full(gqa.shape), _full(wuq.shape), _full(gqh.shape),
                _full(gkva.shape), _full(gkr.shape), _full(wuk.shape), _full(wuv.shape),
                _rows(tile, LANES, per_row_tab), _rows(tile, LANES, per_row_tab), _rows(tile, LANES, per_row_tab)]
    widths = [(S5_WIDTH, F32), (MLA_HEADS * LANES, BF16), (MLA_HEADS * LANES, BF16),
              (MLA_HEADS * MLA_V, BF16), (KV_LORA, F32), (MLA_ROPE, F32)]
    return pl.pallas_call(
        functools.partial(_front_even_kernel, q_scale=q_scale),
        grid=(r // tile,),
        in_specs=in_specs,
        out_specs=[pl.BlockSpec((tile, w), lambda i: (i, 0)) for w, _ in widths],
        out_shape=[jax.ShapeDtypeStruct((r, w), dt) for w, dt in widths],
        compiler_params=_cparams(("arbitrary",)),
        name="front_even",
    )(x, sh, sc, g, win, gqa, wuq, gqh, gkva, gkr, wuk, wuv, *tabs)


def _s5_disc_kernel(lr_ref, li_ref, ls_ref, br_ref, bi_ref, are_ref, aim_ref, bbr_ref, bbi_ref,
                    alr_ref, ali_ref, *, log2_len):
    lr, li = lr_ref[...], li_ref[...]
    dt = jnp.exp(ls_ref[...])
    mag = jnp.exp(lr * dt)
    ang = li * dt
    a_re = mag * jnp.cos(ang)
    a_im = mag * jnp.sin(ang)
    den = lr * lr + li * li
    k_re = ((a_re - 1.0) * lr + a_im * li) / den
    k_im = (a_im * lr - (a_re - 1.0) * li) / den
    are_ref[...] = a_re
    aim_ref[...] = a_im
    for c in range(S5_GROUP):
        br, bi = br_ref[c], bi_ref[c]
        bbr_ref[c] = k_re * br - k_im * bi
        bbi_ref[c] = k_re * bi + k_im * br
    pr, pi = a_re, a_im
    for _ in range(log2_len):
        pr, pi = pr * pr - pi * pi, 2.0 * pr * pi
    alr_ref[...] = pr
    ali_ref[...] = pi


def _s5_discretize(lam_re, lam_im, log_step, b_re, b_im, seg_len):
    g, n = lam_re.shape
    outs = [jax.ShapeDtypeStruct((g, n), F32)] * 2 + [jax.ShapeDtypeStruct((S5_GROUP, g, n), F32)] * 2 \
        + [jax.ShapeDtypeStruct((g, n), F32)] * 2
    return pl.pallas_call(
        functools.partial(_s5_disc_kernel, log2_len=int(math.log2(seg_len))),
        out_shape=outs,
        name="s5_discretize",
    )(lam_re, lam_im, log_step.reshape(g, 1), b_re.transpose(2, 0, 1), b_im.transpose(2, 0, 1))


def _s5_layouts(bb_re, bb_im, c_re, c_im):
    eye8 = jnp.eye(8, dtype=F32)

    def bmat(bb):
        b4 = bb.reshape(S5_GROUP, 4, 8, S5_STATE)
        m = jnp.einsum('cjgn,gh->jgchn', b4, eye8)
        return m.reshape(4, LANES, 512).astype(BF16)

    sel = jax.nn.one_hot((2 * jnp.arange(S5_CHUNKS)[:, None] + jnp.arange(2)[None, :]) % 8, 8, dtype=F32)

    def cmat(c):
        c4 = c.reshape(S5_CHUNKS, 2, S5_GROUP, S5_STATE)
        m = jnp.einsum('asck,asg->askgc', c4, sel)
        return m.reshape(S5_CHUNKS, LANES, LANES).astype(BF16)

    return bmat(bb_re), bmat(bb_im), cmat(c_re), cmat(-c_im)


def _gelu_glu(y, wglu_ref):
    z = jax.nn.gelu(y)
    return z * jax.nn.sigmoid(_dot(z.astype(BF16), wglu_ref[...]))


def _s5_scan_kernel(u0_ref, u1_ref, u2_ref, u3_ref, perm_ref, bre_ref, bim_ref, are_ref, aim_ref, alr_ref,
                    ali_ref, cre_ref, cim_ref, d_ref, wglu_ref, y_ref, st_ref, bu_ref, hs_ref, carry_ref,
                    *, seg_len):
    i = pl.program_id(0)

    @pl.when(i == 0)
    def _():
        carry_ref[...] = jnp.zeros_like(carry_ref)

    us = []
    for j, u_ref in enumerate((u0_ref, u1_ref, u2_ref, u3_ref)):
        uj = jnp.concatenate([u_ref[pl.ds(t, SUBLANES, stride=seg_len), :] for t in range(seg_len)], axis=0)
        us.append(uj)
        ub = uj.astype(BF16)
        re = _dot(ub, bre_ref[j])
        im = _dot(ub, bim_ref[j])
        for q in range(4):
            bu_ref[4 * j + q] = re[:, q * LANES:(q + 1) * LANES]
            bu_ref[S5_CHUNKS + 4 * j + q] = im[:, q * LANES:(q + 1) * LANES]

    per = 4
    for grp in range(S5_CHUNKS // per):
        cs = [grp * per + q for q in range(per)]
        ar = [jnp.broadcast_to(are_ref[c], (SUBLANES, LANES)) for c in cs]
        ai = [jnp.broadcast_to(aim_ref[c], (SUBLANES, LANES)) for c in cs]

        def advance(t, hs, store):
            out = []
            for k, c in enumerate(cs):
                hr, hi = hs[2 * k], hs[2 * k + 1]
                rows = pl.ds(pl.multiple_of(t * SUBLANES, SUBLANES), SUBLANES)
                nr = ar[k] * hr - ai[k] * hi + bu_ref[c, rows, :]
                ni = ar[k] * hi + ai[k] * hr + bu_ref[S5_CHUNKS + c, rows, :]
                if store:
                    hs_ref[c, rows, :] = nr
                    hs_ref[S5_CHUNKS + c, rows, :] = ni
                out += [nr, ni]
            return tuple(out)

        zero = tuple(jnp.zeros((SUBLANES, LANES), F32) for _ in range(2 * per))
        ends = lax.fori_loop(0, seg_len, lambda t, hs: advance(t, hs, False), zero, unroll=2)
        init = []
        for k, c in enumerate(cs):
            er, ei = ends[2 * k], ends[2 * k + 1]
            lr, li = alr_ref[c], ali_ref[c]
            hr, hi = carry_ref[c], carry_ref[S5_CHUNKS + c]
            rows_r, rows_i = [], []
            for s in range(SUBLANES):
                rows_r.append(hr)
                rows_i.append(hi)
                hr, hi = (er[s:s + 1] + lr * hr - li * hi, ei[s:s + 1] + lr * hi + li * hr)
            carry_ref[c] = hr
            carry_ref[S5_CHUNKS + c] = hi
            init += [jnp.concatenate(rows_r, axis=0), jnp.concatenate(rows_i, axis=0)]
        lax.fori_loop(0, seg_len, lambda t, hs: advance(t, hs, True), tuple(init), unroll=2)

    ys = []
    for j in range(4):
        acc = None
        for q in range(4):
            c = 4 * j + q
            t = _dot(hs_ref[c].astype(BF16), cre_ref[c]) + _dot(hs_ref[S5_CHUNKS + c].astype(BF16), cim_ref[c])
            acc = t if acc is None else acc + t
        ys.append(acc)
    y = jnp.concatenate(ys, axis=1) + d_ref[...] * jnp.concatenate(us, axis=1)
    gated = _gelu_glu(y, wglu_ref).astype(BF16)
    y_ref[...] = _dot(perm_ref[...], gated).astype(BF16)

    @pl.when(i == pl.num_programs(0) - 1)
    def _():
        st_ref[...] = carry_ref[...]


def _s5_scan(u, bre, bim, a_re, a_im, al_re, al_im, cre, cim, d, wglu, seg_len):
    t = u.shape[0]
    tile = SUBLANES * seg_len
    ch = lambda a: a.reshape(S5_CHUNKS, 1, LANES)
    r = jnp.arange(tile)
    perm = jax.nn.one_hot(SUBLANES * (r % seg_len) + r // seg_len, tile, dtype=BF16)
    args = (u, u, u, u, perm, bre, bim, ch(a_re), ch(a_im), ch(al_re), ch(al_im), cre, cim, d, wglu)
    in_specs = ([pl.BlockSpec((tile, LANES), functools.partial(lambda i, j: (i, j), j=j)) for j in range(4)]
                + [_full(a.shape) for a in args[4:]])
    return pl.pallas_call(
        functools.partial(_s5_scan_kernel, seg_len=seg_len),
        grid=(t // tile,),
        in_specs=in_specs,
        out_specs=[pl.BlockSpec((tile, S5_WIDTH), lambda i: (i, 0)), _full((2 * S5_CHUNKS, 1, LANES))],
        out_shape=[jax.ShapeDtypeStruct((t, S5_WIDTH), BF16),
                   jax.ShapeDtypeStruct((2 * S5_CHUNKS, 1, LANES), F32)],
        scratch_shapes=[pltpu.VMEM((2 * S5_CHUNKS, tile, LANES), F32),
                        pltpu.VMEM((2 * S5_CHUNKS, tile, LANES), F32),
                        pltpu.VMEM((2 * S5_CHUNKS, 1, LANES), F32)],
        compiler_params=_cparams(("arbitrary",)),
        name="s5_scan",
    )(*args)


def _s5_step_kernel(u_ref, h0r_ref, h0i_ref, bre_ref, bim_ref, are_ref, aim_ref, cre_ref, cim_ref,
                    d_ref, wglu_ref, y_ref, sr_ref, si_ref):
    u = u_ref[...]
    ub = u.astype(BF16)
    ys = []
    for j in range(4):
        uj = ub[:, j * LANES:(j + 1) * LANES]
        re = _dot(uj, bre_ref[j])
        im = _dot(uj, bim_ref[j])
        acc = None
        for q in range(4):
            sl = slice((4 * j + q) * LANES, (4 * j + q + 1) * LANES)
            ar, ai = are_ref[:, sl], aim_ref[:, sl]
            h0r, h0i = h0r_ref[:, sl], h0i_ref[:, sl]
            hr = ar * h0r - ai * h0i + re[:, q * LANES:(q + 1) * LANES]
            hi = ar * h0i + ai * h0r + im[:, q * LANES:(q + 1) * LANES]
            sr_ref[:, sl] = hr
            si_ref[:, sl] = hi
            t = _dot(hr.astype(BF16), cre_ref[4 * j + q]) + _dot(hi.astype(BF16), cim_ref[4 * j + q])
            acc = t if acc is None else acc + t
        ys.append(acc)
    y = jnp.concatenate(ys, axis=1) + d_ref[...] * u
    y_ref[...] = _gelu_glu(y, wglu_ref).astype(BF16)


def _s5_step(u, h0r, h0i, bre, bim, a_re, a_im, cre, cim, d, wglu):
    b = u.shape[0]
    return pl.pallas_call(
        _s5_step_kernel,
        out_shape=[jax.ShapeDtypeStruct((b, S5_WIDTH), BF16), jax.ShapeDtypeStruct((b, S5_LANES), F32),
                   jax.ShapeDtypeStruct((b, S5_LANES), F32)],
        compiler_params=_cparams(),
        name="s5_step",
    )(u, h0r, h0i, bre, bim, a_re.reshape(1, S5_LANES), a_im.reshape(1, S5_LANES), cre, cim, d, wglu)


def _mla_prompt_kernel(iq_ref, jk_ref, q_ref, k_ref, v_ref, o_ref, m_ref, acc_ref, *, hps):
    p = pl.program_id(1)
    i, j = iq_ref[p], jk_ref[p]
    tq, tk = q_ref.shape[0], k_ref.shape[0]
    low = lax.broadcasted_iota(jnp.int32, (1, LANES), 1) < MLA_V

    @pl.when(j == 0)
    def _():
        m_ref[...] = jnp.full_like(m_ref, -jnp.inf)
        acc_ref[...] = jnp.zeros_like(acc_ref)

    def scores(hh):
        sl = slice(hh * LANES, (hh + 1) * LANES)
        return _dot_nt(q_ref[:, sl], k_ref[:, sl])

    def block(diagonal):
        s_next = scores(0)
        for hh in range(hps):
            s = s_next
            if hh + 1 < hps:
                s_next = scores(hh + 1)
            v = v_ref[:, (hh // 2) * LANES:(hh // 2 + 1) * LANES]
            vh = jnp.where(low, v, jnp.ones_like(v)) if hh % 2 == 0 else jnp.where(low, jnp.ones_like(v), v)
            if diagonal:
                row = lax.broadcasted_iota(jnp.int32, (tq, tk), 0)
                col = lax.broadcasted_iota(jnp.int32, (tq, tk), 1)
                s = jnp.where(col <= row, s, -jnp.inf)
            m_prev = m_ref[hh]
            m_new = jnp.maximum(m_prev, jnp.max(s, axis=-1, keepdims=True))
            pr = jnp.exp2(s - m_new).astype(BF16)
            acc_ref[hh] = jnp.exp2(m_prev - m_new) * acc_ref[hh] + _dot(pr, vh)
            m_ref[hh] = m_new

    @pl.when(j < i)
    def _():
        block(False)

    @pl.when(j == i)
    def _():
        block(True)
        for pair in range(hps // 2):
            a0, a1 = acc_ref[2 * pair], acc_ref[2 * pair + 1]
            o = jnp.where(low, a0 / pltpu.roll(a0, MLA_V, axis=1), a1 / pltpu.roll(a1, MLA_V, axis=1))
            o_ref[:, pair * LANES:(pair + 1) * LANES] = o.astype(BF16)


def _mla_prompt_attn(q, k, v, tile, hps):
    t = q.shape[0]
    nq = t // tile
    pairs = [(i, j) for i in range(nq) for j in range(i + 1)]
    iq = jnp.asarray([p[0] for p in pairs], jnp.int32)
    jk = jnp.asarray([p[1] for p in pairs], jnp.int32)
    grid_spec = pltpu.PrefetchScalarGridSpec(
        num_scalar_prefetch=2,
        grid=(MLA_HEADS // hps, len(pairs)),
        in_specs=[pl.BlockSpec((tile, hps * LANES), lambda h, p, iq, jk: (iq[p], h)),
                  pl.BlockSpec((tile, hps * LANES), lambda h, p, iq, jk: (jk[p], h)),
                  pl.BlockSpec((tile, hps * MLA_V), lambda h, p, iq, jk: (jk[p], h))],
        out_specs=pl.BlockSpec((tile, hps * MLA_V), lambda h, p, iq, jk: (iq[p], h)),
        scratch_shapes=[pltpu.VMEM((hps, tile, 1), F32), pltpu.VMEM((hps, tile, LANES), F32)],
    )
    return pl.pallas_call(
        functools.partial(_mla_prompt_kernel, hps=hps),
        grid_spec=grid_spec,
        out_shape=jax.ShapeDtypeStruct((t, MLA_HEADS * MLA_V), BF16),
        compiler_params=_cparams(("arbitrary", "arbitrary")),
        name="mla_prompt_attn",
    )(iq, jk, q, k, v)


def _qlat_kernel(q_ref, wukt_ref, o_ref):
    for hd in range(MLA_HEADS):
        o_ref[hd] = _dot(q_ref[:, hd * LANES:(hd + 1) * LANES], wukt_ref[hd]).astype(BF16)


def _uv_kernel(o_ref, w_ref, y_ref):
    y_ref[...] = _dot(o_ref[...], w_ref[...]).astype(BF16)


def _mla_sample_kernel(pt_ref, ql_ref, qr_ref, cn_ref, kn_ref, ckv_hbm, krt_hbm, o_ref,
                       cbuf, kbuf, sem, *, npages, sub):
    b = pl.program_id(0)
    nb = pl.num_programs(0)
    slot = lax.rem(b, 2)

    def copies(bb, sl):
        out = []
        for p in range(npages):
            page = pt_ref[bb, p]
            tok = pl.ds(p * PAGE, PAGE)
            out.append(pltpu.make_async_copy(ckv_hbm.at[0, page], cbuf.at[sl, tok], sem.at[sl, 0]))
            out.append(pltpu.make_async_copy(krt_hbm.at[0, page], kbuf.at[sl, :, tok], sem.at[sl, 1]))
        return out

    @pl.when(b == 0)
    def _():
        for cp in copies(b, slot):
            cp.start()

    @pl.when(b + 1 < nb)
    def _():
        for cp in copies(b + 1, 1 - slot):
            cp.start()

    ql = ql_ref[0]
    qr = qr_ref[0]
    cn = cn_ref[0].astype(BF16).astype(F32)
    kn = kn_ref[0].astype(BF16).astype(F32)
    s_new = (jnp.sum(ql.astype(F32) * cn, axis=-1, keepdims=True)
             + jnp.sum(qr.astype(F32) * kn, axis=-1, keepdims=True))

    for cp in copies(b, slot):
        cp.wait()

    parts = []
    for sc in range(npages * PAGE // sub):
        tok = pl.ds(sc * sub, sub)
        cpg = cbuf[slot, tok, :].astype(BF16)
        krp = kbuf[slot, :, tok].astype(BF16)
        s = _dot_nt(ql, cpg) + _dot(qr, krp)
        m = jnp.max(s, axis=-1, keepdims=True)
        pr = jnp.exp(s - m)
        parts.append((m, jnp.sum(pr, axis=-1, keepdims=True), _dot(pr.astype(BF16), cpg)))
    m_all = s_new
    for m, _, _ in parts:
        m_all = jnp.maximum(m_all, m)
    w_new = jnp.exp(s_new - m_all)
    den = w_new
    acc = w_new * cn
    for m, l, o in parts:
        w = jnp.exp(m - m_all)
        den = den + w * l
        acc = acc + w * o
    o_ref[0] = (acc / den).astype(BF16)


def _mla_sample_attn(q, c_new, kr_new, cache_ckv, cache_krope, page_table, wukt, wuv_blk, sub):
    b = q.shape[0]
    npages = page_table.shape[1]
    qlat = pl.pallas_call(
        _qlat_kernel,
        out_shape=jax.ShapeDtypeStruct((MLA_HEADS, b, KV_LORA), BF16),
        name="mla_qlat",
    )(q, wukt)
    ql = qlat.transpose(1, 0, 2)
    qr = q.reshape(b, MLA_HEADS, LANES)[:, :, MLA_NOPE:MLA_QK]
    krt = jnp.swapaxes(cache_krope, 2, 3)
    n = npages * PAGE
    grid_spec = pltpu.PrefetchScalarGridSpec(
        num_scalar_prefetch=1,
        grid=(b,),
        in_specs=[pl.BlockSpec((1, MLA_HEADS, KV_LORA), lambda i, pt: (i, 0, 0)),
                  pl.BlockSpec((1, MLA_HEADS, MLA_ROPE), lambda i, pt: (i, 0, 0)),
                  pl.BlockSpec((1, 1, KV_LORA), lambda i, pt: (i, 0, 0)),
                  pl.BlockSpec((1, 1, MLA_ROPE), lambda i, pt: (i, 0, 0)),
                  pl.BlockSpec(memory_space=pl.ANY),
                  pl.BlockSpec(memory_space=pl.ANY)],
        out_specs=pl.BlockSpec((1, MLA_HEADS, KV_LORA), lambda i, pt: (i, 0, 0)),
        scratch_shapes=[pltpu.VMEM((2, n, KV_LORA), F32), pltpu.VMEM((2, MLA_ROPE, n), F32),
                        pltpu.SemaphoreType.DMA((2, 2))],
    )
    o_lat = pl.pallas_call(
        functools.partial(_mla_sample_kernel, npages=npages, sub=sub),
        grid_spec=grid_spec,
        out_shape=jax.ShapeDtypeStruct((b, MLA_HEADS, KV_LORA), BF16),
        compiler_params=_cparams(("arbitrary",)),
        name="mla_sample_attn",
    )(page_table, ql, qr, c_new.reshape(b, 1, KV_LORA), kr_new.reshape(b, 1, MLA_ROPE), cache_ckv, krt)
    return pl.pallas_call(
        _uv_kernel,
        out_shape=jax.ShapeDtypeStruct((b, MLA_HEADS * MLA_V), BF16),
        name="mla_sample_uv",
    )(o_lat.reshape(b, MLA_HEADS * KV_LORA), wuv_blk)


def _router_gates(h2, wr_ref, br_ref):
    h_hi = h2.astype(BF16)
    h_lo = (h2 - h_hi.astype(F32)).astype(BF16)
    logits = _dot(h_hi, wr_ref[0]) + (_dot(h_lo, wr_ref[0]) + _dot(h_hi, wr_ref[1]))
    scores = jax.nn.sigmoid(logits)
    sel = scores + br_ref[...]
    lane = lax.broadcasted_iota(jnp.int32, (1, LANES), 1)
    pos = lane % GROUP_SIZE
    others, wrapped = [], []
    for r in range(1, GROUP_SIZE):
        wrap = pos + r >= GROUP_SIZE
        fwd = pltpu.roll(sel, LANES - r, axis=1)
        bwd = pltpu.roll(sel, GROUP_SIZE - r, axis=1)
        others.append(jnp.where(wrap, bwd, fwd))
        wrapped.append(wrap)
    a, b, c, d = sel, others[0], others[1], others[2]
    hi1, lo1 = jnp.maximum(a, b), jnp.minimum(a, b)
    hi2, lo2 = jnp.maximum(c, d), jnp.minimum(c, d)
    gscore = jnp.maximum(hi1, hi2) + jnp.maximum(jnp.minimum(hi1, hi2), jnp.maximum(lo1, lo2))
    real = lane < N_EXPERTS
    gscore = jnp.where(real, gscore, -jnp.inf)
    gmax = jnp.max(gscore, axis=-1, keepdims=True)
    gidx = (lane // GROUP_SIZE).astype(F32)
    chosen = jnp.min(jnp.where(gscore == gmax, gidx, float(LANES)), axis=-1, keepdims=True)
    rank = jnp.zeros(sel.shape, F32)
    for o, wrap in zip(others, wrapped):
        rank = rank + jnp.where(wrap, jnp.where(o >= sel, 1.0, 0.0), jnp.where(o > sel, 1.0, 0.0))
    w = jnp.where(gidx == chosen, jnp.where(rank < 2.0, scores, 0.0), 0.0)
    return w / jnp.sum(w, axis=-1, keepdims=True)


def _post_kernel(*refs, n_mix):
    x_ref = refs[0]
    ys = refs[1:1 + n_mix]
    ws = refs[1 + n_mix:1 + 2 * n_mix]
    gt_ref, sh_ref, sc_ref, g_ref, wr_ref, br_ref, x1_ref, h2_ref, gate_ref = refs[1 + 2 * n_mix:]
    mix = None
    for y_ref, w_ref in zip(ys, ws):
        t = _dot(y_ref[...], w_ref[...])
        mix = t if mix is None else mix + t
    x1 = x_ref[...] + gt_ref[...] * mix
    x1_ref[...] = x1
    h2 = _rms(x1, g_ref[...]) * (1.0 + sc_ref[...]) + sh_ref[...]
    h2_ref[...] = h2.astype(BF16)
    gate_ref[...] = _router_gates(h2, wr_ref, br_ref)


def _post(x, ys, ws, gt, sh, sc, g, wr, br, tile):
    r = x.shape[0]
    per_row = gt.shape[0] != 1
    n_mix = len(ys)
    in_specs = ([pl.BlockSpec((tile, D_MODEL), lambda i: (i, 0))]
                + [pl.BlockSpec((tile, y.shape[1]), lambda i: (i, 0)) for y in ys]
                + [_full(w.shape) for w in ws]
                + [_rows(tile, D_MODEL, per_row)] * 3
                + [_full(g.shape), _full(wr.shape), _full(br.shape)])
    widths = [(D_MODEL, F32), (D_MODEL, BF16), (LANES, F32)]
    return pl.pallas_call(
        functools.partial(_post_kernel, n_mix=n_mix),
        grid=(r // tile,),
        in_specs=in_specs,
        out_specs=[pl.BlockSpec((tile, w), lambda i: (i, 0)) for w, _ in widths],
        out_shape=[jax.ShapeDtypeStruct((r, w), dt) for w, dt in widths],
        compiler_params=_cparams(("arbitrary",)),
        name="post_mixer",
    )(x, *ys, *ws, gt, sh, sc, g, wr, br)


def _moe_kernel(h_ref, gate_ref, x1_ref, gt_ref, wg_ref, wu_ref, wd_ref, o_ref, acc_ref):
    e = pl.program_id(1)

    @pl.when(e == 0)
    def _():
        acc_ref[...] = jnp.zeros_like(acc_ref)

    h = h_ref[...]
    a = _dot(h, wg_ref[0, 0])
    u = _dot(h, wu_ref[0, 0])
    lane = lax.broadcasted_iota(jnp.int32, (1, LANES), 1)
    gcol = jnp.sum(jnp.where(lane == e, gate_ref[...], 0.0), axis=-1, keepdims=True)
    act = (a * jax.nn.sigmoid(a)) * u * gcol
    acc_ref[...] += _dot(act.astype(BF16), wd_ref[0, 0])

    @pl.when(e == pl.num_programs(1) - 1)
    def _():
        o_ref[...] = x1_ref[...] + gt_ref[...] * acc_ref[...]


def _moe(h2, gate, x1, gt, wg, wu, wd, layer, tile):
    r = h2.shape[0]
    per_row = gt.shape[0] != 1
    gt_spec = (pl.BlockSpec((tile, D_MODEL), lambda i, e: (i, 0)) if per_row
               else pl.BlockSpec((1, D_MODEL), lambda i, e: (0, 0)))
    return pl.pallas_call(
        _moe_kernel,
        grid=(r // tile, N_EXPERTS),
        in_specs=[pl.BlockSpec((tile, D_MODEL), lambda i, e: (i, 0)),
                  pl.BlockSpec((tile, LANES), lambda i, e: (i, 0)),
                  pl.BlockSpec((tile, D_MODEL), lambda i, e: (i, 0)),
                  gt_spec,
                  pl.BlockSpec((1, 1, D_MODEL, D_FF), lambda i, e: (layer, e, 0, 0)),
                  pl.BlockSpec((1, 1, D_MODEL, D_FF), lambda i, e: (layer, e, 0, 0)),
                  pl.BlockSpec((1, 1, D_FF, D_MODEL), lambda i, e: (layer, e, 0, 0))],
        out_specs=pl.BlockSpec((tile, D_MODEL), lambda i, e: (i, 0)),
        out_shape=jax.ShapeDtypeStruct((r, D_MODEL), F32),
        scratch_shapes=[pltpu.VMEM((tile, D_MODEL), F32)],
        compiler_params=_cparams(("arbitrary", "arbitrary")),
        name="moe",
    )(h2, gate, x1, gt, wg, wu, wd)


def _front_odd_kernel(x_ref, sh_ref, sc_ref, g_ref, win_ref, gq_ref, gk_ref, rc_ref, rs1_ref, rs2_ref,
                      q_ref, k_ref, v_ref, ku_ref, vu_ref):
    x = x_ref[...]
    h = _rms(x, g_ref[...]) * (1.0 + sc_ref[...]) + sh_ref[...]
    z = _dot(h.astype(BF16), win_ref[...])
    rc, rs1, rs2 = rc_ref[...], rs1_ref[...], rs2_ref[...]
    scale = SWA_HD ** -0.5
    for hd in range(SWA_HEADS):
        sl = slice(hd * LANES, (hd + 1) * LANES)
        qn = _rope(_rms(z[:, sl], gq_ref[...], SWA_HD), SWA_ROT // 2, rc, rs1, rs2)
        q_ref[:, sl] = (qn * scale).astype(BF16)
    ks, vs = [], []
    for kh in range(SWA_KV):
        sl = slice(kh * LANES, (kh + 1) * LANES)
        zk = z[:, (SWA_HEADS + kh) * LANES:(SWA_HEADS + kh + 1) * LANES]
        kn = _rope(_rms(zk, gk_ref[...], SWA_HD), SWA_ROT // 2, rc, rs1, rs2)
        vv = z[:, (SWA_HEADS + SWA_KV + kh) * LANES:(SWA_HEADS + SWA_KV + kh + 1) * LANES]
        k_ref[:, sl] = kn.astype(BF16)
        v_ref[:, sl] = vv.astype(BF16)
        ks.append(kn)
        vs.append(vv)
    for j in range(SWA_KV // 2):
        sl = slice(j * LANES, (j + 1) * LANES)
        ku_ref[:, sl] = ks[2 * j] + pltpu.roll(ks[2 * j + 1], SWA_HD, axis=1)
        vu_ref[:, sl] = vs[2 * j] + pltpu.roll(vs[2 * j + 1], SWA_HD, axis=1)


def _front_odd(x, sh, sc, g, win, gq, gk, tabs, tile):
    r = x.shape[0]
    per_row_mod = sh.shape[0] != 1
    per_row_tab = tabs[0].shape[0] != 1
    in_specs = [pl.BlockSpec((tile, D_MODEL), lambda i: (i, 0)),
                _rows(tile, D_MODEL, per_row_mod), _rows(tile, D_MODEL, per_row_mod),
                _full(g.shape), _full(win.shape), _full(gq.shape), _full(gk.shape),
                _rows(tile, LANES, per_row_tab), _rows(tile, LANES, per_row_tab), _rows(tile, LANES, per_row_tab)]
    widths = [(SWA_HEADS * LANES, BF16), (SWA_KV * LANES, BF16), (SWA_KV * LANES, BF16),
              (SWA_KV * SWA_HD, F32), (SWA_KV * SWA_HD, F32)]
    return pl.pallas_call(
        _front_odd_kernel,
        grid=(r // tile,),
        in_specs=in_specs,
        out_specs=[pl.BlockSpec((tile, w), lambda i: (i, 0)) for w, _ in widths],
        out_shape=[jax.ShapeDtypeStruct((r, w), dt) for w, dt in widths],
        compiler_params=_cparams(("arbitrary",)),
        name="front_odd",
    )(x, sh, sc, g, win, gq, gk, *tabs)


def _swa_prompt_kernel(sink_ref, q_ref, kp_ref, kc_ref, vp_ref, vc_ref, o_ref):
    n = pl.program_id(0)
    w = WINDOW
    qi = lax.broadcasted_iota(jnp.int32, (w, 2 * w), 0) + w
    kj = lax.broadcasted_iota(jnp.int32, (w, 2 * w), 1)
    valid = (kj <= qi) & (qi - kj < w) & (n * w - w + kj >= 0)
    for kh in range(SWA_KV):
        sl = slice(kh * LANES, (kh + 1) * LANES)
        kk = jnp.concatenate([kp_ref[:, sl], kc_ref[:, sl]], axis=0)
        vv = jnp.concatenate([vp_ref[:, sl], vc_ref[:, sl]], axis=0)
        for gi in range(SWA_GROUP):
            hd = kh * SWA_GROUP + gi
            hsl = slice(hd * LANES, (hd + 1) * LANES)
            s = jnp.where(valid, _dot_nt(q_ref[:, hsl], kk), -jnp.inf)
            sink = sink_ref[hd]
            m = jnp.maximum(jnp.max(s, axis=-1, keepdims=True), sink)
            pr = jnp.exp(s - m)
            den = jnp.sum(pr, axis=-1, keepdims=True) + jnp.exp(sink - m)
            o_ref[:, hsl] = (_dot(pr.astype(BF16), vv) / den).astype(BF16)


def _swa_prompt_attn(q, k, v, sinks):
    t = q.shape[0]
    w = WINDOW
    prev = lambda n, s: (jnp.maximum(n - 1, 0), 0)
    cur = lambda n, s: (n, 0)
    grid_spec = pltpu.PrefetchScalarGridSpec(
        num_scalar_prefetch=1,
        grid=(t // w,),
        in_specs=[pl.BlockSpec((w, SWA_HEADS * LANES), cur),
                  pl.BlockSpec((w, SWA_KV * LANES), prev), pl.BlockSpec((w, SWA_KV * LANES), cur),
                  pl.BlockSpec((w, SWA_KV * LANES), prev), pl.BlockSpec((w, SWA_KV * LANES), cur)],
        out_specs=pl.BlockSpec((w, SWA_HEADS * LANES), cur),
    )
    return pl.pallas_call(
        _swa_prompt_kernel,
        grid_spec=grid_spec,
        out_shape=jax.ShapeDtypeStruct((t, SWA_HEADS * LANES), BF16),
        compiler_params=_cparams(("arbitrary",)),
        name="swa_prompt_attn",
    )(sinks, q, k, k, v, v)


def _swa_sample_kernel(q_ref, kc_ref, vc_ref, kn_ref, vn_ref, sink_ref, o_ref, wk_ref, wv_ref, *, sb):
    w = WINDOW
    width = SWA_KV * SWA_HD
    rowgrp = lax.broadcasted_iota(jnp.int32, (SWA_HEADS, 1), 0) // SWA_GROUP
    col = lax.broadcasted_iota(jnp.int32, (1, w), 1)
    low = lax.broadcasted_iota(jnp.int32, (1, LANES), 1) < SWA_HD
    sink = sink_ref[...]
    for b in range(sb):
        qf = q_ref[b].astype(F32)
        a = jnp.concatenate([qf, jnp.zeros_like(qf)], axis=1)
        qx = a
        for kh in range(1, SWA_KV):
            qx = jnp.where(rowgrp == kh, pltpu.roll(a, kh * SWA_HD, axis=1), qx)
        kc = kc_ref[b]
        vc = vc_ref[b]
        kn = kn_ref[b]
        vn = vn_ref[b]
        s = _dot_nt(qx.astype(BF16), kc.astype(BF16))
        s = jnp.where(col >= 1, s, -jnp.inf)
        s_new = jnp.sum(qx.astype(BF16).astype(F32) * kn.astype(BF16).astype(F32), axis=-1, keepdims=True)
        m = jnp.maximum(jnp.maximum(jnp.max(s, axis=-1, keepdims=True), s_new), sink)
        pr = jnp.exp(s - m)
        pn = jnp.exp(s_new - m)
        den = jnp.sum(pr, axis=-1, keepdims=True) + pn + jnp.exp(sink - m)
        o = (_dot(pr.astype(BF16), vc.astype(BF16)) + pn * vn.astype(BF16).astype(F32)) / den
        ox = o
        for kh in range(1, SWA_KV):
            ox = jnp.where(rowgrp == kh, pltpu.roll(o, width - kh * SWA_HD, axis=1), ox)
        o_ref[b] = jnp.where(low, ox[:, :LANES], 0.0).astype(BF16)
        wk_ref[b, pl.ds(0, w - 1), :] = kc_ref[b, pl.ds(1, w - 1), :]
        wk_ref[b, pl.ds(w - 1, 1), :] = kn
        wv_ref[b, pl.ds(0, w - 1), :] = vc_ref[b, pl.ds(1, w - 1), :]
        wv_ref[b, pl.ds(w - 1, 1), :] = vn


def _swa_sample_attn(q, kc, vc, kn, vn, sinks, sb):
    b = q.shape[0]
    w = WINDOW
    width = SWA_KV * SWA_HD
    blk3 = lambda s1, s2: pl.BlockSpec((sb, s1, s2), lambda i: (i, 0, 0))
    return pl.pallas_call(
        functools.partial(_swa_sample_kernel, sb=sb),
        grid=(b // sb,),
        in_specs=[blk3(SWA_HEADS, LANES), blk3(w, width), blk3(w, width), blk3(1, width), blk3(1, width),
                  _full((SWA_HEADS, 1))],
        out_specs=[blk3(SWA_HEADS, LANES), blk3(w, width), blk3(w, width)],
        out_shape=[jax.ShapeDtypeStruct((b, SWA_HEADS, LANES), BF16),
                   jax.ShapeDtypeStruct((b, w, width), F32), jax.ShapeDtypeStruct((b, w, width), F32)],
        compiler_params=_cparams(("arbitrary",)),
        name="swa_sample_attn",
    )(q.reshape(b, SWA_HEADS, LANES), kc, vc, kn.reshape(b, 1, width), vn.reshape(b, 1, width),
      sinks.reshape(SWA_HEADS, 1))


def _rope_cos_sin(pos):
    half = MLA_ROPE // 2
    inv = ROPE_THETA ** (-jnp.arange(half, dtype=F32) / half)
    ang = pos.astype(F32)[:, None] * inv[None, :]
    return jnp.cos(ang), jnp.sin(ang)


def _rope_tables(cos, sin, offset):
    n, half = cos.shape
    rest = LANES - offset - 2 * half
    ones = lambda w: jnp.ones((n, w), F32)
    zeros = lambda w: jnp.zeros((n, w), F32)
    rc = jnp.concatenate([ones(offset), cos, cos, ones(rest)], axis=1)
    rs1 = jnp.concatenate([zeros(offset), -sin, zeros(half + rest)], axis=1)
    rs2 = jnp.concatenate([zeros(offset + half), sin, zeros(rest)], axis=1)
    return rc, rs1, rs2


def _pad_lanes(v, offset=0):
    return jnp.zeros((1, LANES), F32).at[0, offset:offset + v.shape[0]].set(v)


def _even_weights(w_in, g_qa, w_uq, g_qh, g_kva, g_kr, w_uk, w_uv):
    o3 = S5_WIDTH + Q_LORA + KV_LORA
    win = jnp.zeros((D_MODEL, 1280), F32).at[:, :o3].set(w_in[:, :o3])
    win = win.at[:, o3 + MLA_NOPE:o3 + MLA_QK].set(w_in[:, o3:])
    wuq = jnp.zeros((MLA_HEADS, Q_LORA, LANES), F32).at[:, :, :MLA_QK].set(w_uq.transpose(1, 0, 2))
    wuk = jnp.zeros((KV_LORA, MLA_HEADS, LANES), F32).at[:, :, :MLA_NOPE].set(w_uk)
    front = (win.astype(BF16), g_qa.reshape(1, Q_LORA), wuq.astype(BF16), _pad_lanes(g_qh),
             g_kva.reshape(1, KV_LORA), _pad_lanes(g_kr, MLA_NOPE),
             wuk.reshape(KV_LORA, MLA_HEADS * LANES).astype(BF16),
             w_uv.reshape(KV_LORA, MLA_HEADS * MLA_V).astype(BF16))
    wukt = jnp.zeros((MLA_HEADS, LANES, KV_LORA), F32).at[:, :MLA_NOPE, :].set(w_uk.transpose(1, 2, 0))
    eye = jnp.eye(MLA_HEADS, dtype=F32)
    wuv_blk = jnp.einsum('chd,hg->hcgd', w_uv, eye).reshape(MLA_HEADS * KV_LORA, MLA_HEADS * MLA_V)
    return front, wukt.astype(BF16), wuv_blk.astype(BF16)


def _odd_weights(w_in, w_out):
    nq, nk = SWA_HEADS * SWA_HD, SWA_KV * SWA_HD
    nh = SWA_HEADS + 2 * SWA_KV
    win = jnp.zeros((D_MODEL, nh, LANES), F32).at[:, :, :SWA_HD].set(w_in.reshape(D_MODEL, nh, SWA_HD))
    wout = jnp.zeros((SWA_HEADS, LANES, D_MODEL), F32).at[:, :SWA_HD, :].set(w_out.reshape(SWA_HEADS, SWA_HD, D_MODEL))
    del nq, nk
    return win.reshape(D_MODEL, nh * LANES).astype(BF16), wout.reshape(SWA_HEADS * LANES, D_MODEL).astype(BF16)


def kernel(x_prompt, x_sample, c_prompt, c_sample, cache_ckv, cache_krope, page_table, state_s5_re, state_s5_im, cache_win_k, cache_win_v, w_mod, b_mod, g_norm_mix, g_norm_ffn, w_in_even, w_out_even, s5_lam_re, s5_lam_im, s5_log_step, s5_b_re, s5_b_im, s5_c_re, s5_c_im, s5_d, s5_w_glu, mla_g_qa, mla_w_uq, mla_g_qh, mla_g_kva, mla_g_kr, mla_w_uk, mla_w_uv, w_in_odd, w_out_odd, swa_g_q, swa_g_k, swa_sinks, w_router, b_router, moe_w_gate, moe_w_up, moe_w_down):
    t = x_prompt.shape[1]
    nb = x_sample.shape[0]
    past_len = page_table.shape[1] * PAGE
    row_tile = 512
    s5_seg = 64
    attn_tile, attn_heads = 1024, MLA_HEADS
    mla_scale = MLA_QK ** -0.5
    moe_tile = 1024
    sample_sub = past_len
    swa_seq_block = 8

    n_c = 1 + nb
    rp = -(-n_c // SUBLANES) * SUBLANES
    c_all = jnp.concatenate([c_prompt, c_sample, jnp.zeros((rp - n_c, D_MODEL), F32)], axis=0)
    mod = _modulation(c_all, w_mod, b_mod)

    def mods(layer, sample):
        rows = mod[layer, 1:1 + nb] if sample else mod[layer, 0:1]
        return [rows[:, k * D_MODEL:(k + 1) * D_MODEL] for k in range(6)]

    wr = jnp.zeros((D_MODEL, LANES), F32).at[:, :N_EXPERTS].set(w_router)
    wr_hi = wr.astype(BF16)
    wr = jnp.stack([wr_hi, (wr - wr_hi.astype(F32)).astype(BF16)])
    br = _pad_lanes(b_router)
    wg, wu, wd = moe_w_gate.astype(BF16), moe_w_up.astype(BF16), moe_w_down.astype(BF16)

    xp = x_prompt.reshape(t, D_MODEL)
    xs = x_sample.reshape(nb, D_MODEL)
    cos_p, sin_p = _rope_cos_sin(jnp.arange(t))
    cos_s, sin_s = _rope_cos_sin(jnp.full((1,), past_len))
    mla_tabs_p = _rope_tables(cos_p, sin_p, MLA_NOPE)
    mla_tabs_s = _rope_tables(cos_s, sin_s, MLA_NOPE)
    swa_tabs_p = _rope_tables(cos_p[:, ::2], sin_p[:, ::2], 0)
    swa_tabs_s = _rope_tables(cos_s[:, ::2], sin_s[:, ::2], 0)

    front_w, wukt, wuv_blk = _even_weights(w_in_even[0], mla_g_qa[0], mla_w_uq[0], mla_g_qh[0], mla_g_kva[0],
                                           mla_g_kr[0], mla_w_uk[0], mla_w_uv[0])
    a_re, a_im, bb_re, bb_im, al_re, al_im = _s5_discretize(s5_lam_re[0], s5_lam_im[0], s5_log_step[0],
                                                            s5_b_re[0], s5_b_im[0], s5_seg)
    bre, bim, cre, cim = _s5_layouts(bb_re, bb_im, s5_c_re[0], s5_c_im[0])
    d_skip = s5_d[0].reshape(1, S5_WIDTH)
    wglu = s5_w_glu[0].astype(BF16)
    wo_ssm = w_out_even[0, :S5_WIDTH].astype(BF16)
    wo_att = w_out_even[0, S5_WIDTH:].astype(BF16)
    g_mix0 = g_norm_mix[0].reshape(1, D_MODEL)
    g_ffn0 = g_norm_ffn[0].reshape(1, D_MODEL)

    sh1, sc1, gt1, sh2, sc2, gt2 = mods(0, False)
    u, q, k, v, ckv_p, kr_p = _front_even(xp, sh1, sc1, g_mix0, front_w, mla_tabs_p,
                                          row_tile, mla_scale * math.log2(math.e))
    y_ssm, st = _s5_scan(u, bre, bim, a_re, a_im, al_re, al_im, cre, cim, d_skip, wglu, s5_seg)
    sr_p = st[:S5_CHUNKS].reshape(1, 1, S5_GROUPS, S5_STATE)
    si_p = st[S5_CHUNKS:].reshape(1, 1, S5_GROUPS, S5_STATE)
    y_att = _mla_prompt_attn(q, k, v, attn_tile, attn_heads)
    x1, h2, gate = _post(xp, [y_ssm, y_att], [wo_ssm, wo_att], gt1, sh2, sc2, g_ffn0, wr, br, row_tile)
    xp = _moe(h2, gate, x1, gt2, wg, wu, wd, 0, moe_tile)

    sh1, sc1, gt1, sh2, sc2, gt2 = mods(0, True)
    u, q, _, _, ckv_s, kr_s = _front_even(xs, sh1, sc1, g_mix0, front_w, mla_tabs_s, nb, mla_scale)
    y_ssm, sr_s, si_s = _s5_step(u, state_s5_re[0].reshape(nb, S5_LANES), state_s5_im[0].reshape(nb, S5_LANES),
                                 bre, bim, a_re, a_im, cre, cim, d_skip, wglu)
    y_att = _mla_sample_attn(q, ckv_s, kr_s, cache_ckv, cache_krope, page_table, wukt, wuv_blk, sample_sub)
    x1, h2, gate = _post(xs, [y_ssm, y_att], [wo_ssm, wo_att], gt1, sh2, sc2, g_ffn0, wr, br, nb)
    xs = _moe(h2, gate, x1, gt2, wg, wu, wd, 0, nb)

    win_odd, wout_odd = _odd_weights(w_in_odd[0], w_out_odd[0])
    gq, gk = _pad_lanes(swa_g_q[0]), _pad_lanes(swa_g_k[0])
    g_mix1 = g_norm_mix[1].reshape(1, D_MODEL)
    g_ffn1 = g_norm_ffn[1].reshape(1, D_MODEL)
    sinks = swa_sinks[0]

    sh1, sc1, gt1, sh2, sc2, gt2 = mods(1, False)
    q, k, v, ku, vu = _front_odd(xp, sh1, sc1, g_mix1, win_odd, gq, gk, swa_tabs_p, row_tile)
    o = _swa_prompt_attn(q, k, v, sinks)
    wk_p = ku[t - WINDOW:].reshape(1, 1, WINDOW, SWA_KV, SWA_HD)
    wv_p = vu[t - WINDOW:].reshape(1, 1, WINDOW, SWA_KV, SWA_HD)
    x1, h2, gate = _post(xp, [o], [wout_odd], gt1, sh2, sc2, g_ffn1, wr, br, row_tile)
    xp = _moe(h2, gate, x1, gt2, wg, wu, wd, 1, moe_tile)

    sh1, sc1, gt1, sh2, sc2, gt2 = mods(1, True)
    q, _, _, ku, vu = _front_odd(xs, sh1, sc1, g_mix1, win_odd, gq, gk, swa_tabs_s, nb)
    width = SWA_KV * SWA_HD
    o, wk_s, wv_s = _swa_sample_attn(q, cache_win_k[0].reshape(nb, WINDOW, width),
                                     cache_win_v[0].reshape(nb, WINDOW, width), ku, vu, sinks, swa_seq_block)
    x1, h2, gate = _post(xs, [o.reshape(nb, SWA_HEADS * LANES)], [wout_odd], gt1, sh2, sc2, g_ffn1, wr, br, nb)
    xs = _moe(h2, gate, x1, gt2, wg, wu, wd, 1, nb)

    return (xp.reshape(1, t, D_MODEL), xs.reshape(nb, 1, D_MODEL),
            ckv_p.reshape(1, 1, t, KV_LORA), kr_p.reshape(1, 1, t, MLA_ROPE), sr_p, si_p, wk_p, wv_p,
            ckv_s.reshape(1, nb, 1, KV_LORA), kr_s.reshape(1, nb, 1, MLA_ROPE),
            sr_s.reshape(1, nb, S5_GROUPS, S5_STATE), si_s.reshape(1, nb, S5_GROUPS, S5_STATE),
            wk_s.reshape(1, nb, WINDOW, SWA_KV, SWA_HD), wv_s.reshape(1, nb, WINDOW, SWA_KV, SWA_HD))
```

```python
import functools
import math

import jax
import jax.numpy as jnp
from jax import lax
from jax.experimental import pallas as pl
from jax.experimental.pallas import tpu as pltpu

F32 = jnp.float32
BF16 = jnp.bfloat16
EPS = 1e-6
ROPE_THETA = 500000.0
LANES = 128
SUBLANES = 8
VMEM_LIMIT = 56 * 1024 * 1024

D_MODEL = 1024
PAGE = 128
S5_WIDTH = 512
S5_GROUP = 16
S5_GROUPS = 32
S5_STATE = 64
S5_LANES = S5_GROUPS * S5_STATE
S5_CHUNKS = S5_LANES // LANES
MLA_HEADS = 8
MLA_NOPE = 64
MLA_ROPE = 32
MLA_QK = 96
MLA_V = 64
Q_LORA = 384
KV_LORA = 256
SWA_HEADS = 16
SWA_KV = 4
SWA_GROUP = 4
SWA_HD = 64
SWA_ROT = 16
WINDOW = 128
N_EXPERTS = 16
GROUP_SIZE = 4
D_FF = 256

_NT = (((1,), (1,)), ((), ()))


def _dot(a, b):
    return jnp.dot(a, b, preferred_element_type=F32)


def _dot_nt(a, b):
    return lax.dot_general(a, b, _NT, preferred_element_type=F32)


def _rms(x, g, n=None):
    n = x.shape[-1] if n is None else n
    ss = jnp.sum(x * x, axis=-1, keepdims=True) * (1.0 / n)
    return x * lax.rsqrt(ss + EPS) * g


def _rope(x, rot, rc, rs, exact):
    hi = x.astype(BF16)
    partner = _dot(hi, rot)
    if exact:
        partner = partner + _dot((x - hi.astype(F32)).astype(BF16), rot)
    return x * rc + partner * rs


def _cparams(sem=None, vmem=VMEM_LIMIT):
    return pltpu.CompilerParams(dimension_semantics=sem, vmem_limit_bytes=vmem)


def _full(shape):
    nd = len(shape)
    return pl.BlockSpec(shape, lambda *_: (0,) * nd)


def _rows(tile, width, per_row):
    if per_row:
        return pl.BlockSpec((tile, width), lambda i: (i, 0))
    return pl.BlockSpec((1, width), lambda i: (0, 0))


def _mod_kernel(c_ref, w_ref, b_ref, o_ref):
    c = c_ref[...]
    s = (c * jax.nn.sigmoid(c)).astype(BF16)
    o_ref[0] = _dot(s, w_ref[0].astype(BF16)) + b_ref[0]


def _modulation(c_all, w_mod, b_mod):
    depth, _, n = w_mod.shape
    rp = c_all.shape[0]
    tn = 1536
    return pl.pallas_call(
        _mod_kernel,
        grid=(depth, n // tn),
        in_specs=[pl.BlockSpec((rp, D_MODEL), lambda l, j: (0, 0)),
                  pl.BlockSpec((1, D_MODEL, tn), lambda l, j: (l, 0, j)),
                  pl.BlockSpec((1, 1, tn), lambda l, j: (l, 0, j))],
        out_specs=pl.BlockSpec((1, rp, tn), lambda l, j: (l, 0, j)),
        out_shape=jax.ShapeDtypeStruct((depth, rp, n), F32),
        compiler_params=_cparams(("arbitrary", "arbitrary")),
        name="modulation",
    )(c_all, w_mod, b_mod.reshape(depth, 1, n))


def _front_even_kernel(x_ref, sh_ref, sc_ref, g_ref, win_ref, gqa_ref, wuq_ref, gqh_ref, gkva_ref,
                       gkr_ref, wuk_ref, wuv_ref, rc_ref, rs_ref, rot_ref,
                       u_ref, q_ref, k_ref, v_ref, ckv_ref, kr_ref, *, q_scale):
    x = x_ref[...]
    h = _rms(x, g_ref[...]) * (1.0 + sc_ref[...]) + sh_ref[...]
    z = _dot(h.astype(BF16), win_ref[...])
    u_ref[...] = z[:, :512]
    c_q = _rms(z[:, 512:896], gqa_ref[...]).astype(BF16)
    c_kv = _rms(z[:, 896:1152], gkva_ref[...])
    ckv_ref[...] = c_kv
    rc, rs, rot = rc_ref[...], rs_ref[...], rot_ref[...]
    kr = _rope(_rms(z[:, 1152:1280], gkr_ref[...], MLA_ROPE), rot, rc, rs, True)
    kr_ref[...] = kr[:, MLA_NOPE:MLA_QK]
    ckvb = c_kv.astype(BF16)
    kn = _dot(ckvb, wuk_ref[...])
    v_ref[...] = _dot(ckvb, wuv_ref[...]).astype(BF16)
    for hd in range(MLA_HEADS):
        sl = slice(hd * LANES, (hd + 1) * LANES)
        qh = _dot(c_q, wuq_ref[hd])
        qn = _rope(_rms(qh, gqh_ref[...], MLA_QK), rot, rc, rs, False)
        q_ref[:, sl] = (qn * q_scale).astype(BF16)
        k_ref[:, sl] = (kn[:, sl] + kr).astype(BF16)


def _front_even(x, sh, sc, g, wts, tabs, tile, q_scale):
    r = x.shape[0]
    per_row_mod = sh.shape[0] != 1
    per_row_tab = tabs[0].shape[0] != 1
    win, gqa, wuq, gqh, gkva, gkr, wuk, wuv = wts
    in_specs = [pl.BlockSpec((tile, D_MODEL), lambda i: (i, 0)),
                _rows(tile, D_MODEL, per_row_mod), _rows(tile, D_MODEL, per_row_mod),
                _full(g.shape), _full(win.shape), _full(gqa.shape), _full(wuq.shape), _full(gqh.shape),
                _full(gkva.shape), _full(gkr.shape), _full(wuk.shape), _full(wuv.shape),
                _rows(tile, LANES, per_row_tab), _rows(tile, LANES, per_row_tab), _full(tabs[2].shape)]
    widths = [(S5_WIDTH, F32), (MLA_HEADS * LANES, BF16), (MLA_HEADS * LANES, BF16),
              (MLA_HEADS * MLA_V, BF16), (KV_LORA, F32), (MLA_ROPE, F32)]
    return pl.pallas_call(
        functools.partial(_front_even_kernel, q_scale=q_scale),
        grid=(r // tile,),
        in_specs=in_specs,
        out_specs=[pl.BlockSpec((tile, w), lambda i: (i, 0)) for w, _ in widths],
        out_shape=[jax.ShapeDtypeStruct((r, w), dt) for w, dt in widths],
        compiler_params=_cparams(("arbitrary",)),
        name="front_even",
    )(x, sh, sc, g, win, gqa, wuq, gqh, gkva, gkr, wuk, wuv, *tabs)


def _s5_disc_kernel(lr_ref, li_ref, ls_ref, br_ref, bi_ref, are_ref, aim_ref, bbr_ref, bbi_ref,
                    alr_ref, ali_ref, *, log2_len):
    lr, li = lr_ref[...], li_ref[...]
    dt = jnp.exp(ls_ref[...])
    mag = jnp.exp(lr * dt)
    ang = li * dt
    a_re = mag * jnp.cos(ang)
    a_im = mag * jnp.sin(ang)
    den = lr * lr + li * li
    k_re = ((a_re - 1.0) * lr + a_im * li) / den
    k_im = (a_im * lr - (a_re - 1.0) * li) / den
    are_ref[...] = a_re
    aim_ref[...] = a_im
    for c in range(S5_GROUP):
        br, bi = br_ref[c], bi_ref[c]
        bbr_ref[c] = k_re * br - k_im * bi
        bbi_ref[c] = k_re * bi + k_im * br
    pr, pi = a_re, a_im
    for _ in range(log2_len):
        pr, pi = pr * pr - pi * pi, 2.0 * pr * pi
    alr_ref[...] = pr
    ali_ref[...] = pi


def _s5_discretize(lam_re, lam_im, log_step, b_re, b_im, seg_len):
    g, n = lam_re.shape
    outs = [jax.ShapeDtypeStruct((g, n), F32)] * 2 + [jax.ShapeDtypeStruct((S5_GROUP, g, n), F32)] * 2 \
        + [jax.ShapeDtypeStruct((g, n), F32)] * 2
    return pl.pallas_call(
        functools.partial(_s5_disc_kernel, log2_len=int(math.log2(seg_len))),
        out_shape=outs,
        name="s5_discretize",
    )(lam_re, lam_im, log_step.reshape(g, 1), b_re.transpose(2, 0, 1), b_im.transpose(2, 0, 1))


def _s5_layouts(bb_re, bb_im, c_re, c_im):
    eye8 = jnp.eye(8, dtype=F32)

    def bmat(bb):
        b4 = bb.reshape(S5_GROUP, 4, 8, S5_STATE)
        m = jnp.einsum('cjgn,gh->jgchn', b4, eye8)
        return m.reshape(4, LANES, 512).astype(BF16)

    sel = jax.nn.one_hot((2 * jnp.arange(S5_CHUNKS)[:, None] + jnp.arange(2)[None, :]) % 8, 8, dtype=F32)

    def cmat(c):
        c4 = c.reshape(S5_CHUNKS, 2, S5_GROUP, S5_STATE)
        m = jnp.einsum('asck,asg->askgc', c4, sel)
        return m.reshape(S5_CHUNKS, LANES, LANES).astype(BF16)

    return bmat(bb_re), bmat(bb_im), cmat(c_re), cmat(-c_im)


def _gelu_glu(y, wglu_ref):
    z = jax.nn.gelu(y)
    return z * jax.nn.sigmoid(_dot(z.astype(BF16), wglu_ref[...]))


def _s5_scan_kernel(u0_ref, u1_ref, u2_ref, u3_ref, perm_ref, bre_ref, bim_ref, are_ref, aim_ref, alr_ref,
                    ali_ref, cre_ref, cim_ref, d_ref, wglu_ref, y_ref, st_ref, bu_ref, hs_ref, carry_ref,
                    *, seg_len):
    i = pl.program_id(0)

    @pl.when(i == 0)
    def _():
        carry_ref[...] = jnp.zeros_like(carry_ref)

    us = []
    for j, u_ref in enumerate((u0_ref, u1_ref, u2_ref, u3_ref)):
        uj = jnp.concatenate([u_ref[pl.ds(t, SUBLANES, stride=seg_len), :] for t in range(seg_len)], axis=0)
        us.append(uj)
        ub = uj.astype(BF16)
        re = _dot(ub, bre_ref[j])
        im = _dot(ub, bim_ref[j])
        for q in range(4):
            bu_ref[4 * j + q] = re[:, q * LANES:(q + 1) * LANES]
            bu_ref[S5_CHUNKS + 4 * j + q] = im[:, q * LANES:(q + 1) * LANES]

    per = 4
    for grp in range(S5_CHUNKS // per):
        cs = [grp * per + q for q in range(per)]
        ar = [jnp.broadcast_to(are_ref[c], (SUBLANES, LANES)) for c in cs]
        ai = [jnp.broadcast_to(aim_ref[c], (SUBLANES, LANES)) for c in cs]

        def advance(t, hs, store):
            out = []
            for k, c in enumerate(cs):
                hr, hi = hs[2 * k], hs[2 * k + 1]
                rows = pl.ds(pl.multiple_of(t * SUBLANES, SUBLANES), SUBLANES)
                nr = ar[k] * hr - ai[k] * hi + bu_ref[c, rows, :]
                ni = ar[k] * hi + ai[k] * hr + bu_ref[S5_CHUNKS + c, rows, :]
                if store:
                    hs_ref[c, rows, :] = nr
                    hs_ref[S5_CHUNKS + c, rows, :] = ni
                out += [nr, ni]
            return tuple(out)

        zero = tuple(jnp.zeros((SUBLANES, LANES), F32) for _ in range(2 * per))
        ends = lax.fori_loop(0, seg_len, lambda t, hs: advance(t, hs, False), zero, unroll=2)
        init = []
        for k, c in enumerate(cs):
            er, ei = ends[2 * k], ends[2 * k + 1]
            lr, li = alr_ref[c], ali_ref[c]
            hr, hi = carry_ref[c], carry_ref[S5_CHUNKS + c]
            rows_r, rows_i = [], []
            for s in range(SUBLANES):
                rows_r.append(hr)
                rows_i.append(hi)
                hr, hi = (er[s:s + 1] + lr * hr - li * hi, ei[s:s + 1] + lr * hi + li * hr)
            carry_ref[c] = hr
            carry_ref[S5_CHUNKS + c] = hi
            init += [jnp.concatenate(rows_r, axis=0), jnp.concatenate(rows_i, axis=0)]
        lax.fori_loop(0, seg_len, lambda t, hs: advance(t, hs, True), tuple(init), unroll=2)

    ys = []
    for j in range(4):
        acc = None
        for q in range(4):
            c = 4 * j + q
            t = _dot(hs_ref[c].astype(BF16), cre_ref[c]) + _dot(hs_ref[S5_CHUNKS + c].astype(BF16), cim_ref[c])
            acc = t if acc is None else acc + t
        ys.append(acc)
    y = jnp.concatenate(ys, axis=1) + d_ref[...] * jnp.concatenate(us, axis=1)
    gated = _gelu_glu(y, wglu_ref).astype(BF16)
    y_ref[...] = _dot(perm_ref[...], gated).astype(BF16)

    @pl.when(i == pl.num_programs(0) - 1)
    def _():
        st_ref[...] = carry_ref[...]


def _s5_scan(u, bre, bim, a_re, a_im, al_re, al_im, cre, cim, d, wglu, seg_len):
    t = u.shape[0]
    tile = SUBLANES * seg_len
    ch = lambda a: a.reshape(S5_CHUNKS, 1, LANES)
    r = jnp.arange(tile)
    perm = jax.nn.one_hot(SUBLANES * (r % seg_len) + r // seg_len, tile, dtype=BF16)
    args = (u, u, u, u, perm, bre, bim, ch(a_re), ch(a_im), ch(al_re), ch(al_im), cre, cim, d, wglu)
    in_specs = ([pl.BlockSpec((tile, LANES), functools.partial(lambda i, j: (i, j), j=j)) for j in range(4)]
                + [_full(a.shape) for a in args[4:]])
    return pl.pallas_call(
        functools.partial(_s5_scan_kernel, seg_len=seg_len),
        grid=(t // tile,),
        in_specs=in_specs,
        out_specs=[pl.BlockSpec((tile, S5_WIDTH), lambda i: (i, 0)), _full((2 * S5_CHUNKS, 1, LANES))],
        out_shape=[jax.ShapeDtypeStruct((t, S5_WIDTH), BF16),
                   jax.ShapeDtypeStruct((2 * S5_CHUNKS, 1, LANES), F32)],
        scratch_shapes=[pltpu.VMEM((2 * S5_CHUNKS, tile, LANES), F32),
                        pltpu.VMEM((2 * S5_CHUNKS, tile, LANES), F32),
                        pltpu.VMEM((2 * S5_CHUNKS, 1, LANES), F32)],
        compiler_params=_cparams(("arbitrary",)),
        name="s5_scan",
    )(*args)


def _s5_step_kernel(u_ref, h0r_ref, h0i_ref, bre_ref, bim_ref, are_ref, aim_ref, cre_ref, cim_ref,
                    d_ref, wglu_ref, y_ref, sr_ref, si_ref):
    u = u_ref[...]
    ub = u.astype(BF16)
    ys = []
    for j in range(4):
        uj = ub[:, j * LANES:(j + 1) * LANES]
        re = _dot(uj, bre_ref[j])
        im = _dot(uj, bim_ref[j])
        acc = None
        for q in range(4):
            sl = slice((4 * j + q) * LANES, (4 * j + q + 1) * LANES)
            ar, ai = are_ref[:, sl], aim_ref[:, sl]
            h0r, h0i = h0r_ref[:, sl], h0i_ref[:, sl]
            hr = ar * h0r - ai * h0i + re[:, q * LANES:(q + 1) * LANES]
            hi = ar * h0i + ai * h0r + im[:, q * LANES:(q + 1) * LANES]
            sr_ref[:, sl] = hr
            si_ref[:, sl] = hi
            t = _dot(hr.astype(BF16), cre_ref[4 * j + q]) + _dot(hi.astype(BF16), cim_ref[4 * j + q])
            acc = t if acc is None else acc + t
        ys.append(acc)
    y = jnp.concatenate(ys, axis=1) + d_ref[...] * u
    y_ref[...] = _gelu_glu(y, wglu_ref).astype(BF16)


def _s5_step(u, h0r, h0i, bre, bim, a_re, a_im, cre, cim, d, wglu):
    b = u.shape[0]
    return pl.pallas_call(
        _s5_step_kernel,
        out_shape=[jax.ShapeDtypeStruct((b, S5_WIDTH), BF16), jax.ShapeDtypeStruct((b, S5_LANES), F32),
                   jax.ShapeDtypeStruct((b, S5_LANES), F32)],
        compiler_params=_cparams(),
        name="s5_step",
    )(u, h0r, h0i, bre, bim, a_re.reshape(1, S5_LANES), a_im.reshape(1, S5_LANES), cre, cim, d, wglu)


def _mla_prompt_kernel(iq_ref, jk_ref, q_ref, k_ref, v_ref, o_ref, m_ref, acc_ref, *, hps):
    p = pl.program_id(1)
    i, j = iq_ref[p], jk_ref[p]
    tq, tk = q_ref.shape[0], k_ref.shape[0]
    low = lax.broadcasted_iota(jnp.int32, (1, LANES), 1) < MLA_V

    @pl.when(j == 0)
    def _():
        m_ref[...] = jnp.full_like(m_ref, -jnp.inf)
        acc_ref[...] = jnp.zeros_like(acc_ref)

    def scores(hh):
        sl = slice(hh * LANES, (hh + 1) * LANES)
        return _dot_nt(q_ref[:, sl], k_ref[:, sl])

    def values(hh):
        v = v_ref[:, (hh // 2) * LANES:(hh // 2 + 1) * LANES]
        return jnp.where(low, v, jnp.ones_like(v)) if hh % 2 == 0 else jnp.where(low, jnp.ones_like(v), v)

    def block(diagonal):
        ahead = 2
        queue = [scores(hh) for hh in range(min(ahead, hps))]
        pending = []
        for hh in range(hps):
            s = queue.pop(0)
            if hh + ahead < hps:
                queue.append(scores(hh + ahead))
            if diagonal:
                row = lax.broadcasted_iota(jnp.int32, (tq, tk), 0)
                col = lax.broadcasted_iota(jnp.int32, (tq, tk), 1)
                s = jnp.where(col <= row, s, -jnp.inf)
            m_prev = m_ref[hh]
            m_new = jnp.maximum(m_prev, jnp.max(s, axis=-1, keepdims=True))
            pending.append((hh, jnp.exp2(s - m_new).astype(BF16), jnp.exp2(m_prev - m_new)))
            m_ref[hh] = m_new
            for ph, pr, alpha in pending[:-1] if hh + 1 < hps else pending:
                acc_ref[ph] = alpha * acc_ref[ph] + _dot(pr, values(ph))
            pending = pending[-1:] if hh + 1 < hps else []

    @pl.when(j < i)
    def _():
        block(False)

    @pl.when(j == i)
    def _():
        block(True)
        for pair in range(hps // 2):
            a0, a1 = acc_ref[2 * pair], acc_ref[2 * pair + 1]
            o = jnp.where(low, a0 / pltpu.roll(a0, MLA_V, axis=1), a1 / pltpu.roll(a1, MLA_V, axis=1))
            o_ref[:, pair * LANES:(pair + 1) * LANES] = o.astype(BF16)


def _mla_prompt_attn(q, k, v, tile, hps):
    t = q.shape[0]
    nq = t // tile
    pairs = [(i, j) for i in range(nq) for j in range(i + 1)]
    iq = jnp.asarray([p[0] for p in pairs], jnp.int32)
    jk = jnp.asarray([p[1] for p in pairs], jnp.int32)
    grid_spec = pltpu.PrefetchScalarGridSpec(
        num_scalar_prefetch=2,
        grid=(MLA_HEADS // hps, len(pairs)),
        in_specs=[pl.BlockSpec((tile, hps * LANES), lambda h, p, iq, jk: (iq[p], h)),
                  pl.BlockSpec((tile, hps * LANES), lambda h, p, iq, jk: (jk[p], h)),
                  pl.BlockSpec((tile, hps * MLA_V), lambda h, p, iq, jk: (jk[p], h))],
        out_specs=pl.BlockSpec((tile, hps * MLA_V), lambda h, p, iq, jk: (iq[p], h)),
        scratch_shapes=[pltpu.VMEM((hps, tile, 1), F32), pltpu.VMEM((hps, tile, LANES), F32)],
    )
    return pl.pallas_call(
        functools.partial(_mla_prompt_kernel, hps=hps),
        grid_spec=grid_spec,
        out_shape=jax.ShapeDtypeStruct((t, MLA_HEADS * MLA_V), BF16),
        compiler_params=_cparams(("arbitrary", "arbitrary")),
        name="mla_prompt_attn",
    )(iq, jk, q, k, v)


def _qlat_kernel(q_ref, wukt_ref, o_ref):
    for hd in range(MLA_HEADS):
        o_ref[hd] = _dot(q_ref[:, hd * LANES:(hd + 1) * LANES], wukt_ref[hd]).astype(BF16)


def _uv_kernel(o_ref, w_ref, y_ref):
    y_ref[...] = _dot(o_ref[...], w_ref[...]).astype(BF16)


def _mla_sample_kernel(pt_ref, ql_ref, qr_ref, cn_ref, kn_ref, ckv_hbm, krt_hbm, o_ref,
                       cbuf, kbuf, sem, *, npages, sub):
    b = pl.program_id(0)
    nb = pl.num_programs(0)
    slot = lax.rem(b, 2)

    def copies(bb, sl):
        out = []
        for p in range(npages):
            page = pt_ref[bb, p]
            tok = pl.ds(p * PAGE, PAGE)
            out.append(pltpu.make_async_copy(ckv_hbm.at[0, page], cbuf.at[sl, tok], sem.at[sl, 0]))
            out.append(pltpu.make_async_copy(krt_hbm.at[0, page], kbuf.at[sl, :, tok], sem.at[sl, 1]))
        return out

    @pl.when(b == 0)
    def _():
        for cp in copies(b, slot):
            cp.start()

    @pl.when(b + 1 < nb)
    def _():
        for cp in copies(b + 1, 1 - slot):
            cp.start()

    ql = ql_ref[0]
    qr = qr_ref[0]
    cn = cn_ref[0].astype(BF16).astype(F32)
    kn = kn_ref[0].astype(BF16).astype(F32)
    s_new = (jnp.sum(ql.astype(F32) * cn, axis=-1, keepdims=True)
             + jnp.sum(qr.astype(F32) * kn, axis=-1, keepdims=True))

    for cp in copies(b, slot):
        cp.wait()

    parts = []
    for sc in range(npages * PAGE // sub):
        tok = pl.ds(sc * sub, sub)
        cpg = cbuf[slot, tok, :].astype(BF16)
        krp = kbuf[slot, :, tok].astype(BF16)
        s = _dot_nt(ql, cpg) + _dot(qr, krp)
        m = jnp.max(s, axis=-1, keepdims=True)
        pr = jnp.exp(s - m)
        parts.append((m, jnp.sum(pr, axis=-1, keepdims=True), _dot(pr.astype(BF16), cpg)))
    m_all = s_new
    for m, _, _ in parts:
        m_all = jnp.maximum(m_all, m)
    w_new = jnp.exp(s_new - m_all)
    den = w_new
    acc = w_new * cn
    for m, l, o in parts:
        w = jnp.exp(m - m_all)
        den = den + w * l
        acc = acc + w * o
    o_ref[0] = (acc / den).astype(BF16)


def _mla_sample_attn(q, c_new, kr_new, cache_ckv, cache_krope, page_table, wukt, wuv_blk, sub):
    b = q.shape[0]
    npages = page_table.shape[1]
    qlat = pl.pallas_call(
        _qlat_kernel,
        out_shape=jax.ShapeDtypeStruct((MLA_HEADS, b, KV_LORA), BF16),
        name="mla_qlat",
    )(q, wukt)
    ql = qlat.transpose(1, 0, 2)
    qr = q.reshape(b, MLA_HEADS, LANES)[:, :, MLA_NOPE:MLA_QK]
    krt = jnp.swapaxes(cache_krope, 2, 3)
    n = npages * PAGE
    grid_spec = pltpu.PrefetchScalarGridSpec(
        num_scalar_prefetch=1,
        grid=(b,),
        in_specs=[pl.BlockSpec((1, MLA_HEADS, KV_LORA), lambda i, pt: (i, 0, 0)),
                  pl.BlockSpec((1, MLA_HEADS, MLA_ROPE), lambda i, pt: (i, 0, 0)),
                  pl.BlockSpec((1, 1, KV_LORA), lambda i, pt: (i, 0, 0)),
                  pl.BlockSpec((1, 1, MLA_ROPE), lambda i, pt: (i, 0, 0)),
                  pl.BlockSpec(memory_space=pl.ANY),
                  pl.BlockSpec(memory_space=pl.ANY)],
        out_specs=pl.BlockSpec((1, MLA_HEADS, KV_LORA), lambda i, pt: (i, 0, 0)),
        scratch_shapes=[pltpu.VMEM((2, n, KV_LORA), F32), pltpu.VMEM((2, MLA_ROPE, n), F32),
                        pltpu.SemaphoreType.DMA((2, 2))],
    )
    o_lat = pl.pallas_call(
        functools.partial(_mla_sample_kernel, npages=npages, sub=sub),
        grid_spec=grid_spec,
        out_shape=jax.ShapeDtypeStruct((b, MLA_HEADS, KV_LORA), BF16),
        compiler_params=_cparams(("arbitrary",)),
        name="mla_sample_attn",
    )(page_table, ql, qr, c_new.reshape(b, 1, KV_LORA), kr_new.reshape(b, 1, MLA_ROPE), cache_ckv, krt)
    return pl.pallas_call(
        _uv_kernel,
        out_shape=jax.ShapeDtypeStruct((b, MLA_HEADS * MLA_V), BF16),
        name="mla_sample_uv",
    )(o_lat.reshape(b, MLA_HEADS * KV_LORA), wuv_blk)


def _router_gates(h2, wr_ref, br_ref):
    h_hi = h2.astype(BF16)
    h_lo = (h2 - h_hi.astype(F32)).astype(BF16)
    logits = _dot(h_hi, wr_ref[0]) + (_dot(h_lo, wr_ref[0]) + _dot(h_hi, wr_ref[1]))
    scores = jax.nn.sigmoid(logits)
    sel = scores + br_ref[...]
    lane = lax.broadcasted_iota(jnp.int32, (1, LANES), 1)
    pos = lane % GROUP_SIZE
    others, wrapped = [], []
    for r in range(1, GROUP_SIZE):
        wrap = pos + r >= GROUP_SIZE
        fwd = pltpu.roll(sel, LANES - r, axis=1)
        bwd = pltpu.roll(sel, GROUP_SIZE - r, axis=1)
        others.append(jnp.where(wrap, bwd, fwd))
        wrapped.append(wrap)
    a, b, c, d = sel, others[0], others[1], others[2]
    hi1, lo1 = jnp.maximum(a, b), jnp.minimum(a, b)
    hi2, lo2 = jnp.maximum(c, d), jnp.minimum(c, d)
    gscore = jnp.maximum(hi1, hi2) + jnp.maximum(jnp.minimum(hi1, hi2), jnp.maximum(lo1, lo2))
    real = lane < N_EXPERTS
    gscore = jnp.where(real, gscore, -jnp.inf)
    gmax = jnp.max(gscore, axis=-1, keepdims=True)
    gidx = (lane // GROUP_SIZE).astype(F32)
    chosen = jnp.min(jnp.where(gscore == gmax, gidx, float(LANES)), axis=-1, keepdims=True)
    rank = jnp.zeros(sel.shape, F32)
    for o, wrap in zip(others, wrapped):
        rank = rank + jnp.where(wrap, jnp.where(o >= sel, 1.0, 0.0), jnp.where(o > sel, 1.0, 0.0))
    w = jnp.where(gidx == chosen, jnp.where(rank < 2.0, scores, 0.0), 0.0)
    return w / jnp.sum(w, axis=-1, keepdims=True)


def _post_kernel(*refs, n_mix):
    x_ref = refs[0]
    ys = refs[1:1 + n_mix]
    ws = refs[1 + n_mix:1 + 2 * n_mix]
    gt_ref, sh_ref, sc_ref, g_ref, wr_ref, br_ref, x1_ref, h2_ref, gate_ref = refs[1 + 2 * n_mix:]
    mix = None
    for y_ref, w_ref in zip(ys, ws):
        t = _dot(y_ref[...], w_ref[...])
        mix = t if mix is None else mix + t
    x1 = x_ref[...] + gt_ref[...] * mix
    x1_ref[...] = x1
    h2 = _rms(x1, g_ref[...]) * (1.0 + sc_ref[...]) + sh_ref[...]
    h2_ref[...] = h2.astype(BF16)
    gate_ref[...] = _router_gates(h2, wr_ref, br_ref)


def _post(x, ys, ws, gt, sh, sc, g, wr, br, tile):
    r = x.shape[0]
    per_row = gt.shape[0] != 1
    n_mix = len(ys)
    in_specs = ([pl.BlockSpec((tile, D_MODEL), lambda i: (i, 0))]
                + [pl.BlockSpec((tile, y.shape[1]), lambda i: (i, 0)) for y in ys]
                + [_full(w.shape) for w in ws]
                + [_rows(tile, D_MODEL, per_row)] * 3
                + [_full(g.shape), _full(wr.shape), _full(br.shape)])
    widths = [(D_MODEL, F32), (D_MODEL, BF16), (LANES, F32)]
    return pl.pallas_call(
        functools.partial(_post_kernel, n_mix=n_mix),
        grid=(r // tile,),
        in_specs=in_specs,
        out_specs=[pl.BlockSpec((tile, w), lambda i: (i, 0)) for w, _ in widths],
        out_shape=[jax.ShapeDtypeStruct((r, w), dt) for w, dt in widths],
        compiler_params=_cparams(("arbitrary",)),
        name="post_mixer",
    )(x, *ys, *ws, gt, sh, sc, g, wr, br)


def _moe_kernel(h_ref, gate_ref, x1_ref, gt_ref, wg_ref, wu_ref, wd_ref, o_ref, acc_ref):
    e = pl.program_id(1)

    @pl.when(e == 0)
    def _():
        acc_ref[...] = jnp.zeros_like(acc_ref)

    h = h_ref[...]
    a = _dot(h, wg_ref[0, 0])
    u = _dot(h, wu_ref[0, 0])
    lane = lax.broadcasted_iota(jnp.int32, (1, LANES), 1)
    gcol = jnp.sum(jnp.where(lane == e, gate_ref[...], 0.0), axis=-1, keepdims=True)
    act = (a * jax.nn.sigmoid(a)) * u * gcol
    acc_ref[...] += _dot(act.astype(BF16), wd_ref[0, 0])

    @pl.when(e == pl.num_programs(1) - 1)
    def _():
        o_ref[...] = x1_ref[...] + gt_ref[...] * acc_ref[...]


def _moe(h2, gate, x1, gt, wg, wu, wd, layer, tile):
    r = h2.shape[0]
    per_row = gt.shape[0] != 1
    gt_spec = (pl.BlockSpec((tile, D_MODEL), lambda i, e: (i, 0)) if per_row
               else pl.BlockSpec((1, D_MODEL), lambda i, e: (0, 0)))
    return pl.pallas_call(
        _moe_kernel,
        grid=(r // tile, N_EXPERTS),
        in_specs=[pl.BlockSpec((tile, D_MODEL), lambda i, e: (i, 0)),
                  pl.BlockSpec((tile, LANES), lambda i, e: (i, 0)),
                  pl.BlockSpec((tile, D_MODEL), lambda i, e: (i, 0)),
                  gt_spec,
                  pl.BlockSpec((1, 1, D_MODEL, D_FF), lambda i, e: (layer, e, 0, 0)),
                  pl.BlockSpec((1, 1, D_MODEL, D_FF), lambda i, e: (layer, e, 0, 0)),
                  pl.BlockSpec((1, 1, D_FF, D_MODEL), lambda i, e: (layer, e, 0, 0))],
        out_specs=pl.BlockSpec((tile, D_MODEL), lambda i, e: (i, 0)),
        out_shape=jax.ShapeDtypeStruct((r, D_MODEL), F32),
        scratch_shapes=[pltpu.VMEM((tile, D_MODEL), F32)],
        compiler_params=_cparams(("arbitrary", "arbitrary")),
        name="moe",
    )(h2, gate, x1, gt, wg, wu, wd)


def _front_odd_kernel(x_ref, sh_ref, sc_ref, g_ref, win_ref, gq_ref, gk_ref, rc_ref, rs_ref, rot_ref,
                      q_ref, k_ref, v_ref, ku_ref, vu_ref):
    x = x_ref[...]
    h = _rms(x, g_ref[...]) * (1.0 + sc_ref[...]) + sh_ref[...]
    z = _dot(h.astype(BF16), win_ref[...])
    rc, rs, rot = rc_ref[...], rs_ref[...], rot_ref[...]
    scale = SWA_HD ** -0.5
    for hd in range(SWA_HEADS):
        sl = slice(hd * LANES, (hd + 1) * LANES)
        qn = _rope(_rms(z[:, sl], gq_ref[...], SWA_HD), rot, rc, rs, False)
        q_ref[:, sl] = (qn * scale).astype(BF16)
    ks, vs = [], []
    for kh in range(SWA_KV):
        sl = slice(kh * LANES, (kh + 1) * LANES)
        zk = z[:, (SWA_HEADS + kh) * LANES:(SWA_HEADS + kh + 1) * LANES]
        kn = _rope(_rms(zk, gk_ref[...], SWA_HD), rot, rc, rs, True)
        vv = z[:, (SWA_HEADS + SWA_KV + kh) * LANES:(SWA_HEADS + SWA_KV + kh + 1) * LANES]
        k_ref[:, sl] = kn.astype(BF16)
        v_ref[:, sl] = vv.astype(BF16)
        ks.append(kn)
        vs.append(vv)
    for j in range(SWA_KV // 2):
        sl = slice(j * LANES, (j + 1) * LANES)
        ku_ref[:, sl] = ks[2 * j] + pltpu.roll(ks[2 * j + 1], SWA_HD, axis=1)
        vu_ref[:, sl] = vs[2 * j] + pltpu.roll(vs[2 * j + 1], SWA_HD, axis=1)


def _front_odd(x, sh, sc, g, win, gq, gk, tabs, tile):
    r = x.shape[0]
    per_row_mod = sh.shape[0] != 1
    per_row_tab = tabs[0].shape[0] != 1
    in_specs = [pl.BlockSpec((tile, D_MODEL), lambda i: (i, 0)),
                _rows(tile, D_MODEL, per_row_mod), _rows(tile, D_MODEL, per_row_mod),
                _full(g.shape), _full(win.shape), _full(gq.shape), _full(gk.shape),
                _rows(tile, LANES, per_row_tab), _rows(tile, LANES, per_row_tab), _full(tabs[2].shape)]
    widths = [(SWA_HEADS * LANES, BF16), (SWA_KV * LANES, BF16), (SWA_KV * LANES, BF16),
              (SWA_KV * SWA_HD, F32), (SWA_KV * SWA_HD, F32)]
    return pl.pallas_call(
        _front_odd_kernel,
        grid=(r // tile,),
        in_specs=in_specs,
        out_specs=[pl.BlockSpec((tile, w), lambda i: (i, 0)) for w, _ in widths],
        out_shape=[jax.ShapeDtypeStruct((r, w), dt) for w, dt in widths],
        compiler_params=_cparams(("arbitrary",)),
        name="front_odd",
    )(x, sh, sc, g, win, gq, gk, *tabs)


def _swa_prompt_kernel(sink_ref, q_ref, kp_ref, kc_ref, vp_ref, vc_ref, o_ref):
    n = pl.program_id(0)
    w = WINDOW
    qi = lax.broadcasted_iota(jnp.int32, (w, 2 * w), 0) + w
    kj = lax.broadcasted_iota(jnp.int32, (w, 2 * w), 1)
    valid = (kj <= qi) & (qi - kj < w) & (n * w - w + kj >= 0)
    for kh in range(SWA_KV):
        sl = slice(kh * LANES, (kh + 1) * LANES)
        kk = jnp.concatenate([kp_ref[:, sl], kc_ref[:, sl]], axis=0)
        vv = jnp.concatenate([vp_ref[:, sl], vc_ref[:, sl]], axis=0)
        for gi in range(SWA_GROUP):
            hd = kh * SWA_GROUP + gi
            hsl = slice(hd * LANES, (hd + 1) * LANES)
            s = jnp.where(valid, _dot_nt(q_ref[:, hsl], kk), -jnp.inf)
            sink = sink_ref[hd]
            m = jnp.maximum(jnp.max(s, axis=-1, keepdims=True), sink)
            pr = jnp.exp(s - m)
            den = jnp.sum(pr, axis=-1, keepdims=True) + jnp.exp(sink - m)
            o_ref[:, hsl] = (_dot(pr.astype(BF16), vv) / den).astype(BF16)


def _swa_prompt_attn(q, k, v, sinks):
    t = q.shape[0]
    w = WINDOW
    prev = lambda n, s: (jnp.maximum(n - 1, 0), 0)
    cur = lambda n, s: (n, 0)
    grid_spec = pltpu.PrefetchScalarGridSpec(
        num_scalar_prefetch=1,
        grid=(t // w,),
        in_specs=[pl.BlockSpec((w, SWA_HEADS * LANES), cur),
                  pl.BlockSpec((w, SWA_KV * LANES), prev), pl.BlockSpec((w, SWA_KV * LANES), cur),
                  pl.BlockSpec((w, SWA_KV * LANES), prev), pl.BlockSpec((w, SWA_KV * LANES), cur)],
        out_specs=pl.BlockSpec((w, SWA_HEADS * LANES), cur),
    )
    return pl.pallas_call(
        _swa_prompt_kernel,
        grid_spec=grid_spec,
        out_shape=jax.ShapeDtypeStruct((t, SWA_HEADS * LANES), BF16),
        compiler_params=_cparams(("arbitrary",)),
        name="swa_prompt_attn",
    )(sinks, q, k, k, v, v)


def _swa_sample_kernel(q_ref, kc_ref, vc_ref, kn_ref, vn_ref, sink_ref, o_ref, wk_ref, wv_ref, *, sb):
    w = WINDOW
    width = SWA_KV * SWA_HD
    rowgrp = lax.broadcasted_iota(jnp.int32, (SWA_HEADS, 1), 0) // SWA_GROUP
    col = lax.broadcasted_iota(jnp.int32, (1, w), 1)
    low = lax.broadcasted_iota(jnp.int32, (1, LANES), 1) < SWA_HD
    sink = sink_ref[...]
    for b in range(sb):
        qf = q_ref[b].astype(F32)
        a = jnp.concatenate([qf, jnp.zeros_like(qf)], axis=1)
        qx = a
        for kh in range(1, SWA_KV):
            qx = jnp.where(rowgrp == kh, pltpu.roll(a, kh * SWA_HD, axis=1), qx)
        kc = kc_ref[b]
        vc = vc_ref[b]
        kn = kn_ref[b]
        vn = vn_ref[b]
        s = _dot_nt(qx.astype(BF16), kc.astype(BF16))
        s = jnp.where(col >= 1, s, -jnp.inf)
        s_new = jnp.sum(qx.astype(BF16).astype(F32) * kn.astype(BF16).astype(F32), axis=-1, keepdims=True)
        m = jnp.maximum(jnp.maximum(jnp.max(s, axis=-1, keepdims=True), s_new), sink)
        pr = jnp.exp(s - m)
        pn = jnp.exp(s_new - m)
        den = jnp.sum(pr, axis=-1, keepdims=True) + pn + jnp.exp(sink - m)
        o = (_dot(pr.astype(BF16), vc.astype(BF16)) + pn * vn.astype(BF16).astype(F32)) / den
        ox = o
        for kh in range(1, SWA_KV):
            ox = jnp.where(rowgrp == kh, pltpu.roll(o, width - kh * SWA_HD, axis=1), ox)
        o_ref[b] = jnp.where(low, ox[:, :LANES], 0.0).astype(BF16)
        wk_ref[b, pl.ds(0, w - 1), :] = kc_ref[b, pl.ds(1, w - 1), :]
        wk_ref[b, pl.ds(w - 1, 1), :] = kn
        wv_ref[b, pl.ds(0, w - 1), :] = vc_ref[b, pl.ds(1, w - 1), :]
        wv_ref[b, pl.ds(w - 1, 1), :] = vn


def _swa_sample_attn(q, kc, vc, kn, vn, sinks, sb):
    b = q.shape[0]
    w = WINDOW
    width = SWA_KV * SWA_HD
    blk3 = lambda s1, s2: pl.BlockSpec((sb, s1, s2), lambda i: (i, 0, 0))
    return pl.pallas_call(
        functools.partial(_swa_sample_kernel, sb=sb),
        grid=(b // sb,),
        in_specs=[blk3(SWA_HEADS, LANES), blk3(w, width), blk3(w, width), blk3(1, width), blk3(1, width),
                  _full((SWA_HEADS, 1))],
        out_specs=[blk3(SWA_HEADS, LANES), blk3(w, width), blk3(w, width)],
        out_shape=[jax.ShapeDtypeStruct((b, SWA_HEADS, LANES), BF16),
                   jax.ShapeDtypeStruct((b, w, width), F32), jax.ShapeDtypeStruct((b, w, width), F32)],
        compiler_params=_cparams(("arbitrary",)),
        name="swa_sample_attn",
    )(q.reshape(b, SWA_HEADS, LANES), kc, vc, kn.reshape(b, 1, width), vn.reshape(b, 1, width),
      sinks.reshape(SWA_HEADS, 1))


def _rope_cos_sin(pos):
    inv = jnp.concatenate([ROPE_THETA ** (-jnp.arange(h, dtype=F32) / h) for h in (MLA_ROPE // 2, SWA_ROT // 2)])
    ang = pos.astype(F32)[:, None] * inv[None, :]
    cos, sin = jnp.cos(ang), jnp.sin(ang)
    cut = MLA_ROPE // 2
    return (cos[:, :cut], sin[:, :cut]), (cos[:, cut:], sin[:, cut:])


def _rope_tables(cos_sin, offset):
    cos, sin = cos_sin
    n, half = cos.shape
    rest = LANES - offset - 2 * half
    rc = jnp.concatenate([jnp.ones((n, offset), F32), cos, cos, jnp.ones((n, rest), F32)], axis=1)
    rs = jnp.concatenate([jnp.zeros((n, offset), F32), sin, sin, jnp.zeros((n, rest), F32)], axis=1)
    lane = jnp.arange(LANES)
    first = (lane >= offset) & (lane < offset + half)
    second = (lane >= offset + half) & (lane < offset + 2 * half)
    src = lane[:, None]
    rot = (jnp.where(first[None, :] & (src == lane[None, :] + half), -1.0, 0.0)
           + jnp.where(second[None, :] & (src == lane[None, :] - half), 1.0, 0.0))
    return rc, rs, rot.astype(BF16)


def _pad_lanes(v, offset=0):
    return jnp.zeros((1, LANES), F32).at[0, offset:offset + v.shape[0]].set(v)


def _even_weights(w_in, g_qa, w_uq, g_qh, g_kva, g_kr, w_uk, w_uv):
    o3 = S5_WIDTH + Q_LORA + KV_LORA
    win = jnp.zeros((D_MODEL, 1280), F32).at[:, :o3].set(w_in[:, :o3])
    win = win.at[:, o3 + MLA_NOPE:o3 + MLA_QK].set(w_in[:, o3:])
    wuq = jnp.zeros((MLA_HEADS, Q_LORA, LANES), F32).at[:, :, :MLA_QK].set(w_uq.transpose(1, 0, 2))
    wuk = jnp.zeros((KV_LORA, MLA_HEADS, LANES), F32).at[:, :, :MLA_NOPE].set(w_uk)
    front = (win.astype(BF16), g_qa.reshape(1, Q_LORA), wuq.astype(BF16), _pad_lanes(g_qh),
             g_kva.reshape(1, KV_LORA), _pad_lanes(g_kr, MLA_NOPE),
             wuk.reshape(KV_LORA, MLA_HEADS * LANES).astype(BF16),
             w_uv.reshape(KV_LORA, MLA_HEADS * MLA_V).astype(BF16))
    wukt = jnp.zeros((MLA_HEADS, LANES, KV_LORA), F32).at[:, :MLA_NOPE, :].set(w_uk.transpose(1, 2, 0))
    eye = jnp.eye(MLA_HEADS, dtype=F32)
    wuv_blk = jnp.einsum('chd,hg->hcgd', w_uv, eye).reshape(MLA_HEADS * KV_LORA, MLA_HEADS * MLA_V)
    return front, wukt.astype(BF16), wuv_blk.astype(BF16)


def _odd_weights(w_in, w_out):
    nq, nk = SWA_HEADS * SWA_HD, SWA_KV * SWA_HD
    nh = SWA_HEADS + 2 * SWA_KV
    win = jnp.zeros((D_MODEL, nh, LANES), F32).at[:, :, :SWA_HD].set(w_in.reshape(D_MODEL, nh, SWA_HD))
    wout = jnp.zeros((SWA_HEADS, LANES, D_MODEL), F32).at[:, :SWA_HD, :].set(w_out.reshape(SWA_HEADS, SWA_HD, D_MODEL))
    del nq, nk
    return win.reshape(D_MODEL, nh * LANES).astype(BF16), wout.reshape(SWA_HEADS * LANES, D_MODEL).astype(BF16)


def kernel(x_prompt, x_sample, c_prompt, c_sample, cache_ckv, cache_krope, page_table, state_s5_re, state_s5_im, cache_win_k, cache_win_v, w_mod, b_mod, g_norm_mix, g_norm_ffn, w_in_even, w_out_even, s5_lam_re, s5_lam_im, s5_log_step, s5_b_re, s5_b_im, s5_c_re, s5_c_im, s5_d, s5_w_glu, mla_g_qa, mla_w_uq, mla_g_qh, mla_g_kva, mla_g_kr, mla_w_uk, mla_w_uv, w_in_odd, w_out_odd, swa_g_q, swa_g_k, swa_sinks, w_router, b_router, moe_w_gate, moe_w_up, moe_w_down):
    t = x_prompt.shape[1]
    nb = x_sample.shape[0]
    past_len = page_table.shape[1] * PAGE
    row_tile = 512
    s5_seg = 64
    attn_tile, attn_heads = 1024, MLA_HEADS
    mla_scale = MLA_QK ** -0.5
    moe_tile = 1024
    sample_sub = past_len
    swa_seq_block = 8

    n_c = 1 + nb
    rp = -(-n_c // SUBLANES) * SUBLANES
    c_all = jnp.concatenate([c_prompt, c_sample, jnp.zeros((rp - n_c, D_MODEL), F32)], axis=0)
    mod = _modulation(c_all, w_mod, b_mod)

    def mods(layer, sample):
        rows = mod[layer, 1:1 + nb] if sample else mod[layer, 0:1]
        return [rows[:, k * D_MODEL:(k + 1) * D_MODEL] for k in range(6)]

    wr = jnp.zeros((D_MODEL, LANES), F32).at[:, :N_EXPERTS].set(w_router)
    wr_hi = wr.astype(BF16)
    wr = jnp.stack([wr_hi, (wr - wr_hi.astype(F32)).astype(BF16)])
    br = _pad_lanes(b_router)
    wg, wu, wd = moe_w_gate.astype(BF16), moe_w_up.astype(BF16), moe_w_down.astype(BF16)

    xp = x_prompt.reshape(t, D_MODEL)
    xs = x_sample.reshape(nb, D_MODEL)
    mla_cs_p, swa_cs_p = _rope_cos_sin(jnp.arange(t))
    mla_cs_s, swa_cs_s = _rope_cos_sin(jnp.full((1,), past_len))
    mla_tabs_p = _rope_tables(mla_cs_p, MLA_NOPE)
    mla_tabs_s = _rope_tables(mla_cs_s, MLA_NOPE)
    swa_tabs_p = _rope_tables(swa_cs_p, 0)
    swa_tabs_s = _rope_tables(swa_cs_s, 0)

    front_w, wukt, wuv_blk = _even_weights(w_in_even[0], mla_g_qa[0], mla_w_uq[0], mla_g_qh[0], mla_g_kva[0],
                                           mla_g_kr[0], mla_w_uk[0], mla_w_uv[0])
    a_re, a_im, bb_re, bb_im, al_re, al_im = _s5_discretize(s5_lam_re[0], s5_lam_im[0], s5_log_step[0],
                                                            s5_b_re[0], s5_b_im[0], s5_seg)
    bre, bim, cre, cim = _s5_layouts(bb_re, bb_im, s5_c_re[0], s5_c_im[0])
    d_skip = s5_d[0].reshape(1, S5_WIDTH)
    wglu = s5_w_glu[0].astype(BF16)
    wo_ssm = w_out_even[0, :S5_WIDTH].astype(BF16)
    wo_att = w_out_even[0, S5_WIDTH:].astype(BF16)
    g_mix0 = g_norm_mix[0].reshape(1, D_MODEL)
    g_ffn0 = g_norm_ffn[0].reshape(1, D_MODEL)

    sh1, sc1, gt1, sh2, sc2, gt2 = mods(0, False)
    u, q, k, v, ckv_p, kr_p = _front_even(xp, sh1, sc1, g_mix0, front_w, mla_tabs_p,
                                          row_tile, mla_scale * math.log2(math.e))
    y_ssm, st = _s5_scan(u, bre, bim, a_re, a_im, al_re, al_im, cre, cim, d_skip, wglu, s5_seg)
    sr_p = st[:S5_CHUNKS].reshape(1, 1, S5_GROUPS, S5_STATE)
    si_p = st[S5_CHUNKS:].reshape(1, 1, S5_GROUPS, S5_STATE)
    y_att = _mla_prompt_attn(q, k, v, attn_tile, attn_heads)
    x1, h2, gate = _post(xp, [y_ssm, y_att], [wo_ssm, wo_att], gt1, sh2, sc2, g_ffn0, wr, br, row_tile)
    xp = _moe(h2, gate, x1, gt2, wg, wu, wd, 0, moe_tile)

    sh1, sc1, gt1, sh2, sc2, gt2 = mods(0, True)
    u, q, _, _, ckv_s, kr_s = _front_even(xs, sh1, sc1, g_mix0, front_w, mla_tabs_s, nb, mla_scale)
    y_ssm, sr_s, si_s = _s5_step(u, state_s5_re[0].reshape(nb, S5_LANES), state_s5_im[0].reshape(nb, S5_LANES),
                                 bre, bim, a_re, a_im, cre, cim, d_skip, wglu)
    y_att = _mla_sample_attn(q, ckv_s, kr_s, cache_ckv, cache_krope, page_table, wukt, wuv_blk, sample_sub)
    x1, h2, gate = _post(xs, [y_ssm, y_att], [wo_ssm, wo_att], gt1, sh2, sc2, g_ffn0, wr, br, nb)
    xs = _moe(h2, gate, x1, gt2, wg, wu, wd, 0, nb)

    win_odd, wout_odd = _odd_weights(w_in_odd[0], w_out_odd[0])
    gq, gk = _pad_lanes(swa_g_q[0]), _pad_lanes(swa_g_k[0])
    g_mix1 = g_norm_mix[1].reshape(1, D_MODEL)
    g_ffn1 = g_norm_ffn[1].reshape(1, D_MODEL)
    sinks = swa_sinks[0]

    sh1, sc1, gt1, sh2, sc2, gt2 = mods(1, False)
    q, k, v, ku, vu = _front_odd(xp, sh1, sc1, g_mix1, win_odd, gq, gk, swa_tabs_p, row_tile)
    o = _swa_prompt_attn(q, k, v, sinks)
    wk_p = ku[t - WINDOW:].reshape(1, 1, WINDOW, SWA_KV, SWA_HD)
    wv_p = vu[t - WINDOW:].reshape(1, 1, WINDOW, SWA_KV, SWA_HD)
    x1, h2, gate = _post(xp, [o], [wout_odd], gt1, sh2, sc2, g_ffn1, wr, br, row_tile)
    xp = _moe(h2, gate, x1, gt2, wg, wu, wd, 1, moe_tile)

    sh1, sc1, gt1, sh2, sc2, gt2 = mods(1, True)
    q, _, _, ku, vu = _front_odd(xs, sh1, sc1, g_mix1, win_odd, gq, gk, swa_tabs_s, nb)
    width = SWA_KV * SWA_HD
    o, wk_s, wv_s = _swa_sample_attn(q, cache_win_k[0].reshape(nb, WINDOW, width),
                                     cache_win_v[0].reshape(nb, WINDOW, width), ku, vu, sinks, swa_seq_block)
    x1, h2, gate = _post(xs, [o.reshape(nb, SWA_HEADS * LANES)], [wout_odd], gt1, sh2, sc2, g_ffn1, wr, br, nb)
    xs = _moe(h2, gate, x1, gt2, wg, wu, wd, 1, nb)

    return (xp.reshape(1, t, D_MODEL), xs.reshape(nb, 1, D_MODEL),
            ckv_p.reshape(1, 1, t, KV_LORA), kr_p.reshape(1, 1, t, MLA_ROPE), sr_p, si_p, wk_p, wv_p,
            ckv_s.reshape(1, nb, 1, KV_LORA), kr_s.reshape(1, nb, 1, MLA_ROPE),
            sr_s.reshape(1, nb, S5_GROUPS, S5_STATE), si_s.reshape(1, nb, S5_GROUPS, S5_STATE),
            wk_s.reshape(1, nb, WINDOW, SWA_KV, SWA_HD), wv_s.reshape(1, nb, WINDOW, SWA_KV, SWA_HD))
```

```python
import functools
import math

import jax
import jax.numpy as jnp
from jax import lax
from jax.experimental import pallas as pl
from jax.experimental.pallas import tpu as pltpu

F32 = jnp.float32
BF16 = jnp.bfloat16
EPS = 1e-6
ROPE_THETA = 500000.0
LANES = 128
SUBLANES = 8
VMEM_LIMIT = 56 * 1024 * 1024

D_MODEL = 1024
PAGE = 128
S5_WIDTH = 512
S5_GROUP = 16
S5_GROUPS = 32
S5_STATE = 64
S5_LANES = S5_GROUPS * S5_STATE
S5_CHUNKS = S5_LANES // LANES
MLA_HEADS = 8
MLA_NOPE = 64
MLA_ROPE = 32
MLA_QK = 96
MLA_V = 64
Q_LORA = 384
KV_LORA = 256
SWA_HEADS = 16
SWA_KV = 4
SWA_GROUP = 4
SWA_HD = 64
SWA_ROT = 16
WINDOW = 128
N_EXPERTS = 16
GROUP_SIZE = 4
D_FF = 256

_NT = (((1,), (1,)), ((), ()))


def _dot(a, b):
    return jnp.dot(a, b, preferred_element_type=F32)


def _dot_nt(a, b):
    return lax.dot_general(a, b, _NT, preferred_element_type=F32)


def _rms(x, g, n=None):
    n = x.shape[-1] if n is None else n
    ss = jnp.sum(x * x, axis=-1, keepdims=True) * (1.0 / n)
    return x * lax.rsqrt(ss + EPS) * g


def _rope(x, rot, rc, rs, exact):
    hi = x.astype(BF16)
    partner = _dot(hi, rot)
    if exact:
        partner = partner + _dot((x - hi.astype(F32)).astype(BF16), rot)
    return x * rc + partner * rs


def _cparams(sem=None, vmem=VMEM_LIMIT):
    return pltpu.CompilerParams(dimension_semantics=sem, vmem_limit_bytes=vmem)


def _full(shape):
    nd = len(shape)
    return pl.BlockSpec(shape, lambda *_: (0,) * nd)


def _rows(tile, width, per_row):
    if per_row:
        return pl.BlockSpec((tile, width), lambda i: (i, 0))
    return pl.BlockSpec((1, width), lambda i: (0, 0))


def _mod_kernel(c_ref, w_ref, b_ref, o_ref):
    c = c_ref[...]
    s = (c * jax.nn.sigmoid(c)).astype(BF16)
    o_ref[0] = _dot(s, w_ref[0].astype(BF16)) + b_ref[0]


def _modulation(c_all, w_mod, b_mod):
    depth, _, n = w_mod.shape
    rp = c_all.shape[0]
    tn = 1536
    return pl.pallas_call(
        _mod_kernel,
        grid=(depth, n // tn),
        in_specs=[pl.BlockSpec((rp, D_MODEL), lambda l, j: (0, 0)),
                  pl.BlockSpec((1, D_MODEL, tn), lambda l, j: (l, 0, j)),
                  pl.BlockSpec((1, 1, tn), lambda l, j: (l, 0, j))],
        out_specs=pl.BlockSpec((1, rp, tn), lambda l, j: (l, 0, j)),
        out_shape=jax.ShapeDtypeStruct((depth, rp, n), F32),
        compiler_params=_cparams(("arbitrary", "arbitrary")),
        name="modulation",
    )(c_all, w_mod, b_mod.reshape(depth, 1, n))


def _front_even_kernel(x_ref, sh_ref, sc_ref, g_ref, win_ref, gqa_ref, wuq_ref, gqh_ref, gkva_ref,
                       gkr_ref, wuk_ref, wuv_ref, rc_ref, rs_ref, rot_ref,
                       u_ref, q_ref, k_ref, v_ref, ckv_ref, kr_ref, *, q_scale):
    x = x_ref[...]
    h = _rms(x, g_ref[...]) * (1.0 + sc_ref[...]) + sh_ref[...]
    z = _dot(h.astype(BF16), win_ref[...])
    u_ref[...] = z[:, :512]
    c_q = _rms(z[:, 512:896], gqa_ref[...]).astype(BF16)
    c_kv = _rms(z[:, 896:1152], gkva_ref[...])
    ckv_ref[...] = c_kv
    rc, rs, rot = rc_ref[...], rs_ref[...], rot_ref[...]
    kr = _rope(_rms(z[:, 1152:1280], gkr_ref[...], MLA_ROPE), rot, rc, rs, True)
    kr_ref[...] = kr[:, MLA_NOPE:MLA_QK]
    ckvb = c_kv.astype(BF16)
    kn = _dot(ckvb, wuk_ref[...])
    v_ref[...] = _dot(ckvb, wuv_ref[...]).astype(BF16)
    for hd in range(MLA_HEADS):
        sl = slice(hd * LANES, (hd + 1) * LANES)
        qh = _dot(c_q, wuq_ref[hd])
        qn = _rope(_rms(qh, gqh_ref[...], MLA_QK), rot, rc, rs, False)
        q_ref[:, sl] = (qn * q_scale).astype(BF16)
        k_ref[:, sl] = (kn[:, sl] + kr).astype(BF16)


def _front_even(x, sh, sc, g, wts, tabs, tile, q_scale):
    r = x.shape[0]
    per_row_mod = sh.shape[0] != 1
    per_row_tab = tabs[0].shape[0] != 1
    win, gqa, wuq, gqh, gkva, gkr, wuk, wuv = wts
    in_specs = [pl.BlockSpec((tile, D_MODEL), lambda i: (i, 0)),
                _rows(tile, D_MODEL, per_row_mod), _rows(tile, D_MODEL, per_row_mod),
                _full(g.shape), _full(win.shape), _full(gqa.shape), _full(wuq.shape), _full(gqh.shape),
                _full(gkva.shape), _full(gkr.shape), _full(wuk.shape), _full(wuv.shape),
                _rows(tile, LANES, per_row_tab), _rows(tile, LANES, per_row_tab), _full(tabs[2].shape)]
    widths = [(S5_WIDTH, F32), (MLA_HEADS * LANES, BF16), (MLA_HEADS * LANES, BF16),
              (MLA_HEADS * MLA_V, BF16), (KV_LORA, F32), (MLA_ROPE, F32)]
    return pl.pallas_call(
        functools.partial(_front_even_kernel, q_scale=q_scale),
        grid=(r // tile,),
        in_specs=in_specs,
        out_specs=[pl.BlockSpec((tile, w), lambda i: (i, 0)) for w, _ in widths],
        out_shape=[jax.ShapeDtypeStruct((r, w), dt) for w, dt in widths],
        compiler_params=_cparams(("arbitrary",)),
        name="front_even",
    )(x, sh, sc, g, win, gqa, wuq, gqh, gkva, gkr, wuk, wuv, *tabs)


def _s5_disc_kernel(lr_ref, li_ref, ls_ref, br_ref, bi_ref, are_ref, aim_ref, bbr_ref, bbi_ref,
                    alr_ref, ali_ref, *, log2_len):
    lr, li = lr_ref[...], li_ref[...]
    dt = jnp.exp(ls_ref[...])
    mag = jnp.exp(lr * dt)
    ang = li * dt
    a_re = mag * jnp.cos(ang)
    a_im = mag * jnp.sin(ang)
    den = lr * lr + li * li
    k_re = ((a_re - 1.0) * lr + a_im * li) / den
    k_im = (a_im * lr - (a_re - 1.0) * li) / den
    are_ref[...] = a_re
    aim_ref[...] = a_im
    for c in range(S5_GROUP):
        br, bi = br_ref[c], bi_ref[c]
        bbr_ref[c] = k_re * br - k_im * bi
        bbi_ref[c] = k_re * bi + k_im * br
    pr, pi = a_re, a_im
    for _ in range(log2_len):
        pr, pi = pr * pr - pi * pi, 2.0 * pr * pi
    alr_ref[...] = pr
    ali_ref[...] = pi


def _s5_discretize(lam_re, lam_im, log_step, b_re, b_im, seg_len):
    g, n = lam_re.shape
    outs = [jax.ShapeDtypeStruct((g, n), F32)] * 2 + [jax.ShapeDtypeStruct((S5_GROUP, g, n), F32)] * 2 \
        + [jax.ShapeDtypeStruct((g, n), F32)] * 2
    return pl.pallas_call(
        functools.partial(_s5_disc_kernel, log2_len=int(math.log2(seg_len))),
        out_shape=outs,
        name="s5_discretize",
    )(lam_re, lam_im, log_step.reshape(g, 1), b_re.transpose(2, 0, 1), b_im.transpose(2, 0, 1))


def _s5_layouts(bb_re, bb_im, c_re, c_im):
    eye8 = jnp.eye(8, dtype=F32)

    def bmat(bb):
        b4 = bb.reshape(S5_GROUP, 4, 8, S5_STATE)
        m = jnp.einsum('cjgn,gh->jgchn', b4, eye8)
        return m.reshape(4, LANES, 512).astype(BF16)

    sel = jax.nn.one_hot((2 * jnp.arange(S5_CHUNKS)[:, None] + jnp.arange(2)[None, :]) % 8, 8, dtype=F32)

    def cmat(c):
        c4 = c.reshape(S5_CHUNKS, 2, S5_GROUP, S5_STATE)
        m = jnp.einsum('asck,asg->askgc', c4, sel)
        return m.reshape(S5_CHUNKS, LANES, LANES).astype(BF16)

    return bmat(bb_re), bmat(bb_im), cmat(c_re), cmat(-c_im)


def _gelu_glu(y, wglu_ref):
    z = jax.nn.gelu(y)
    return z * jax.nn.sigmoid(_dot(z.astype(BF16), wglu_ref[...]))


def _s5_scan_kernel(u0_ref, u1_ref, u2_ref, u3_ref, perm_ref, bre_ref, bim_ref, are_ref, aim_ref, alr_ref,
                    ali_ref, cre_ref, cim_ref, d_ref, wglu_ref, y_ref, st_ref, bu_ref, hs_ref, carry_ref,
                    *, seg_len):
    i = pl.program_id(0)

    @pl.when(i == 0)
    def _():
        carry_ref[...] = jnp.zeros_like(carry_ref)

    us = []
    for j, u_ref in enumerate((u0_ref, u1_ref, u2_ref, u3_ref)):
        uj = jnp.concatenate([u_ref[pl.ds(t, SUBLANES, stride=seg_len), :] for t in range(seg_len)], axis=0)
        us.append(uj)
        ub = uj.astype(BF16)
        re = _dot(ub, bre_ref[j])
        im = _dot(ub, bim_ref[j])
        for q in range(4):
            bu_ref[4 * j + q] = re[:, q * LANES:(q + 1) * LANES]
            bu_ref[S5_CHUNKS + 4 * j + q] = im[:, q * LANES:(q + 1) * LANES]

    per = 4
    for grp in range(S5_CHUNKS // per):
        cs = [grp * per + q for q in range(per)]
        ar = [jnp.broadcast_to(are_ref[c], (SUBLANES, LANES)) for c in cs]
        ai = [jnp.broadcast_to(aim_ref[c], (SUBLANES, LANES)) for c in cs]

        def advance(t, hs, store):
            out = []
            for k, c in enumerate(cs):
                hr, hi = hs[2 * k], hs[2 * k + 1]
                rows = pl.ds(pl.multiple_of(t * SUBLANES, SUBLANES), SUBLANES)
                nr = ar[k] * hr - ai[k] * hi + bu_ref[c, rows, :]
                ni = ar[k] * hi + ai[k] * hr + bu_ref[S5_CHUNKS + c, rows, :]
                if store:
                    hs_ref[c, rows, :] = nr
                    hs_ref[S5_CHUNKS + c, rows, :] = ni
                out += [nr, ni]
            return tuple(out)

        zero = tuple(jnp.zeros((SUBLANES, LANES), F32) for _ in range(2 * per))
        ends = lax.fori_loop(0, seg_len, lambda t, hs: advance(t, hs, False), zero, unroll=2)
        init = []
        for k, c in enumerate(cs):
            er, ei = ends[2 * k], ends[2 * k + 1]
            lr, li = alr_ref[c], ali_ref[c]
            hr, hi = carry_ref[c], carry_ref[S5_CHUNKS + c]
            rows_r, rows_i = [], []
            for s in range(SUBLANES):
                rows_r.append(hr)
                rows_i.append(hi)
                hr, hi = (er[s:s + 1] + lr * hr - li * hi, ei[s:s + 1] + lr * hi + li * hr)
            carry_ref[c] = hr
            carry_ref[S5_CHUNKS + c] = hi
            init += [jnp.concatenate(rows_r, axis=0), jnp.concatenate(rows_i, axis=0)]
        lax.fori_loop(0, seg_len, lambda t, hs: advance(t, hs, True), tuple(init), unroll=2)

    ys = []
    for j in range(4):
        acc = None
        for q in range(4):
            c = 4 * j + q
            t = _dot(hs_ref[c].astype(BF16), cre_ref[c]) + _dot(hs_ref[S5_CHUNKS + c].astype(BF16), cim_ref[c])
            acc = t if acc is None else acc + t
        ys.append(acc)
    y = jnp.concatenate(ys, axis=1) + d_ref[...] * jnp.concatenate(us, axis=1)
    gated = _gelu_glu(y, wglu_ref).astype(BF16)
    y_ref[...] = _dot(perm_ref[...], gated).astype(BF16)

    @pl.when(i == pl.num_programs(0) - 1)
    def _():
        st_ref[...] = carry_ref[...]


def _s5_scan(u, bre, bim, a_re, a_im, al_re, al_im, cre, cim, d, wglu, seg_len):
    t = u.shape[0]
    tile = SUBLANES * seg_len
    ch = lambda a: a.reshape(S5_CHUNKS, 1, LANES)
    r = jnp.arange(tile)
    perm = jax.nn.one_hot(SUBLANES * (r % seg_len) + r // seg_len, tile, dtype=BF16)
    args = (u, u, u, u, perm, bre, bim, ch(a_re), ch(a_im), ch(al_re), ch(al_im), cre, cim, d, wglu)
    in_specs = ([pl.BlockSpec((tile, LANES), functools.partial(lambda i, j: (i, j), j=j)) for j in range(4)]
                + [_full(a.shape) for a in args[4:]])
    return pl.pallas_call(
        functools.partial(_s5_scan_kernel, seg_len=seg_len),
        grid=(t // tile,),
        in_specs=in_specs,
        out_specs=[pl.BlockSpec((tile, S5_WIDTH), lambda i: (i, 0)), _full((2 * S5_CHUNKS, 1, LANES))],
        out_shape=[jax.ShapeDtypeStruct((t, S5_WIDTH), BF16),
                   jax.ShapeDtypeStruct((2 * S5_CHUNKS, 1, LANES), F32)],
        scratch_shapes=[pltpu.VMEM((2 * S5_CHUNKS, tile, LANES), F32),
                        pltpu.VMEM((2 * S5_CHUNKS, tile, LANES), F32),
                        pltpu.VMEM((2 * S5_CHUNKS, 1, LANES), F32)],
        compiler_params=_cparams(("arbitrary",)),
        name="s5_scan",
    )(*args)


def _s5_step_kernel(u_ref, h0r_ref, h0i_ref, bre_ref, bim_ref, are_ref, aim_ref, cre_ref, cim_ref,
                    d_ref, wglu_ref, y_ref, sr_ref, si_ref):
    u = u_ref[...]
    ub = u.astype(BF16)
    ys = []
    for j in range(4):
        uj = ub[:, j * LANES:(j + 1) * LANES]
        re = _dot(uj, bre_ref[j])
        im = _dot(uj, bim_ref[j])
        acc = None
        for q in range(4):
            sl = slice((4 * j + q) * LANES, (4 * j + q + 1) * LANES)
            ar, ai = are_ref[:, sl], aim_ref[:, sl]
            h0r, h0i = h0r_ref[:, sl], h0i_ref[:, sl]
            hr = ar * h0r - ai * h0i + re[:, q * LANES:(q + 1) * LANES]
            hi = ar * h0i + ai * h0r + im[:, q * LANES:(q + 1) * LANES]
            sr_ref[:, sl] = hr
            si_ref[:, sl] = hi
            t = _dot(hr.astype(BF16), cre_ref[4 * j + q]) + _dot(hi.astype(BF16), cim_ref[4 * j + q])
            acc = t if acc is None else acc + t
        ys.append(acc)
    y = jnp.concatenate(ys, axis=1) + d_ref[...] * u
    y_ref[...] = _gelu_glu(y, wglu_ref).astype(BF16)


def _s5_step(u, h0r, h0i, bre, bim, a_re, a_im, cre, cim, d, wglu):
    b = u.shape[0]
    return pl.pallas_call(
        _s5_step_kernel,
        out_shape=[jax.ShapeDtypeStruct((b, S5_WIDTH), BF16), jax.ShapeDtypeStruct((b, S5_LANES), F32),
                   jax.ShapeDtypeStruct((b, S5_LANES), F32)],
        compiler_params=_cparams(),
        name="s5_step",
    )(u, h0r, h0i, bre, bim, a_re.reshape(1, S5_LANES), a_im.reshape(1, S5_LANES), cre, cim, d, wglu)


def _mla_prompt_kernel(iq_ref, jk_ref, q_ref, k_ref, v_ref, o_ref, m_ref, acc_ref, *, hps):
    p = pl.program_id(1)
    i, j = iq_ref[p], jk_ref[p]
    tq, tk = q_ref.shape[0], k_ref.shape[0]
    low = lax.broadcasted_iota(jnp.int32, (1, LANES), 1) < MLA_V

    @pl.when(j == 0)
    def _():
        m_ref[...] = jnp.full_like(m_ref, -jnp.inf)
        acc_ref[...] = jnp.zeros_like(acc_ref)

    def scores(hh):
        sl = slice(hh * LANES, (hh + 1) * LANES)
        return _dot_nt(q_ref[:, sl], k_ref[:, sl])

    def values(hh):
        v = v_ref[:, (hh // 2) * LANES:(hh // 2 + 1) * LANES]
        return jnp.where(low, v, jnp.ones_like(v)) if hh % 2 == 0 else jnp.where(low, jnp.ones_like(v), v)

    def block(diagonal):
        ahead = 2
        queue = [scores(hh) for hh in range(min(ahead, hps))]
        pending = []
        for hh in range(hps):
            s = queue.pop(0)
            if hh + ahead < hps:
                queue.append(scores(hh + ahead))
            if diagonal:
                row = lax.broadcasted_iota(jnp.int32, (tq, tk), 0)
                col = lax.broadcasted_iota(jnp.int32, (tq, tk), 1)
                s = jnp.where(col <= row, s, -jnp.inf)
            m_prev = m_ref[hh]
            m_new = jnp.maximum(m_prev, jnp.max(s, axis=-1, keepdims=True))
            pending.append((hh, jnp.exp2(s - m_new).astype(BF16), jnp.exp2(m_prev - m_new)))
            m_ref[hh] = m_new
            for ph, pr, alpha in pending[:-1] if hh + 1 < hps else pending:
                acc_ref[ph] = alpha * acc_ref[ph] + _dot(pr, values(ph))
            pending = pending[-1:] if hh + 1 < hps else []

    @pl.when(j < i)
    def _():
        block(False)

    @pl.when(j == i)
    def _():
        block(True)
        for pair in range(hps // 2):
            a0, a1 = acc_ref[2 * pair], acc_ref[2 * pair + 1]
            o = jnp.where(low, a0 / pltpu.roll(a0, MLA_V, axis=1), a1 / pltpu.roll(a1, MLA_V, axis=1))
            o_ref[:, pair * LANES:(pair + 1) * LANES] = o.astype(BF16)


def _mla_prompt_attn(q, k, v, tile, hps):
    t = q.shape[0]
    nq = t // tile
    pairs = [(i, j) for i in range(nq) for j in range(i + 1)]
    iq = jnp.asarray([p[0] for p in pairs], jnp.int32)
    jk = jnp.asarray([p[1] for p in pairs], jnp.int32)
    grid_spec = pltpu.PrefetchScalarGridSpec(
        num_scalar_prefetch=2,
        grid=(MLA_HEADS // hps, len(pairs)),
        in_specs=[pl.BlockSpec((tile, hps * LANES), lambda h, p, iq, jk: (iq[p], h)),
                  pl.BlockSpec((tile, hps * LANES), lambda h, p, iq, jk: (jk[p], h)),
                  pl.BlockSpec((tile, hps * MLA_V), lambda h, p, iq, jk: (jk[p], h))],
        out_specs=pl.BlockSpec((tile, hps * MLA_V), lambda h, p, iq, jk: (iq[p], h)),
        scratch_shapes=[pltpu.VMEM((hps, tile, 1), F32), pltpu.VMEM((hps, tile, LANES), F32)],
    )
    return pl.pallas_call(
        functools.partial(_mla_prompt_kernel, hps=hps),
        grid_spec=grid_spec,
        out_shape=jax.ShapeDtypeStruct((t, MLA_HEADS * MLA_V), BF16),
        compiler_params=_cparams(("arbitrary", "arbitrary")),
        name="mla_prompt_attn",
    )(iq, jk, q, k, v)


def _qlat_kernel(q_ref, wukt_ref, o_ref):
    for hd in range(MLA_HEADS):
        o_ref[hd] = _dot(q_ref[:, hd * LANES:(hd + 1) * LANES], wukt_ref[hd]).astype(BF16)


def _uv_kernel(o_ref, w_ref, y_ref):
    y_ref[...] = _dot(o_ref[...], w_ref[...]).astype(BF16)


def _mla_sample_kernel(pt_ref, ql_ref, qr_ref, cn_ref, kn_ref, ckv_hbm, krt_hbm, o_ref,
                       cbuf, kbuf, sem, *, npages, sub):
    b = pl.program_id(0)
    nb = pl.num_programs(0)
    slot = lax.rem(b, 2)

    def copies(bb, sl):
        out = []
        for p in range(npages):
            page = pt_ref[bb, p]
            tok = pl.ds(p * PAGE, PAGE)
            out.append(pltpu.make_async_copy(ckv_hbm.at[0, page], cbuf.at[sl, tok], sem.at[sl, 0]))
            out.append(pltpu.make_async_copy(krt_hbm.at[0, page], kbuf.at[sl, :, tok], sem.at[sl, 1]))
        return out

    @pl.when(b == 0)
    def _():
        for cp in copies(b, slot):
            cp.start()

    @pl.when(b + 1 < nb)
    def _():
        for cp in copies(b + 1, 1 - slot):
            cp.start()

    ql = ql_ref[0]
    qr = qr_ref[0]
    cn = cn_ref[0].astype(BF16).astype(F32)
    kn = kn_ref[0].astype(BF16).astype(F32)
    s_new = (jnp.sum(ql.astype(F32) * cn, axis=-1, keepdims=True)
             + jnp.sum(qr.astype(F32) * kn, axis=-1, keepdims=True))

    for cp in copies(b, slot):
        cp.wait()

    parts = []
    for sc in range(npages * PAGE // sub):
        tok = pl.ds(sc * sub, sub)
        cpg = cbuf[slot, tok, :].astype(BF16)
        krp = kbuf[slot, :, tok].astype(BF16)
        s = _dot_nt(ql, cpg) + _dot(qr, krp)
        m = jnp.max(s, axis=-1, keepdims=True)
        pr = jnp.exp(s - m)
        parts.append((m, jnp.sum(pr, axis=-1, keepdims=True), _dot(pr.astype(BF16), cpg)))
    m_all = s_new
    for m, _, _ in parts:
        m_all = jnp.maximum(m_all, m)
    w_new = jnp.exp(s_new - m_all)
    den = w_new
    acc = w_new * cn
    for m, l, o in parts:
        w = jnp.exp(m - m_all)
        den = den + w * l
        acc = acc + w * o
    o_ref[0] = (acc / den).astype(BF16)


def _mla_sample_attn(q, c_new, kr_new, cache_ckv, cache_krope, page_table, wukt, wuv_blk, sub):
    b = q.shape[0]
    npages = page_table.shape[1]
    qlat = pl.pallas_call(
        _qlat_kernel,
        out_shape=jax.ShapeDtypeStruct((MLA_HEADS, b, KV_LORA), BF16),
        name="mla_qlat",
    )(q, wukt)
    ql = qlat.transpose(1, 0, 2)
    qr = q.reshape(b, MLA_HEADS, LANES)[:, :, MLA_NOPE:MLA_QK]
    krt = jnp.swapaxes(cache_krope, 2, 3)
    n = npages * PAGE
    grid_spec = pltpu.PrefetchScalarGridSpec(
        num_scalar_prefetch=1,
        grid=(b,),
        in_specs=[pl.BlockSpec((1, MLA_HEADS, KV_LORA), lambda i, pt: (i, 0, 0)),
                  pl.BlockSpec((1, MLA_HEADS, MLA_ROPE), lambda i, pt: (i, 0, 0)),
                  pl.BlockSpec((1, 1, KV_LORA), lambda i, pt: (i, 0, 0)),
                  pl.BlockSpec((1, 1, MLA_ROPE), lambda i, pt: (i, 0, 0)),
                  pl.BlockSpec(memory_space=pl.ANY),
                  pl.BlockSpec(memory_space=pl.ANY)],
        out_specs=pl.BlockSpec((1, MLA_HEADS, KV_LORA), lambda i, pt: (i, 0, 0)),
        scratch_shapes=[pltpu.VMEM((2, n, KV_LORA), F32), pltpu.VMEM((2, MLA_ROPE, n), F32),
                        pltpu.SemaphoreType.DMA((2, 2))],
    )
    o_lat = pl.pallas_call(
        functools.partial(_mla_sample_kernel, npages=npages, sub=sub),
        grid_spec=grid_spec,
        out_shape=jax.ShapeDtypeStruct((b, MLA_HEADS, KV_LORA), BF16),
        compiler_params=_cparams(("arbitrary",)),
        name="mla_sample_attn",
    )(page_table, ql, qr, c_new.reshape(b, 1, KV_LORA), kr_new.reshape(b, 1, MLA_ROPE), cache_ckv, krt)
    return pl.pallas_call(
        _uv_kernel,
        out_shape=jax.ShapeDtypeStruct((b, MLA_HEADS * MLA_V), BF16),
        name="mla_sample_uv",
    )(o_lat.reshape(b, MLA_HEADS * KV_LORA), wuv_blk)


def _router_gates(h2, wr_ref, br_ref):
    h_hi = h2.astype(BF16)
    h_lo = (h2 - h_hi.astype(F32)).astype(BF16)
    logits = _dot(h_hi, wr_ref[0]) + (_dot(h_lo, wr_ref[0]) + _dot(h_hi, wr_ref[1]))
    scores = jax.nn.sigmoid(logits)
    sel = scores + br_ref[...]
    lane = lax.broadcasted_iota(jnp.int32, (1, LANES), 1)
    pos = lane % GROUP_SIZE
    others, wrapped = [], []
    for r in range(1, GROUP_SIZE):
        wrap = pos + r >= GROUP_SIZE
        fwd = pltpu.roll(sel, LANES - r, axis=1)
        bwd = pltpu.roll(sel, GROUP_SIZE - r, axis=1)
        others.append(jnp.where(wrap, bwd, fwd))
        wrapped.append(wrap)
    a, b, c, d = sel, others[0], others[1], others[2]
    hi1, lo1 = jnp.maximum(a, b), jnp.minimum(a, b)
    hi2, lo2 = jnp.maximum(c, d), jnp.minimum(c, d)
    gscore = jnp.maximum(hi1, hi2) + jnp.maximum(jnp.minimum(hi1, hi2), jnp.maximum(lo1, lo2))
    real = lane < N_EXPERTS
    gscore = jnp.where(real, gscore, -jnp.inf)
    gmax = jnp.max(gscore, axis=-1, keepdims=True)
    gidx = (lane // GROUP_SIZE).astype(F32)
    chosen = jnp.min(jnp.where(gscore == gmax, gidx, float(LANES)), axis=-1, keepdims=True)
    rank = jnp.zeros(sel.shape, F32)
    for o, wrap in zip(others, wrapped):
        rank = rank + jnp.where(wrap, jnp.where(o >= sel, 1.0, 0.0), jnp.where(o > sel, 1.0, 0.0))
    w = jnp.where(gidx == chosen, jnp.where(rank < 2.0, scores, 0.0), 0.0)
    return w / jnp.sum(w, axis=-1, keepdims=True)


def _post_kernel(*refs, n_mix):
    x_ref = refs[0]
    ys = refs[1:1 + n_mix]
    ws = refs[1 + n_mix:1 + 2 * n_mix]
    gt_ref, sh_ref, sc_ref, g_ref, wr_ref, br_ref, x1_ref, h2_ref, gate_ref = refs[1 + 2 * n_mix:]
    mix = None
    for y_ref, w_ref in zip(ys, ws):
        t = _dot(y_ref[...], w_ref[...])
        mix = t if mix is None else mix + t
    x1 = x_ref[...] + gt_ref[...] * mix
    x1_ref[...] = x1
    h2 = _rms(x1, g_ref[...]) * (1.0 + sc_ref[...]) + sh_ref[...]
    h2_ref[...] = h2.astype(BF16)
    gate_ref[...] = _router_gates(h2, wr_ref, br_ref)


def _post(x, ys, ws, gt, sh, sc, g, wr, br, tile):
    r = x.shape[0]
    per_row = gt.shape[0] != 1
    n_mix = len(ys)
    in_specs = ([pl.BlockSpec((tile, D_MODEL), lambda i: (i, 0))]
                + [pl.BlockSpec((tile, y.shape[1]), lambda i: (i, 0)) for y in ys]
                + [_full(w.shape) for w in ws]
                + [_rows(tile, D_MODEL, per_row)] * 3
                + [_full(g.shape), _full(wr.shape), _full(br.shape)])
    widths = [(D_MODEL, F32), (D_MODEL, BF16), (LANES, F32)]
    return pl.pallas_call(
        functools.partial(_post_kernel, n_mix=n_mix),
        grid=(r // tile,),
        in_specs=in_specs,
        out_specs=[pl.BlockSpec((tile, w), lambda i: (i, 0)) for w, _ in widths],
        out_shape=[jax.ShapeDtypeStruct((r, w), dt) for w, dt in widths],
        compiler_params=_cparams(("arbitrary",)),
        name="post_mixer",
    )(x, *ys, *ws, gt, sh, sc, g, wr, br)


N_GROUPS = N_EXPERTS // GROUP_SIZE
MOE_CHUNK = 128
_TN = (((0,), (0,)), ((), ()))


def _moe_kernel(h_ref, gate_ref, x1_ref, gt_ref, tri_ref, wg_hbm, wu_hbm, wd_hbm, o_ref,
                wg_v, wu_v, wd_v, sem, hs_ref, ys_ref, *, layer):
    @pl.when(pl.program_id(0) == 0)
    def _():
        copies = [pltpu.make_async_copy(src.at[layer], dst, sem.at[k])
                  for k, (src, dst) in enumerate(((wg_hbm, wg_v), (wu_hbm, wu_v), (wd_hbm, wd_v)))]
        for cp in copies:
            cp.start()
        for cp in copies:
            cp.wait()

    nrows = hs_ref.shape[0]
    gate = gate_ref[...]
    lane = lax.broadcasted_iota(jnp.int32, (1, LANES), 1)
    er = lax.broadcasted_iota(jnp.int32, (LANES, LANES), 0)
    ec = lax.broadcasted_iota(jnp.int32, (LANES, LANES), 1)
    member = jnp.where((er // GROUP_SIZE == ec) & (er < N_EXPERTS), 1.0, 0.0).astype(BF16)
    before = jnp.where(er < ec, 1.0, 0.0).astype(BF16)
    g1 = gate.astype(BF16)
    ind = jnp.where(_dot(g1, member) > 0.5, 1.0, 0.0)
    rank = _dot(tri_ref[...], ind.astype(BF16))
    count = jnp.sum(ind, axis=0, keepdims=True).astype(jnp.int32)
    shift = MOE_CHUNK.bit_length() - 1
    region = (((count + (MOE_CHUNK - 1)) >> shift) << shift).astype(F32)
    start = _dot(jnp.broadcast_to(region, (SUBLANES, LANES)).astype(BF16), before)[0:1]
    pos = jnp.sum(ind * (start + rank), axis=-1, keepdims=True)
    slot = lax.broadcasted_iota(jnp.int32, (1, nrows), 1).astype(F32)
    place = jnp.where(pos == slot, 1.0, 0.0).astype(BF16)

    r1 = gate - g1.astype(F32)
    g2 = r1.astype(BF16)
    g3 = (r1 - g2.astype(F32)).astype(BF16)
    moved = lax.dot_general(place, jnp.concatenate([h_ref[...], g1, g2, g3], axis=1), _TN,
                            preferred_element_type=F32)
    hs_ref[...] = moved[:, :D_MODEL].astype(BF16)
    gs = (moved[:, D_MODEL:D_MODEL + LANES] + moved[:, D_MODEL + LANES:D_MODEL + 2 * LANES]
          + moved[:, D_MODEL + 2 * LANES:])

    ends = [jnp.sum(jnp.where(lane == g, start + region, 0.0)).astype(jnp.int32) for g in range(N_GROUPS)]
    for c in range(nrows // MOE_CHUNK):
        first = c * MOE_CHUNK
        rows = pl.ds(first, MOE_CHUNK)
        gs_c = gs[first:first + MOE_CHUNK]

        @pl.when(first < ends[-1])
        def _():
            grp = sum((first >= e).astype(jnp.int32) for e in ends[:-1])
            hs_c = hs_ref[rows, :]
            acc = None
            for k in range(GROUP_SIZE):
                e = grp * GROUP_SIZE + k
                a = _dot(hs_c, wg_v[e])
                u = _dot(hs_c, wu_v[e])
                gcol = jnp.sum(jnp.where(lane == e, gs_c, 0.0), axis=-1, keepdims=True)
                t = _dot(((a * jax.nn.sigmoid(a)) * u * gcol).astype(BF16), wd_v[e])
                acc = t if acc is None else acc + t
            ys_ref[rows, :] = acc.astype(BF16)

        @pl.when(first >= ends[-1])
        def _():
            ys_ref[rows, :] = jnp.zeros((MOE_CHUNK, D_MODEL), BF16)

    o_ref[...] = x1_ref[...] + gt_ref[...] * _dot(place, ys_ref[...])


def _moe(h2, gate, x1, gt, wg, wu, wd, layer, tile):
    r = h2.shape[0]
    per_row = gt.shape[0] != 1
    nrows = (-(-tile // MOE_CHUNK) + N_GROUPS) * MOE_CHUNK
    idx = jnp.arange(tile)
    tri = (idx[None, :] < idx[:, None]).astype(BF16)
    return pl.pallas_call(
        functools.partial(_moe_kernel, layer=layer),
        grid=(r // tile,),
        in_specs=[pl.BlockSpec((tile, D_MODEL), lambda i: (i, 0)),
                  pl.BlockSpec((tile, LANES), lambda i: (i, 0)),
                  pl.BlockSpec((tile, D_MODEL), lambda i: (i, 0)),
                  _rows(tile, D_MODEL, per_row),
                  _full(tri.shape),
                  pl.BlockSpec(memory_space=pl.ANY),
                  pl.BlockSpec(memory_space=pl.ANY),
                  pl.BlockSpec(memory_space=pl.ANY)],
        out_specs=pl.BlockSpec((tile, D_MODEL), lambda i: (i, 0)),
        out_shape=jax.ShapeDtypeStruct((r, D_MODEL), F32),
        scratch_shapes=[pltpu.VMEM((N_EXPERTS, D_MODEL, D_FF), BF16),
                        pltpu.VMEM((N_EXPERTS, D_MODEL, D_FF), BF16),
                        pltpu.VMEM((N_EXPERTS, D_FF, D_MODEL), BF16),
                        pltpu.SemaphoreType.DMA((3,)),
                        pltpu.VMEM((nrows, D_MODEL), BF16),
                        pltpu.VMEM((nrows, D_MODEL), BF16)],
        compiler_params=_cparams(("arbitrary",)),
        name="moe",
    )(h2, gate, x1, gt, tri, wg, wu, wd)


def _front_odd_kernel(x_ref, sh_ref, sc_ref, g_ref, win_ref, gq_ref, gk_ref, rc_ref, rs_ref, rot_ref,
                      q_ref, k_ref, v_ref, ku_ref, vu_ref):
    x = x_ref[...]
    h = _rms(x, g_ref[...]) * (1.0 + sc_ref[...]) + sh_ref[...]
    z = _dot(h.astype(BF16), win_ref[...])
    rc, rs, rot = rc_ref[...], rs_ref[...], rot_ref[...]
    scale = SWA_HD ** -0.5
    for hd in range(SWA_HEADS):
        sl = slice(hd * LANES, (hd + 1) * LANES)
        qn = _rope(_rms(z[:, sl], gq_ref[...], SWA_HD), rot, rc, rs, False)
        q_ref[:, sl] = (qn * scale).astype(BF16)
    ks, vs = [], []
    for kh in range(SWA_KV):
        sl = slice(kh * LANES, (kh + 1) * LANES)
        zk = z[:, (SWA_HEADS + kh) * LANES:(SWA_HEADS + kh + 1) * LANES]
        kn = _rope(_rms(zk, gk_ref[...], SWA_HD), rot, rc, rs, True)
        vv = z[:, (SWA_HEADS + SWA_KV + kh) * LANES:(SWA_HEADS + SWA_KV + kh + 1) * LANES]
        k_ref[:, sl] = kn.astype(BF16)
        v_ref[:, sl] = vv.astype(BF16)
        ks.append(kn)
        vs.append(vv)
    for j in range(SWA_KV // 2):
        sl = slice(j * LANES, (j + 1) * LANES)
        ku_ref[:, sl] = ks[2 * j] + pltpu.roll(ks[2 * j + 1], SWA_HD, axis=1)
        vu_ref[:, sl] = vs[2 * j] + pltpu.roll(vs[2 * j + 1], SWA_HD, axis=1)


def _front_odd(x, sh, sc, g, win, gq, gk, tabs, tile):
    r = x.shape[0]
    per_row_mod = sh.shape[0] != 1
    per_row_tab = tabs[0].shape[0] != 1
    in_specs = [pl.BlockSpec((tile, D_MODEL), lambda i: (i, 0)),
                _rows(tile, D_MODEL, per_row_mod), _rows(tile, D_MODEL, per_row_mod),
                _full(g.shape), _full(win.shape), _full(gq.shape), _full(gk.shape),
                _rows(tile, LANES, per_row_tab), _rows(tile, LANES, per_row_tab), _full(tabs[2].shape)]
    widths = [(SWA_HEADS * LANES, BF16), (SWA_KV * LANES, BF16), (SWA_KV * LANES, BF16),
              (SWA_KV * SWA_HD, F32), (SWA_KV * SWA_HD, F32)]
    return pl.pallas_call(
        _front_odd_kernel,
        grid=(r // tile,),
        in_specs=in_specs,
        out_specs=[pl.BlockSpec((tile, w), lambda i: (i, 0)) for w, _ in widths],
        out_shape=[jax.ShapeDtypeStruct((r, w), dt) for w, dt in widths],
        compiler_params=_cparams(("arbitrary",)),
        name="front_odd",
    )(x, sh, sc, g, win, gq, gk, *tabs)


def _swa_prompt_kernel(sink_ref, q_ref, kp_ref, kc_ref, vp_ref, vc_ref, o_ref):
    n = pl.program_id(0)
    w = WINDOW
    qi = lax.broadcasted_iota(jnp.int32, (w, 2 * w), 0) + w
    kj = lax.broadcasted_iota(jnp.int32, (w, 2 * w), 1)
    valid = (kj <= qi) & (qi - kj < w) & (n * w - w + kj >= 0)
    for kh in range(SWA_KV):
        sl = slice(kh * LANES, (kh + 1) * LANES)
        kk = jnp.concatenate([kp_ref[:, sl], kc_ref[:, sl]], axis=0)
        vv = jnp.concatenate([vp_ref[:, sl], vc_ref[:, sl]], axis=0)
        for gi in range(SWA_GROUP):
            hd = kh * SWA_GROUP + gi
            hsl = slice(hd * LANES, (hd + 1) * LANES)
            s = jnp.where(valid, _dot_nt(q_ref[:, hsl], kk), -jnp.inf)
            sink = sink_ref[hd]
            m = jnp.maximum(jnp.max(s, axis=-1, keepdims=True), sink)
            pr = jnp.exp(s - m)
            den = jnp.sum(pr, axis=-1, keepdims=True) + jnp.exp(sink - m)
            o_ref[:, hsl] = (_dot(pr.astype(BF16), vv) / den).astype(BF16)


def _swa_prompt_attn(q, k, v, sinks):
    t = q.shape[0]
    w = WINDOW
    prev = lambda n, s: (jnp.maximum(n - 1, 0), 0)
    cur = lambda n, s: (n, 0)
    grid_spec = pltpu.PrefetchScalarGridSpec(
        num_scalar_prefetch=1,
        grid=(t // w,),
        in_specs=[pl.BlockSpec((w, SWA_HEADS * LANES), cur),
                  pl.BlockSpec((w, SWA_KV * LANES), prev), pl.BlockSpec((w, SWA_KV * LANES), cur),
                  pl.BlockSpec((w, SWA_KV * LANES), prev), pl.BlockSpec((w, SWA_KV * LANES), cur)],
        out_specs=pl.BlockSpec((w, SWA_HEADS * LANES), cur),
    )
    return pl.pallas_call(
        _swa_prompt_kernel,
        grid_spec=grid_spec,
        out_shape=jax.ShapeDtypeStruct((t, SWA_HEADS * LANES), BF16),
        compiler_params=_cparams(("arbitrary",)),
        name="swa_prompt_attn",
    )(sinks, q, k, k, v, v)


def _swa_sample_kernel(q_ref, kc_ref, vc_ref, kn_ref, vn_ref, sink_ref, o_ref, wk_ref, wv_ref, *, sb):
    w = WINDOW
    width = SWA_KV * SWA_HD
    rowgrp = lax.broadcasted_iota(jnp.int32, (SWA_HEADS, 1), 0) // SWA_GROUP
    col = lax.broadcasted_iota(jnp.int32, (1, w), 1)
    low = lax.broadcasted_iota(jnp.int32, (1, LANES), 1) < SWA_HD
    sink = sink_ref[...]
    for b in range(sb):
        qf = q_ref[b].astype(F32)
        a = jnp.concatenate([qf, jnp.zeros_like(qf)], axis=1)
        qx = a
        for kh in range(1, SWA_KV):
            qx = jnp.where(rowgrp == kh, pltpu.roll(a, kh * SWA_HD, axis=1), qx)
        kc = kc_ref[b]
        vc = vc_ref[b]
        kn = kn_ref[b]
        vn = vn_ref[b]
        s = _dot_nt(qx.astype(BF16), kc.astype(BF16))
        s = jnp.where(col >= 1, s, -jnp.inf)
        s_new = jnp.sum(qx.astype(BF16).astype(F32) * kn.astype(BF16).astype(F32), axis=-1, keepdims=True)
        m = jnp.maximum(jnp.maximum(jnp.max(s, axis=-1, keepdims=True), s_new), sink)
        pr = jnp.exp(s - m)
        pn = jnp.exp(s_new - m)
        den = jnp.sum(pr, axis=-1, keepdims=True) + pn + jnp.exp(sink - m)
        o = (_dot(pr.astype(BF16), vc.astype(BF16)) + pn * vn.astype(BF16).astype(F32)) / den
        ox = o
        for kh in range(1, SWA_KV):
            ox = jnp.where(rowgrp == kh, pltpu.roll(o, width - kh * SWA_HD, axis=1), ox)
        o_ref[b] = jnp.where(low, ox[:, :LANES], 0.0).astype(BF16)
        wk_ref[b, pl.ds(0, w - 1), :] = kc_ref[b, pl.ds(1, w - 1), :]
        wk_ref[b, pl.ds(w - 1, 1), :] = kn
        wv_ref[b, pl.ds(0, w - 1), :] = vc_ref[b, pl.ds(1, w - 1), :]
        wv_ref[b, pl.ds(w - 1, 1), :] = vn


def _swa_sample_attn(q, kc, vc, kn, vn, sinks, sb):
    b = q.shape[0]
    w = WINDOW
    width = SWA_KV * SWA_HD
    blk3 = lambda s1, s2: pl.BlockSpec((sb, s1, s2), lambda i: (i, 0, 0))
    return pl.pallas_call(
        functools.partial(_swa_sample_kernel, sb=sb),
        grid=(b // sb,),
        in_specs=[blk3(SWA_HEADS, LANES), blk3(w, width), blk3(w, width), blk3(1, width), blk3(1, width),
                  _full((SWA_HEADS, 1))],
        out_specs=[blk3(SWA_HEADS, LANES), blk3(w, width), blk3(w, width)],
        out_shape=[jax.ShapeDtypeStruct((b, SWA_HEADS, LANES), BF16),
                   jax.ShapeDtypeStruct((b, w, width), F32), jax.ShapeDtypeStruct((b, w, width), F32)],
        compiler_params=_cparams(("arbitrary",)),
        name="swa_sample_attn",
    )(q.reshape(b, SWA_HEADS, LANES), kc, vc, kn.reshape(b, 1, width), vn.reshape(b, 1, width),
      sinks.reshape(SWA_HEADS, 1))


def _rope_cos_sin(pos):
    inv = jnp.concatenate([ROPE_THETA ** (-jnp.arange(h, dtype=F32) / h) for h in (MLA_ROPE // 2, SWA_ROT // 2)])
    ang = pos.astype(F32)[:, None] * inv[None, :]
    cos, sin = jnp.cos(ang), jnp.sin(ang)
    cut = MLA_ROPE // 2
    return (cos[:, :cut], sin[:, :cut]), (cos[:, cut:], sin[:, cut:])


def _rope_tables(cos_sin, offset):
    cos, sin = cos_sin
    n, half = cos.shape
    rest = LANES - offset - 2 * half
    rc = jnp.concatenate([jnp.ones((n, offset), F32), cos, cos, jnp.ones((n, rest), F32)], axis=1)
    rs = jnp.concatenate([jnp.zeros((n, offset), F32), sin, sin, jnp.zeros((n, rest), F32)], axis=1)
    lane = jnp.arange(LANES)
    first = (lane >= offset) & (lane < offset + half)
    second = (lane >= offset + half) & (lane < offset + 2 * half)
    src = lane[:, None]
    rot = (jnp.where(first[None, :] & (src == lane[None, :] + half), -1.0, 0.0)
           + jnp.where(second[None, :] & (src == lane[None, :] - half), 1.0, 0.0))
    return rc, rs, rot.astype(BF16)


def _pad_lanes(v, offset=0):
    return jnp.zeros((1, LANES), F32).at[0, offset:offset + v.shape[0]].set(v)


def _even_weights(w_in, g_qa, w_uq, g_qh, g_kva, g_kr, w_uk, w_uv):
    o3 = S5_WIDTH + Q_LORA + KV_LORA
    win = jnp.zeros((D_MODEL, 1280), F32).at[:, :o3].set(w_in[:, :o3])
    win = win.at[:, o3 + MLA_NOPE:o3 + MLA_QK].set(w_in[:, o3:])
    wuq = jnp.zeros((MLA_HEADS, Q_LORA, LANES), F32).at[:, :, :MLA_QK].set(w_uq.transpose(1, 0, 2))
    wuk = jnp.zeros((KV_LORA, MLA_HEADS, LANES), F32).at[:, :, :MLA_NOPE].set(w_uk)
    front = (win.astype(BF16), g_qa.reshape(1, Q_LORA), wuq.astype(BF16), _pad_lanes(g_qh),
             g_kva.reshape(1, KV_LORA), _pad_lanes(g_kr, MLA_NOPE),
             wuk.reshape(KV_LORA, MLA_HEADS * LANES).astype(BF16),
             w_uv.reshape(KV_LORA, MLA_HEADS * MLA_V).astype(BF16))
    wukt = jnp.zeros((MLA_HEADS, LANES, KV_LORA), F32).at[:, :MLA_NOPE, :].set(w_uk.transpose(1, 2, 0))
    eye = jnp.eye(MLA_HEADS, dtype=F32)
    wuv_blk = jnp.einsum('chd,hg->hcgd', w_uv, eye).reshape(MLA_HEADS * KV_LORA, MLA_HEADS * MLA_V)
    return front, wukt.astype(BF16), wuv_blk.astype(BF16)


def _odd_weights(w_in, w_out):
    nq, nk = SWA_HEADS * SWA_HD, SWA_KV * SWA_HD
    nh = SWA_HEADS + 2 * SWA_KV
    win = jnp.zeros((D_MODEL, nh, LANES), F32).at[:, :, :SWA_HD].set(w_in.reshape(D_MODEL, nh, SWA_HD))
    wout = jnp.zeros((SWA_HEADS, LANES, D_MODEL), F32).at[:, :SWA_HD, :].set(w_out.reshape(SWA_HEADS, SWA_HD, D_MODEL))
    del nq, nk
    return win.reshape(D_MODEL, nh * LANES).astype(BF16), wout.reshape(SWA_HEADS * LANES, D_MODEL).astype(BF16)


def kernel(x_prompt, x_sample, c_prompt, c_sample, cache_ckv, cache_krope, page_table, state_s5_re, state_s5_im, cache_win_k, cache_win_v, w_mod, b_mod, g_norm_mix, g_norm_ffn, w_in_even, w_out_even, s5_lam_re, s5_lam_im, s5_log_step, s5_b_re, s5_b_im, s5_c_re, s5_c_im, s5_d, s5_w_glu, mla_g_qa, mla_w_uq, mla_g_qh, mla_g_kva, mla_g_kr, mla_w_uk, mla_w_uv, w_in_odd, w_out_odd, swa_g_q, swa_g_k, swa_sinks, w_router, b_router, moe_w_gate, moe_w_up, moe_w_down):
    t = x_prompt.shape[1]
    nb = x_sample.shape[0]
    past_len = page_table.shape[1] * PAGE
    row_tile = 512
    s5_seg = 64
    attn_tile, attn_heads = 1024, MLA_HEADS
    mla_scale = MLA_QK ** -0.5
    moe_tile = 512
    sample_sub = past_len
    swa_seq_block = 8

    n_c = 1 + nb
    rp = -(-n_c // SUBLANES) * SUBLANES
    c_all = jnp.concatenate([c_prompt, c_sample, jnp.zeros((rp - n_c, D_MODEL), F32)], axis=0)
    mod = _modulation(c_all, w_mod, b_mod)

    def mods(layer, sample):
        rows = mod[layer, 1:1 + nb] if sample else mod[layer, 0:1]
        return [rows[:, k * D_MODEL:(k + 1) * D_MODEL] for k in range(6)]

    wr = jnp.zeros((D_MODEL, LANES), F32).at[:, :N_EXPERTS].set(w_router)
    wr_hi = wr.astype(BF16)
    wr = jnp.stack([wr_hi, (wr - wr_hi.astype(F32)).astype(BF16)])
    br = _pad_lanes(b_router)
    wg, wu, wd = moe_w_gate.astype(BF16), moe_w_up.astype(BF16), moe_w_down.astype(BF16)

    xp = x_prompt.reshape(t, D_MODEL)
    xs = x_sample.reshape(nb, D_MODEL)
    mla_cs_p, swa_cs_p = _rope_cos_sin(jnp.arange(t))
    mla_cs_s, swa_cs_s = _rope_cos_sin(jnp.full((1,), past_len))
    mla_tabs_p = _rope_tables(mla_cs_p, MLA_NOPE)
    mla_tabs_s = _rope_tables(mla_cs_s, MLA_NOPE)
    swa_tabs_p = _rope_tables(swa_cs_p, 0)
    swa_tabs_s = _rope_tables(swa_cs_s, 0)

    front_w, wukt, wuv_blk = _even_weights(w_in_even[0], mla_g_qa[0], mla_w_uq[0], mla_g_qh[0], mla_g_kva[0],
                                           mla_g_kr[0], mla_w_uk[0], mla_w_uv[0])
    a_re, a_im, bb_re, bb_im, al_re, al_im = _s5_discretize(s5_lam_re[0], s5_lam_im[0], s5_log_step[0],
                                                            s5_b_re[0], s5_b_im[0], s5_seg)
    bre, bim, cre, cim = _s5_layouts(bb_re, bb_im, s5_c_re[0], s5_c_im[0])
    d_skip = s5_d[0].reshape(1, S5_WIDTH)
    wglu = s5_w_glu[0].astype(BF16)
    wo_ssm = w_out_even[0, :S5_WIDTH].astype(BF16)
    wo_att = w_out_even[0, S5_WIDTH:].astype(BF16)
    g_mix0 = g_norm_mix[0].reshape(1, D_MODEL)
    g_ffn0 = g_norm_ffn[0].reshape(1, D_MODEL)

    sh1, sc1, gt1, sh2, sc2, gt2 = mods(0, False)
    u, q, k, v, ckv_p, kr_p = _front_even(xp, sh1, sc1, g_mix0, front_w, mla_tabs_p,
                                          row_tile, mla_scale * math.log2(math.e))
    y_ssm, st = _s5_scan(u, bre, bim, a_re, a_im, al_re, al_im, cre, cim, d_skip, wglu, s5_seg)
    sr_p = st[:S5_CHUNKS].reshape(1, 1, S5_GROUPS, S5_STATE)
    si_p = st[S5_CHUNKS:].reshape(1, 1, S5_GROUPS, S5_STATE)
    y_att = _mla_prompt_attn(q, k, v, attn_tile, attn_heads)
    x1, h2, gate = _post(xp, [y_ssm, y_att], [wo_ssm, wo_att], gt1, sh2, sc2, g_ffn0, wr, br, row_tile)
    xp = _moe(h2, gate, x1, gt2, wg, wu, wd, 0, moe_tile)

    sh1, sc1, gt1, sh2, sc2, gt2 = mods(0, True)
    u, q, _, _, ckv_s, kr_s = _front_even(xs, sh1, sc1, g_mix0, front_w, mla_tabs_s, nb, mla_scale)
    y_ssm, sr_s, si_s = _s5_step(u, state_s5_re[0].reshape(nb, S5_LANES), state_s5_im[0].reshape(nb, S5_LANES),
                                 bre, bim, a_re, a_im, cre, cim, d_skip, wglu)
    y_att = _mla_sample_attn(q, ckv_s, kr_s, cache_ckv, cache_krope, page_table, wukt, wuv_blk, sample_sub)
    x1, h2, gate = _post(xs, [y_ssm, y_att], [wo_ssm, wo_att], gt1, sh2, sc2, g_ffn0, wr, br, nb)
    xs = _moe(h2, gate, x1, gt2, wg, wu, wd, 0, nb)

    win_odd, wout_odd = _odd_weights(w_in_odd[0], w_out_odd[0])
    gq, gk = _pad_lanes(swa_g_q[0]), _pad_lanes(swa_g_k[0])
    g_mix1 = g_norm_mix[1].reshape(1, D_MODEL)
    g_ffn1 = g_norm_ffn[1].reshape(1, D_MODEL)
    sinks = swa_sinks[0]

    sh1, sc1, gt1, sh2, sc2, gt2 = mods(1, False)
    q, k, v, ku, vu = _front_odd(xp, sh1, sc1, g_mix1, win_odd, gq, gk, swa_tabs_p, row_tile)
    o = _swa_prompt_attn(q, k, v, sinks)
    wk_p = ku[t - WINDOW:].reshape(1, 1, WINDOW, SWA_KV, SWA_HD)
    wv_p = vu[t - WINDOW:].reshape(1, 1, WINDOW, SWA_KV, SWA_HD)
    x1, h2, gate = _post(xp, [o], [wout_odd], gt1, sh2, sc2, g_ffn1, wr, br, row_tile)
    xp = _moe(h2, gate, x1, gt2, wg, wu, wd, 1, moe_tile)

    sh1, sc1, gt1, sh2, sc2, gt2 = mods(1, True)
    q, _, _, ku, vu = _front_odd(xs, sh1, sc1, g_mix1, win_odd, gq, gk, swa_tabs_s, nb)
    width = SWA_KV * SWA_HD
    o, wk_s, wv_s = _swa_sample_attn(q, cache_win_k[0].reshape(nb, WINDOW, width),
                                     cache_win_v[0].reshape(nb, WINDOW, width), ku, vu, sinks, swa_seq_block)
    x1, h2, gate = _post(xs, [o.reshape(nb, SWA_HEADS * LANES)], [wout_odd], gt1, sh2, sc2, g_ffn1, wr, br, nb)
    xs = _moe(h2, gate, x1, gt2, wg, wu, wd, 1, nb)

    return (xp.reshape(1, t, D_MODEL), xs.reshape(nb, 1, D_MODEL),
            ckv_p.reshape(1, 1, t, KV_LORA), kr_p.reshape(1, 1, t, MLA_ROPE), sr_p, si_p, wk_p, wv_p,
            ckv_s.reshape(1, nb, 1, KV_LORA), kr_s.reshape(1, nb, 1, MLA_ROPE),
            sr_s.reshape(1, nb, S5_GROUPS, S5_STATE), si_s.reshape(1, nb, S5_GROUPS, S5_STATE),
            wk_s.reshape(1, nb, WINDOW, SWA_KV, SWA_HD), wv_s.reshape(1, nb, WINDOW, SWA_KV, SWA_HD))
```

```python
import functools
import math

import jax
import jax.numpy as jnp
from jax import lax
from jax.experimental import pallas as pl
from jax.experimental.pallas import tpu as pltpu

F32 = jnp.float32
BF16 = jnp.bfloat16
EPS = 1e-6
ROPE_THETA = 500000.0
LANES = 128
SUBLANES = 8
VMEM_LIMIT = 56 * 1024 * 1024

D_MODEL = 1024
PAGE = 128
S5_WIDTH = 512
S5_GROUP = 16
S5_GROUPS = 32
S5_STATE = 64
S5_LANES = S5_GROUPS * S5_STATE
S5_CHUNKS = S5_LANES // LANES
MLA_HEADS = 8
MLA_NOPE = 64
MLA_ROPE = 32
MLA_QK = 96
MLA_V = 64
Q_LORA = 384
KV_LORA = 256
SWA_HEADS = 16
SWA_KV = 4
SWA_GROUP = 4
SWA_HD = 64
SWA_ROT = 16
WINDOW = 128
N_EXPERTS = 16
GROUP_SIZE = 4
D_FF = 256

_NT = (((1,), (1,)), ((), ()))


def _dot(a, b):
    return jnp.dot(a, b, preferred_element_type=F32)


def _dot_nt(a, b):
    return lax.dot_general(a, b, _NT, preferred_element_type=F32)


def _rms(x, g, n=None):
    n = x.shape[-1] if n is None else n
    ss = jnp.sum(x * x, axis=-1, keepdims=True) * (1.0 / n)
    return x * lax.rsqrt(ss + EPS) * g


def _rope(x, rot, rc, rs, exact):
    hi = x.astype(BF16)
    partner = _dot(hi, rot)
    if exact:
        partner = partner + _dot((x - hi.astype(F32)).astype(BF16), rot)
    return x * rc + partner * rs


def _row_blocks(rows):
    n = 2 if rows % (4 * SUBLANES) == 0 else 1
    return [slice(b * rows // n, (b + 1) * rows // n) for b in range(n)]


def _cparams(sem=None, vmem=VMEM_LIMIT):
    return pltpu.CompilerParams(dimension_semantics=sem, vmem_limit_bytes=vmem)


def _full(shape):
    nd = len(shape)
    return pl.BlockSpec(shape, lambda *_: (0,) * nd)


def _rows(tile, width, per_row):
    if per_row:
        return pl.BlockSpec((tile, width), lambda i: (i, 0))
    return pl.BlockSpec((1, width), lambda i: (0, 0))


def _mod_kernel(c_ref, w_ref, b_ref, o_ref):
    c = c_ref[...]
    s = (c * jax.nn.sigmoid(c)).astype(BF16)
    o_ref[0] = _dot(s, w_ref[0].astype(BF16)) + b_ref[0]


def _modulation(c_all, w_mod, b_mod):
    depth, _, n = w_mod.shape
    rp = c_all.shape[0]
    tn = 1536
    return pl.pallas_call(
        _mod_kernel,
        grid=(depth, n // tn),
        in_specs=[pl.BlockSpec((rp, D_MODEL), lambda l, j: (0, 0)),
                  pl.BlockSpec((1, D_MODEL, tn), lambda l, j: (l, 0, j)),
                  pl.BlockSpec((1, 1, tn), lambda l, j: (l, 0, j))],
        out_specs=pl.BlockSpec((1, rp, tn), lambda l, j: (l, 0, j)),
        out_shape=jax.ShapeDtypeStruct((depth, rp, n), F32),
        compiler_params=_cparams(("arbitrary", "arbitrary")),
        name="modulation",
    )(c_all, w_mod, b_mod.reshape(depth, 1, n))


def _front_even_kernel(x_ref, sh_ref, sc_ref, g_ref, win_ref, gqa_ref, wuq_ref, gqh_ref, gkva_ref,
                       gkr_ref, wuk_ref, wuv_ref, rc_ref, rs_ref, rot_ref,
                       u_ref, q_ref, k_ref, v_ref, ckv_ref, kr_ref, *, q_scale):
    blocks = _row_blocks(x_ref.shape[0])
    rows = lambda ref, rs: ref[rs, :] if ref.shape[0] != 1 else ref[...]
    rot = rot_ref[...]
    hs = [_rms(x_ref[rs, :], g_ref[...]) * (1.0 + rows(sc_ref, rs)) + rows(sh_ref, rs) for rs in blocks]
    zs = [_dot(h.astype(BF16), win_ref[...]) for h in hs]
    c_qs = [_rms(z[:, 512:896], gqa_ref[...]).astype(BF16) for z in zs]
    c_kvs = [_rms(z[:, 896:1152], gkva_ref[...]) for z in zs]
    krs = [_rope(_rms(z[:, 1152:1280], gkr_ref[...], MLA_ROPE), rot, rows(rc_ref, rs), rows(rs_ref, rs), True)
           for rs, z in zip(blocks, zs)]
    kns = [_dot(c_kv.astype(BF16), wuk_ref[...]) for c_kv in c_kvs]
    vs = [_dot(c_kv.astype(BF16), wuv_ref[...]) for c_kv in c_kvs]
    for rs, z, c_kv, kr, v in zip(blocks, zs, c_kvs, krs, vs):
        u_ref[rs, :] = z[:, :512]
        ckv_ref[rs, :] = c_kv
        kr_ref[rs, :] = kr[:, MLA_NOPE:MLA_QK]
        v_ref[rs, :] = v.astype(BF16)
    for hd in range(MLA_HEADS):
        sl = slice(hd * LANES, (hd + 1) * LANES)
        qhs = [_dot(c_q, wuq_ref[hd]) for c_q in c_qs]
        qns = [_rope(_rms(qh, gqh_ref[...], MLA_QK), rot, rows(rc_ref, rs), rows(rs_ref, rs), False)
               for rs, qh in zip(blocks, qhs)]
        for rs, qn, kn, kr in zip(blocks, qns, kns, krs):
            q_ref[rs, sl] = (qn * q_scale).astype(BF16)
            k_ref[rs, sl] = (kn[:, sl] + kr).astype(BF16)


def _front_even(x, sh, sc, g, wts, tabs, tile, q_scale):
    r = x.shape[0]
    per_row_mod = sh.shape[0] != 1
    per_row_tab = tabs[0].shape[0] != 1
    win, gqa, wuq, gqh, gkva, gkr, wuk, wuv = wts
    in_specs = [pl.BlockSpec((tile, D_MODEL), lambda i: (i, 0)),
                _rows(tile, D_MODEL, per_row_mod), _rows(tile, D_MODEL, per_row_mod),
                _full(g.shape), _full(win.shape), _full(gqa.shape), _full(wuq.shape), _full(gqh.shape),
                _full(gkva.shape), _full(gkr.shape), _full(wuk.shape), _full(wuv.shape),
                _rows(tile, LANES, per_row_tab), _rows(tile, LANES, per_row_tab), _full(tabs[2].shape)]
    widths = [(S5_WIDTH, F32), (MLA_HEADS * LANES, BF16), (MLA_HEADS * LANES, BF16),
              (MLA_HEADS * MLA_V, BF16), (KV_LORA, F32), (MLA_ROPE, F32)]
    return pl.pallas_call(
        functools.partial(_front_even_kernel, q_scale=q_scale),
        grid=(r // tile,),
        in_specs=in_specs,
        out_specs=[pl.BlockSpec((tile, w), lambda i: (i, 0)) for w, _ in widths],
        out_shape=[jax.ShapeDtypeStruct((r, w), dt) for w, dt in widths],
        compiler_params=_cparams(("arbitrary",)),
        name="front_even",
    )(x, sh, sc, g, win, gqa, wuq, gqh, gkva, gkr, wuk, wuv, *tabs)


def _s5_disc_kernel(lr_ref, li_ref, ls_ref, br_ref, bi_ref, are_ref, aim_ref, bbr_ref, bbi_ref,
                    alr_ref, ali_ref, *, log2_len):
    lr, li = lr_ref[...], li_ref[...]
    dt = jnp.exp(ls_ref[...])
    mag = jnp.exp(lr * dt)
    ang = li * dt
    a_re = mag * jnp.cos(ang)
    a_im = mag * jnp.sin(ang)
    den = lr * lr + li * li
    k_re = ((a_re - 1.0) * lr + a_im * li) / den
    k_im = (a_im * lr - (a_re - 1.0) * li) / den
    are_ref[...] = a_re
    aim_ref[...] = a_im
    for c in range(S5_GROUP):
        br, bi = br_ref[c], bi_ref[c]
        bbr_ref[c] = k_re * br - k_im * bi
        bbi_ref[c] = k_re * bi + k_im * br
    pr, pi = a_re, a_im
    for _ in range(log2_len):
        pr, pi = pr * pr - pi * pi, 2.0 * pr * pi
    alr_ref[...] = pr
    ali_ref[...] = pi


def _s5_discretize(lam_re, lam_im, log_step, b_re, b_im, seg_len):
    g, n = lam_re.shape
    outs = [jax.ShapeDtypeStruct((g, n), F32)] * 2 + [jax.ShapeDtypeStruct((S5_GROUP, g, n), F32)] * 2 \
        + [jax.ShapeDtypeStruct((g, n), F32)] * 2
    return pl.pallas_call(
        functools.partial(_s5_disc_kernel, log2_len=int(math.log2(seg_len))),
        out_shape=outs,
        name="s5_discretize",
    )(lam_re, lam_im, log_step.reshape(g, 1), b_re.transpose(2, 0, 1), b_im.transpose(2, 0, 1))


def _s5_layouts(bb_re, bb_im, c_re, c_im):
    eye8 = jnp.eye(8, dtype=F32)

    def bmat(bb):
        b4 = bb.reshape(S5_GROUP, 4, 8, S5_STATE)
        m = jnp.einsum('cjgn,gh->jgchn', b4, eye8)
        return m.reshape(4, LANES, 512).astype(BF16)

    sel = jax.nn.one_hot((2 * jnp.arange(S5_CHUNKS)[:, None] + jnp.arange(2)[None, :]) % 8, 8, dtype=F32)

    def cmat(c):
        c4 = c.reshape(S5_CHUNKS, 2, S5_GROUP, S5_STATE)
        m = jnp.einsum('asck,asg->askgc', c4, sel)
        return m.reshape(S5_CHUNKS, LANES, LANES).astype(BF16)

    return bmat(bb_re), bmat(bb_im), cmat(c_re), cmat(-c_im)


def _gelu_glu(y, wglu_ref):
    z = jax.nn.gelu(y)
    return z * jax.nn.sigmoid(_dot(z.astype(BF16), wglu_ref[...]))


def _s5_scan_kernel(u0_ref, u1_ref, u2_ref, u3_ref, perm_ref, bre_ref, bim_ref, are_ref, aim_ref, alr_ref,
                    ali_ref, cre_ref, cim_ref, d_ref, wglu_ref, y_ref, st_ref, bu_ref, hs_ref, carry_ref,
                    *, seg_len):
    i = pl.program_id(0)

    @pl.when(i == 0)
    def _():
        carry_ref[...] = jnp.zeros_like(carry_ref)

    us = []
    for j, u_ref in enumerate((u0_ref, u1_ref, u2_ref, u3_ref)):
        uj = jnp.concatenate([u_ref[pl.ds(t, SUBLANES, stride=seg_len), :] for t in range(seg_len)], axis=0)
        us.append(uj)
        ub = uj.astype(BF16)
        re = _dot(ub, bre_ref[j])
        im = _dot(ub, bim_ref[j])
        for q in range(4):
            bu_ref[4 * j + q] = re[:, q * LANES:(q + 1) * LANES]
            bu_ref[S5_CHUNKS + 4 * j + q] = im[:, q * LANES:(q + 1) * LANES]

    per = 4
    for grp in range(S5_CHUNKS // per):
        cs = [grp * per + q for q in range(per)]
        ar = [jnp.broadcast_to(are_ref[c], (SUBLANES, LANES)) for c in cs]
        ai = [jnp.broadcast_to(aim_ref[c], (SUBLANES, LANES)) for c in cs]

        def advance(t, hs, store):
            out = []
            for k, c in enumerate(cs):
                hr, hi = hs[2 * k], hs[2 * k + 1]
                rows = pl.ds(pl.multiple_of(t * SUBLANES, SUBLANES), SUBLANES)
                nr = ar[k] * hr - ai[k] * hi + bu_ref[c, rows, :]
                ni = ar[k] * hi + ai[k] * hr + bu_ref[S5_CHUNKS + c, rows, :]
                if store:
                    hs_ref[c, rows, :] = nr
                    hs_ref[S5_CHUNKS + c, rows, :] = ni
                out += [nr, ni]
            return tuple(out)

        zero = tuple(jnp.zeros((SUBLANES, LANES), F32) for _ in range(2 * per))
        ends = lax.fori_loop(0, seg_len, lambda t, hs: advance(t, hs, False), zero, unroll=2)
        init = []
        for k, c in enumerate(cs):
            er, ei = ends[2 * k], ends[2 * k + 1]
            lr, li = alr_ref[c], ali_ref[c]
            hr, hi = carry_ref[c], carry_ref[S5_CHUNKS + c]
            rows_r, rows_i = [], []
            for s in range(SUBLANES):
                rows_r.append(hr)
                rows_i.append(hi)
                hr, hi = (er[s:s + 1] + lr * hr - li * hi, ei[s:s + 1] + lr * hi + li * hr)
            carry_ref[c] = hr
            carry_ref[S5_CHUNKS + c] = hi
            init += [jnp.concatenate(rows_r, axis=0), jnp.concatenate(rows_i, axis=0)]
        lax.fori_loop(0, seg_len, lambda t, hs: advance(t, hs, True), tuple(init), unroll=2)

    ys = []
    for j in range(4):
        cs = [4 * j + q for q in range(4)]
        hcat = jnp.concatenate([hs_ref[c].astype(BF16) for c in cs]
                               + [hs_ref[S5_CHUNKS + c].astype(BF16) for c in cs], axis=1)
        ccat = jnp.concatenate([cre_ref[c] for c in cs] + [cim_ref[c] for c in cs], axis=0)
        ys.append(_dot(hcat, ccat))
    y = jnp.concatenate(ys, axis=1) + d_ref[...] * jnp.concatenate(us, axis=1)
    gated = _gelu_glu(y, wglu_ref).astype(BF16)
    y_ref[...] = _dot(perm_ref[...], gated).astype(BF16)

    @pl.when(i == pl.num_programs(0) - 1)
    def _():
        st_ref[...] = carry_ref[...]


def _s5_scan(u, bre, bim, a_re, a_im, al_re, al_im, cre, cim, d, wglu, seg_len):
    t = u.shape[0]
    tile = SUBLANES * seg_len
    ch = lambda a: a.reshape(S5_CHUNKS, 1, LANES)
    r = jnp.arange(tile)
    perm = jax.nn.one_hot(SUBLANES * (r % seg_len) + r // seg_len, tile, dtype=BF16)
    args = (u, u, u, u, perm, bre, bim, ch(a_re), ch(a_im), ch(al_re), ch(al_im), cre, cim, d, wglu)
    in_specs = ([pl.BlockSpec((tile, LANES), functools.partial(lambda i, j: (i, j), j=j)) for j in range(4)]
                + [_full(a.shape) for a in args[4:]])
    return pl.pallas_call(
        functools.partial(_s5_scan_kernel, seg_len=seg_len),
        grid=(t // tile,),
        in_specs=in_specs,
        out_specs=[pl.BlockSpec((tile, S5_WIDTH), lambda i: (i, 0)), _full((2 * S5_CHUNKS, 1, LANES))],
        out_shape=[jax.ShapeDtypeStruct((t, S5_WIDTH), BF16),
                   jax.ShapeDtypeStruct((2 * S5_CHUNKS, 1, LANES), F32)],
        scratch_shapes=[pltpu.VMEM((2 * S5_CHUNKS, tile, LANES), F32),
                        pltpu.VMEM((2 * S5_CHUNKS, tile, LANES), F32),
                        pltpu.VMEM((2 * S5_CHUNKS, 1, LANES), F32)],
        compiler_params=_cparams(("arbitrary",)),
        name="s5_scan",
    )(*args)


def _s5_step_kernel(u_ref, h0r_ref, h0i_ref, bre_ref, bim_ref, are_ref, aim_ref, cre_ref, cim_ref,
                    d_ref, wglu_ref, y_ref, sr_ref, si_ref):
    u = u_ref[...]
    ub = u.astype(BF16)
    ys = []
    for j in range(4):
        uj = ub[:, j * LANES:(j + 1) * LANES]
        re = _dot(uj, bre_ref[j])
        im = _dot(uj, bim_ref[j])
        acc = None
        for q in range(4):
            sl = slice((4 * j + q) * LANES, (4 * j + q + 1) * LANES)
            ar, ai = are_ref[:, sl], aim_ref[:, sl]
            h0r, h0i = h0r_ref[:, sl], h0i_ref[:, sl]
            hr = ar * h0r - ai * h0i + re[:, q * LANES:(q + 1) * LANES]
            hi = ar * h0i + ai * h0r + im[:, q * LANES:(q + 1) * LANES]
            sr_ref[:, sl] = hr
            si_ref[:, sl] = hi
            t = _dot(hr.astype(BF16), cre_ref[4 * j + q]) + _dot(hi.astype(BF16), cim_ref[4 * j + q])
            acc = t if acc is None else acc + t
        ys.append(acc)
    y = jnp.concatenate(ys, axis=1) + d_ref[...] * u
    y_ref[...] = _gelu_glu(y, wglu_ref).astype(BF16)


def _s5_step(u, h0r, h0i, bre, bim, a_re, a_im, cre, cim, d, wglu):
    b = u.shape[0]
    return pl.pallas_call(
        _s5_step_kernel,
        out_shape=[jax.ShapeDtypeStruct((b, S5_WIDTH), BF16), jax.ShapeDtypeStruct((b, S5_LANES), F32),
                   jax.ShapeDtypeStruct((b, S5_LANES), F32)],
        compiler_params=_cparams(),
        name="s5_step",
    )(u, h0r, h0i, bre, bim, a_re.reshape(1, S5_LANES), a_im.reshape(1, S5_LANES), cre, cim, d, wglu)


def _mla_prompt_kernel(iq_ref, jk_ref, q_ref, k_ref, v_ref, o_ref, m_ref, acc_ref, *, hps):
    p = pl.program_id(1)
    i, j = iq_ref[p], jk_ref[p]
    tq, tk = q_ref.shape[0], k_ref.shape[0]
    low = lax.broadcasted_iota(jnp.int32, (1, LANES), 1) < MLA_V

    @pl.when(j == 0)
    def _():
        m_ref[...] = jnp.full_like(m_ref, -jnp.inf)
        acc_ref[...] = jnp.zeros_like(acc_ref)

    def scores(hh):
        sl = slice(hh * LANES, (hh + 1) * LANES)
        return _dot_nt(q_ref[:, sl], k_ref[:, sl])

    def values(hh):
        v = v_ref[:, (hh // 2) * LANES:(hh // 2 + 1) * LANES]
        return jnp.where(low, v, jnp.ones_like(v)) if hh % 2 == 0 else jnp.where(low, jnp.ones_like(v), v)

    def block(diagonal):
        ahead = 2
        queue = [scores(hh) for hh in range(min(ahead, hps))]
        pending = []
        for hh in range(hps):
            s = queue.pop(0)
            if hh + ahead < hps:
                queue.append(scores(hh + ahead))
            if diagonal:
                row = lax.broadcasted_iota(jnp.int32, (tq, tk), 0)
                col = lax.broadcasted_iota(jnp.int32, (tq, tk), 1)
                s = jnp.where(col <= row, s, -jnp.inf)
            m_prev = m_ref[hh]
            m_new = jnp.maximum(m_prev, jnp.max(s, axis=-1, keepdims=True))
            pending.append((hh, jnp.exp2(s - m_new).astype(BF16), jnp.exp2(m_prev - m_new)))
            m_ref[hh] = m_new
            for ph, pr, alpha in pending[:-1] if hh + 1 < hps else pending:
                acc_ref[ph] = alpha * acc_ref[ph] + _dot(pr, values(ph))
            pending = pending[-1:] if hh + 1 < hps else []

    @pl.when(j < i)
    def _():
        block(False)

    @pl.when(j == i)
    def _():
        block(True)
        for pair in range(hps // 2):
            a0, a1 = acc_ref[2 * pair], acc_ref[2 * pair + 1]
            o = jnp.where(low, a0 / pltpu.roll(a0, MLA_V, axis=1), a1 / pltpu.roll(a1, MLA_V, axis=1))
            o_ref[:, pair * LANES:(pair + 1) * LANES] = o.astype(BF16)


def _mla_prompt_attn(q, k, v, tile, hps):
    t = q.shape[0]
    nq = t // tile
    pairs = [(i, j) for i in range(nq) for j in range(i + 1)]
    iq = jnp.asarray([p[0] for p in pairs], jnp.int32)
    jk = jnp.asarray([p[1] for p in pairs], jnp.int32)
    grid_spec = pltpu.PrefetchScalarGridSpec(
        num_scalar_prefetch=2,
        grid=(MLA_HEADS // hps, len(pairs)),
        in_specs=[pl.BlockSpec((tile, hps * LANES), lambda h, p, iq, jk: (iq[p], h)),
                  pl.BlockSpec((tile, hps * LANES), lambda h, p, iq, jk: (jk[p], h)),
                  pl.BlockSpec((tile, hps * MLA_V), lambda h, p, iq, jk: (jk[p], h))],
        out_specs=pl.BlockSpec((tile, hps * MLA_V), lambda h, p, iq, jk: (iq[p], h)),
        scratch_shapes=[pltpu.VMEM((hps, tile, 1), F32), pltpu.VMEM((hps, tile, LANES), F32)],
    )
    return pl.pallas_call(
        functools.partial(_mla_prompt_kernel, hps=hps),
        grid_spec=grid_spec,
        out_shape=jax.ShapeDtypeStruct((t, MLA_HEADS * MLA_V), BF16),
        compiler_params=_cparams(("arbitrary", "arbitrary")),
        name="mla_prompt_attn",
    )(iq, jk, q, k, v)


def _qlat_kernel(q_ref, wukt_ref, o_ref):
    for hd in range(MLA_HEADS):
        o_ref[hd] = _dot(q_ref[:, hd * LANES:(hd + 1) * LANES], wukt_ref[hd]).astype(BF16)


def _uv_kernel(o_ref, w_ref, y_ref):
    y_ref[...] = _dot(o_ref[...], w_ref[...]).astype(BF16)


def _mla_sample_kernel(pt_ref, ql_ref, qr_ref, cn_ref, kn_ref, ckv_hbm, krt_hbm, o_ref,
                       cbuf, kbuf, sem, *, npages, sub):
    b = pl.program_id(0)
    nb = pl.num_programs(0)
    slot = lax.rem(b, 2)

    def copies(bb, sl):
        out = []
        for p in range(npages):
            page = pt_ref[bb, p]
            tok = pl.ds(p * PAGE, PAGE)
            out.append(pltpu.make_async_copy(ckv_hbm.at[0, page], cbuf.at[sl, tok], sem.at[sl, 0]))
            out.append(pltpu.make_async_copy(krt_hbm.at[0, page], kbuf.at[sl, :, tok], sem.at[sl, 1]))
        return out

    @pl.when(b == 0)
    def _():
        for cp in copies(b, slot):
            cp.start()

    @pl.when(b + 1 < nb)
    def _():
        for cp in copies(b + 1, 1 - slot):
            cp.start()

    ql = ql_ref[0]
    qr = qr_ref[0]
    cn = cn_ref[0].astype(BF16).astype(F32)
    kn = kn_ref[0].astype(BF16).astype(F32)
    s_new = (jnp.sum(ql.astype(F32) * cn, axis=-1, keepdims=True)
             + jnp.sum(qr.astype(F32) * kn, axis=-1, keepdims=True))

    for cp in copies(b, slot):
        cp.wait()

    nsub = npages * PAGE // sub
    cps = [cbuf[slot, pl.ds(sc * sub, sub), :].astype(BF16) for sc in range(nsub)]
    ss = [_dot_nt(ql, cps[sc]) + _dot(qr, kbuf[slot, :, pl.ds(sc * sub, sub)].astype(BF16)) for sc in range(nsub)]
    ms = [jnp.max(s, axis=-1, keepdims=True) for s in ss]
    prs = [jnp.exp(s - m) for s, m in zip(ss, ms)]
    parts = [(m, jnp.sum(pr, axis=-1, keepdims=True), _dot(pr.astype(BF16), cp)) for m, pr, cp in zip(ms, prs, cps)]
    m_all = s_new
    for m, _, _ in parts:
        m_all = jnp.maximum(m_all, m)
    w_new = jnp.exp(s_new - m_all)
    den = w_new
    acc = w_new * cn
    for m, l, o in parts:
        w = jnp.exp(m - m_all)
        den = den + w * l
        acc = acc + w * o
    o_ref[0] = (acc / den).astype(BF16)


def _mla_sample_attn(q, c_new, kr_new, cache_ckv, cache_krope, page_table, wukt, wuv_blk, sub):
    b = q.shape[0]
    npages = page_table.shape[1]
    qlat = pl.pallas_call(
        _qlat_kernel,
        out_shape=jax.ShapeDtypeStruct((MLA_HEADS, b, KV_LORA), BF16),
        name="mla_qlat",
    )(q, wukt)
    ql = qlat.transpose(1, 0, 2)
    qr = q.reshape(b, MLA_HEADS, LANES)[:, :, MLA_NOPE:MLA_QK]
    krt = jnp.swapaxes(cache_krope, 2, 3)
    n = npages * PAGE
    grid_spec = pltpu.PrefetchScalarGridSpec(
        num_scalar_prefetch=1,
        grid=(b,),
        in_specs=[pl.BlockSpec((1, MLA_HEADS, KV_LORA), lambda i, pt: (i, 0, 0)),
                  pl.BlockSpec((1, MLA_HEADS, MLA_ROPE), lambda i, pt: (i, 0, 0)),
                  pl.BlockSpec((1, 1, KV_LORA), lambda i, pt: (i, 0, 0)),
                  pl.BlockSpec((1, 1, MLA_ROPE), lambda i, pt: (i, 0, 0)),
                  pl.BlockSpec(memory_space=pl.ANY),
                  pl.BlockSpec(memory_space=pl.ANY)],
        out_specs=pl.BlockSpec((1, MLA_HEADS, KV_LORA), lambda i, pt: (i, 0, 0)),
        scratch_shapes=[pltpu.VMEM((2, n, KV_LORA), F32), pltpu.VMEM((2, MLA_ROPE, n), F32),
                        pltpu.SemaphoreType.DMA((2, 2))],
    )
    o_lat = pl.pallas_call(
        functools.partial(_mla_sample_kernel, npages=npages, sub=sub),
        grid_spec=grid_spec,
        out_shape=jax.ShapeDtypeStruct((b, MLA_HEADS, KV_LORA), BF16),
        compiler_params=_cparams(("arbitrary",)),
        name="mla_sample_attn",
    )(page_table, ql, qr, c_new.reshape(b, 1, KV_LORA), kr_new.reshape(b, 1, MLA_ROPE), cache_ckv, krt)
    return pl.pallas_call(
        _uv_kernel,
        out_shape=jax.ShapeDtypeStruct((b, MLA_HEADS * MLA_V), BF16),
        name="mla_sample_uv",
    )(o_lat.reshape(b, MLA_HEADS * KV_LORA), wuv_blk)


def _router_gates(h2, wr_ref, br_ref):
    h_hi = h2.astype(BF16)
    h_lo = (h2 - h_hi.astype(F32)).astype(BF16)
    logits = _dot(h_hi, wr_ref[0]) + (_dot(h_lo, wr_ref[0]) + _dot(h_hi, wr_ref[1]))
    scores = jax.nn.sigmoid(logits)
    sel = scores + br_ref[...]
    lane = lax.broadcasted_iota(jnp.int32, (1, LANES), 1)
    pos = lane % GROUP_SIZE
    others, wrapped = [], []
    for r in range(1, GROUP_SIZE):
        wrap = pos + r >= GROUP_SIZE
        fwd = pltpu.roll(sel, LANES - r, axis=1)
        bwd = pltpu.roll(sel, GROUP_SIZE - r, axis=1)
        others.append(jnp.where(wrap, bwd, fwd))
        wrapped.append(wrap)
    a, b, c, d = sel, others[0], others[1], others[2]
    hi1, lo1 = jnp.maximum(a, b), jnp.minimum(a, b)
    hi2, lo2 = jnp.maximum(c, d), jnp.minimum(c, d)
    gscore = jnp.maximum(hi1, hi2) + jnp.maximum(jnp.minimum(hi1, hi2), jnp.maximum(lo1, lo2))
    real = lane < N_EXPERTS
    gscore = jnp.where(real, gscore, -jnp.inf)
    gmax = jnp.max(gscore, axis=-1, keepdims=True)
    gidx = (lane // GROUP_SIZE).astype(F32)
    chosen = jnp.min(jnp.where(gscore == gmax, gidx, float(LANES)), axis=-1, keepdims=True)
    rank = jnp.zeros(sel.shape, F32)
    for o, wrap in zip(others, wrapped):
        rank = rank + jnp.where(wrap, jnp.where(o >= sel, 1.0, 0.0), jnp.where(o > sel, 1.0, 0.0))
    w = jnp.where(gidx == chosen, jnp.where(rank < 2.0, scores, 0.0), 0.0)
    return w / jnp.sum(w, axis=-1, keepdims=True)


def _post_kernel(*refs, n_mix):
    x_ref = refs[0]
    ys = refs[1:1 + n_mix]
    ws = refs[1 + n_mix:1 + 2 * n_mix]
    gt_ref, sh_ref, sc_ref, g_ref, wr_ref, br_ref, x1_ref, h2_ref, gate_ref = refs[1 + 2 * n_mix:]
    blocks = _row_blocks(x_ref.shape[0])
    per_row = gt_ref.shape[0] != 1
    mod = lambda ref, rs: ref[rs, :] if per_row else ref[...]
    mixes = []
    for rs in blocks:
        mix = None
        for y_ref, w_ref in zip(ys, ws):
            t = _dot(y_ref[rs, :], w_ref[...])
            mix = t if mix is None else mix + t
        mixes.append(mix)
    x1s = [x_ref[rs, :] + mod(gt_ref, rs) * mix for rs, mix in zip(blocks, mixes)]
    h2s = [_rms(x1, g_ref[...]) * (1.0 + mod(sc_ref, rs)) + mod(sh_ref, rs) for rs, x1 in zip(blocks, x1s)]
    gates = [_router_gates(h2, wr_ref, br_ref) for h2 in h2s]
    for rs, x1, h2, gate in zip(blocks, x1s, h2s, gates):
        x1_ref[rs, :] = x1
        h2_ref[rs, :] = h2.astype(BF16)
        gate_ref[rs, :] = gate


def _post(x, ys, ws, gt, sh, sc, g, wr, br, tile):
    r = x.shape[0]
    per_row = gt.shape[0] != 1
    n_mix = len(ys)
    in_specs = ([pl.BlockSpec((tile, D_MODEL), lambda i: (i, 0))]
                + [pl.BlockSpec((tile, y.shape[1]), lambda i: (i, 0)) for y in ys]
                + [_full(w.shape) for w in ws]
                + [_rows(tile, D_MODEL, per_row)] * 3
                + [_full(g.shape), _full(wr.shape), _full(br.shape)])
    widths = [(D_MODEL, F32), (D_MODEL, BF16), (LANES, F32)]
    return pl.pallas_call(
        functools.partial(_post_kernel, n_mix=n_mix),
        grid=(r // tile,),
        in_specs=in_specs,
        out_specs=[pl.BlockSpec((tile, w), lambda i: (i, 0)) for w, _ in widths],
        out_shape=[jax.ShapeDtypeStruct((r, w), dt) for w, dt in widths],
        compiler_params=_cparams(("arbitrary",)),
        name="post_mixer",
    )(x, *ys, *ws, gt, sh, sc, g, wr, br)


N_GROUPS = N_EXPERTS // GROUP_SIZE
MOE_CHUNK = 128
_TN = (((0,), (0,)), ((), ()))


def _moe_kernel(h_ref, gate_ref, x1_ref, gt_ref, tri_ref, wg_hbm, wu_hbm, wd_hbm, o_ref,
                wg_v, wu_v, wd_v, sem, hs_ref, ys_ref, *, layer):
    @pl.when(pl.program_id(0) == 0)
    def _():
        copies = [pltpu.make_async_copy(src.at[layer], dst, sem.at[k])
                  for k, (src, dst) in enumerate(((wg_hbm, wg_v), (wu_hbm, wu_v), (wd_hbm, wd_v)))]
        for cp in copies:
            cp.start()
        for cp in copies:
            cp.wait()

    nrows = hs_ref.shape[0]
    gate = gate_ref[...]
    lane = lax.broadcasted_iota(jnp.int32, (1, LANES), 1)
    er = lax.broadcasted_iota(jnp.int32, (LANES, LANES), 0)
    ec = lax.broadcasted_iota(jnp.int32, (LANES, LANES), 1)
    member = jnp.where((er // GROUP_SIZE == ec) & (er < N_EXPERTS), 1.0, 0.0).astype(BF16)
    before = jnp.where(er < ec, 1.0, 0.0).astype(BF16)
    g1 = gate.astype(BF16)
    ind = jnp.where(_dot(g1, member) > 0.5, 1.0, 0.0)
    rank = _dot(tri_ref[...], ind.astype(BF16))
    count = jnp.sum(ind, axis=0, keepdims=True).astype(jnp.int32)
    shift = MOE_CHUNK.bit_length() - 1
    region = (((count + (MOE_CHUNK - 1)) >> shift) << shift).astype(F32)
    start = _dot(jnp.broadcast_to(region, (SUBLANES, LANES)).astype(BF16), before)[0:1]
    pos = jnp.sum(ind * (start + rank), axis=-1, keepdims=True)
    slot = lax.broadcasted_iota(jnp.int32, (1, nrows), 1).astype(F32)
    place = jnp.where(pos == slot, 1.0, 0.0).astype(BF16)

    r1 = gate - g1.astype(F32)
    g2 = r1.astype(BF16)
    g3 = (r1 - g2.astype(F32)).astype(BF16)
    moved = lax.dot_general(place, jnp.concatenate([h_ref[...], g1, g2, g3], axis=1), _TN,
                            preferred_element_type=F32)
    hs_ref[...] = moved[:, :D_MODEL].astype(BF16)
    gs = (moved[:, D_MODEL:D_MODEL + LANES] + moved[:, D_MODEL + LANES:D_MODEL + 2 * LANES]
          + moved[:, D_MODEL + 2 * LANES:])

    ends = [jnp.sum(jnp.where(lane == g, start + region, 0.0)).astype(jnp.int32) for g in range(N_GROUPS)]
    for c in range(nrows // MOE_CHUNK):
        first = c * MOE_CHUNK
        rows = pl.ds(first, MOE_CHUNK)
        gs_c = gs[first:first + MOE_CHUNK]

        @pl.when(first < ends[-1])
        def _():
            grp = sum((first >= e).astype(jnp.int32) for e in ends[:-1])
            hs_c = hs_ref[rows, :]
            experts = [grp * GROUP_SIZE + k for k in range(GROUP_SIZE)]
            gates = [_dot(hs_c, wg_v[e]) for e in experts]
            ups = [_dot(hs_c, wu_v[e]) for e in experts]
            acc = None
            for e, a, u in zip(experts, gates, ups):
                gcol = jnp.sum(jnp.where(lane == e, gs_c, 0.0), axis=-1, keepdims=True)
                t = _dot(((a * jax.nn.sigmoid(a)) * u * gcol).astype(BF16), wd_v[e])
                acc = t if acc is None else acc + t
            ys_ref[rows, :] = acc.astype(BF16)

        @pl.when(first >= ends[-1])
        def _():
            ys_ref[rows, :] = jnp.zeros((MOE_CHUNK, D_MODEL), BF16)

    o_ref[...] = x1_ref[...] + gt_ref[...] * _dot(place, ys_ref[...])


def _moe(h2, gate, x1, gt, wg, wu, wd, layer, tile):
    r = h2.shape[0]
    per_row = gt.shape[0] != 1
    nrows = (-(-tile // MOE_CHUNK) + N_GROUPS) * MOE_CHUNK
    idx = jnp.arange(tile)
    tri = (idx[None, :] < idx[:, None]).astype(BF16)
    return pl.pallas_call(
        functools.partial(_moe_kernel, layer=layer),
        grid=(r // tile,),
        in_specs=[pl.BlockSpec((tile, D_MODEL), lambda i: (i, 0)),
                  pl.BlockSpec((tile, LANES), lambda i: (i, 0)),
                  pl.BlockSpec((tile, D_MODEL), lambda i: (i, 0)),
                  _rows(tile, D_MODEL, per_row),
                  _full(tri.shape),
                  pl.BlockSpec(memory_space=pl.ANY),
                  pl.BlockSpec(memory_space=pl.ANY),
                  pl.BlockSpec(memory_space=pl.ANY)],
        out_specs=pl.BlockSpec((tile, D_MODEL), lambda i: (i, 0)),
        out_shape=jax.ShapeDtypeStruct((r, D_MODEL), F32),
        scratch_shapes=[pltpu.VMEM((N_EXPERTS, D_MODEL, D_FF), BF16),
                        pltpu.VMEM((N_EXPERTS, D_MODEL, D_FF), BF16),
                        pltpu.VMEM((N_EXPERTS, D_FF, D_MODEL), BF16),
                        pltpu.SemaphoreType.DMA((3,)),
                        pltpu.VMEM((nrows, D_MODEL), BF16),
                        pltpu.VMEM((nrows, D_MODEL), BF16)],
        compiler_params=_cparams(("arbitrary",)),
        name="moe",
    )(h2, gate, x1, gt, tri, wg, wu, wd)


def _front_odd_kernel(x_ref, sh_ref, sc_ref, g_ref, win_ref, gq_ref, gk_ref, rc_ref, rs_ref, rot_ref,
                      q_ref, k_ref, v_ref, ku_ref, vu_ref):
    x = x_ref[...]
    h = _rms(x, g_ref[...]) * (1.0 + sc_ref[...]) + sh_ref[...]
    z = _dot(h.astype(BF16), win_ref[...])
    rc, rs, rot = rc_ref[...], rs_ref[...], rot_ref[...]
    scale = SWA_HD ** -0.5
    for hd in range(SWA_HEADS):
        sl = slice(hd * LANES, (hd + 1) * LANES)
        qn = _rope(_rms(z[:, sl], gq_ref[...], SWA_HD), rot, rc, rs, False)
        q_ref[:, sl] = (qn * scale).astype(BF16)
    ks, vs = [], []
    for kh in range(SWA_KV):
        sl = slice(kh * LANES, (kh + 1) * LANES)
        zk = z[:, (SWA_HEADS + kh) * LANES:(SWA_HEADS + kh + 1) * LANES]
        kn = _rope(_rms(zk, gk_ref[...], SWA_HD), rot, rc, rs, True)
        vv = z[:, (SWA_HEADS + SWA_KV + kh) * LANES:(SWA_HEADS + SWA_KV + kh + 1) * LANES]
        k_ref[:, sl] = kn.astype(BF16)
        v_ref[:, sl] = vv.astype(BF16)
        ks.append(kn)
        vs.append(vv)
    for j in range(SWA_KV // 2):
        sl = slice(j * LANES, (j + 1) * LANES)
        ku_ref[:, sl] = ks[2 * j] + pltpu.roll(ks[2 * j + 1], SWA_HD, axis=1)
        vu_ref[:, sl] = vs[2 * j] + pltpu.roll(vs[2 * j + 1], SWA_HD, axis=1)


def _front_odd(x, sh, sc, g, win, gq, gk, tabs, tile):
    r = x.shape[0]
    per_row_mod = sh.shape[0] != 1
    per_row_tab = tabs[0].shape[0] != 1
    in_specs = [pl.BlockSpec((tile, D_MODEL), lambda i: (i, 0)),
                _rows(tile, D_MODEL, per_row_mod), _rows(tile, D_MODEL, per_row_mod),
                _full(g.shape), _full(win.shape), _full(gq.shape), _full(gk.shape),
                _rows(tile, LANES, per_row_tab), _rows(tile, LANES, per_row_tab), _full(tabs[2].shape)]
    widths = [(SWA_HEADS * LANES, BF16), (SWA_KV * LANES, BF16), (SWA_KV * LANES, BF16),
              (SWA_KV * SWA_HD, F32), (SWA_KV * SWA_HD, F32)]
    return pl.pallas_call(
        _front_odd_kernel,
        grid=(r // tile,),
        in_specs=in_specs,
        out_specs=[pl.BlockSpec((tile, w), lambda i: (i, 0)) for w, _ in widths],
        out_shape=[jax.ShapeDtypeStruct((r, w), dt) for w, dt in widths],
        compiler_params=_cparams(("arbitrary",)),
        name="front_odd",
    )(x, sh, sc, g, win, gq, gk, *tabs)


def _swa_prompt_kernel(sink_ref, q_ref, kp_ref, kc_ref, vp_ref, vc_ref, o_ref):
    n = pl.program_id(0)
    w = WINDOW
    qi = lax.broadcasted_iota(jnp.int32, (w, 2 * w), 0) + w
    kj = lax.broadcasted_iota(jnp.int32, (w, 2 * w), 1)
    valid = (kj <= qi) & (qi - kj < w) & (n * w - w + kj >= 0)
    for kh in range(SWA_KV):
        sl = slice(kh * LANES, (kh + 1) * LANES)
        kk = jnp.concatenate([kp_ref[:, sl], kc_ref[:, sl]], axis=0)
        vv = jnp.concatenate([vp_ref[:, sl], vc_ref[:, sl]], axis=0)
        for gi in range(SWA_GROUP):
            hd = kh * SWA_GROUP + gi
            hsl = slice(hd * LANES, (hd + 1) * LANES)
            s = jnp.where(valid, _dot_nt(q_ref[:, hsl], kk), -jnp.inf)
            sink = sink_ref[hd]
            m = jnp.maximum(jnp.max(s, axis=-1, keepdims=True), sink)
            pr = jnp.exp(s - m)
            den = jnp.sum(pr, axis=-1, keepdims=True) + jnp.exp(sink - m)
            o_ref[:, hsl] = (_dot(pr.astype(BF16), vv) / den).astype(BF16)


def _swa_prompt_attn(q, k, v, sinks):
    t = q.shape[0]
    w = WINDOW
    prev = lambda n, s: (jnp.maximum(n - 1, 0), 0)
    cur = lambda n, s: (n, 0)
    grid_spec = pltpu.PrefetchScalarGridSpec(
        num_scalar_prefetch=1,
        grid=(t // w,),
        in_specs=[pl.BlockSpec((w, SWA_HEADS * LANES), cur),
                  pl.BlockSpec((w, SWA_KV * LANES), prev), pl.BlockSpec((w, SWA_KV * LANES), cur),
                  pl.BlockSpec((w, SWA_KV * LANES), prev), pl.BlockSpec((w, SWA_KV * LANES), cur)],
        out_specs=pl.BlockSpec((w, SWA_HEADS * LANES), cur),
    )
    return pl.pallas_call(
        _swa_prompt_kernel,
        grid_spec=grid_spec,
        out_shape=jax.ShapeDtypeStruct((t, SWA_HEADS * LANES), BF16),
        compiler_params=_cparams(("arbitrary",)),
        name="swa_prompt_attn",
    )(sinks, q, k, k, v, v)


def _swa_sample_kernel(q_ref, kc_ref, vc_ref, kn_ref, vn_ref, sink_ref, o_ref, wk_ref, wv_ref, *, sb):
    w = WINDOW
    width = SWA_KV * SWA_HD
    rowgrp = lax.broadcasted_iota(jnp.int32, (SWA_HEADS, 1), 0) // SWA_GROUP
    col = lax.broadcasted_iota(jnp.int32, (1, w), 1)
    low = lax.broadcasted_iota(jnp.int32, (1, LANES), 1) < SWA_HD
    sink = sink_ref[...]
    for b in range(sb):
        qf = q_ref[b].astype(F32)
        a = jnp.concatenate([qf, jnp.zeros_like(qf)], axis=1)
        qx = a
        for kh in range(1, SWA_KV):
            qx = jnp.where(rowgrp == kh, pltpu.roll(a, kh * SWA_HD, axis=1), qx)
        kc = kc_ref[b]
        vc = vc_ref[b]
        kn = kn_ref[b]
        vn = vn_ref[b]
        s = _dot_nt(qx.astype(BF16), kc.astype(BF16))
        s = jnp.where(col >= 1, s, -jnp.inf)
        s_new = jnp.sum(qx.astype(BF16).astype(F32) * kn.astype(BF16).astype(F32), axis=-1, keepdims=True)
        m = jnp.maximum(jnp.maximum(jnp.max(s, axis=-1, keepdims=True), s_new), sink)
        pr = jnp.exp(s - m)
        pn = jnp.exp(s_new - m)
        den = jnp.sum(pr, axis=-1, keepdims=True) + pn + jnp.exp(sink - m)
        o = (_dot(pr.astype(BF16), vc.astype(BF16)) + pn * vn.astype(BF16).astype(F32)) / den
        ox = o
        for kh in range(1, SWA_KV):
            ox = jnp.where(rowgrp == kh, pltpu.roll(o, width - kh * SWA_HD, axis=1), ox)
        o_ref[b] = jnp.where(low, ox[:, :LANES], 0.0).astype(BF16)
        wk_ref[b, pl.ds(0, w - 1), :] = kc_ref[b, pl.ds(1, w - 1), :]
        wk_ref[b, pl.ds(w - 1, 1), :] = kn
        wv_ref[b, pl.ds(0, w - 1), :] = vc_ref[b, pl.ds(1, w - 1), :]
        wv_ref[b, pl.ds(w - 1, 1), :] = vn


def _swa_sample_attn(q, kc, vc, kn, vn, sinks, sb):
    b = q.shape[0]
    w = WINDOW
    width = SWA_KV * SWA_HD
    blk3 = lambda s1, s2: pl.BlockSpec((sb, s1, s2), lambda i: (i, 0, 0))
    return pl.pallas_call(
        functools.partial(_swa_sample_kernel, sb=sb),
        grid=(b // sb,),
        in_specs=[blk3(SWA_HEADS, LANES), blk3(w, width), blk3(w, width), blk3(1, width), blk3(1, width),
                  _full((SWA_HEADS, 1))],
        out_specs=[blk3(SWA_HEADS, LANES), blk3(w, width), blk3(w, width)],
        out_shape=[jax.ShapeDtypeStruct((b, SWA_HEADS, LANES), BF16),
                   jax.ShapeDtypeStruct((b, w, width), F32), jax.ShapeDtypeStruct((b, w, width), F32)],
        compiler_params=_cparams(("arbitrary",)),
        name="swa_sample_attn",
    )(q.reshape(b, SWA_HEADS, LANES), kc, vc, kn.reshape(b, 1, width), vn.reshape(b, 1, width),
      sinks.reshape(SWA_HEADS, 1))


def _rope_cos_sin(pos):
    inv = jnp.concatenate([ROPE_THETA ** (-jnp.arange(h, dtype=F32) / h) for h in (MLA_ROPE // 2, SWA_ROT // 2)])
    ang = pos.astype(F32)[:, None] * inv[None, :]
    cos, sin = jnp.cos(ang), jnp.sin(ang)
    cut = MLA_ROPE // 2
    return (cos[:, :cut], sin[:, :cut]), (cos[:, cut:], sin[:, cut:])


def _rope_tables(cos_sin, offset):
    cos, sin = cos_sin
    n, half = cos.shape
    rest = LANES - offset - 2 * half
    rc = jnp.concatenate([jnp.ones((n, offset), F32), cos, cos, jnp.ones((n, rest), F32)], axis=1)
    rs = jnp.concatenate([jnp.zeros((n, offset), F32), sin, sin, jnp.zeros((n, rest), F32)], axis=1)
    lane = jnp.arange(LANES)
    first = (lane >= offset) & (lane < offset + half)
    second = (lane >= offset + half) & (lane < offset + 2 * half)
    src = lane[:, None]
    rot = (jnp.where(first[None, :] & (src == lane[None, :] + half), -1.0, 0.0)
           + jnp.where(second[None, :] & (src == lane[None, :] - half), 1.0, 0.0))
    return rc, rs, rot.astype(BF16)


def _pad_lanes(v, offset=0):
    return jnp.zeros((1, LANES), F32).at[0, offset:offset + v.shape[0]].set(v)


def _even_weights(w_in, g_qa, w_uq, g_qh, g_kva, g_kr, w_uk, w_uv):
    o3 = S5_WIDTH + Q_LORA + KV_LORA
    win = jnp.zeros((D_MODEL, 1280), F32).at[:, :o3].set(w_in[:, :o3])
    win = win.at[:, o3 + MLA_NOPE:o3 + MLA_QK].set(w_in[:, o3:])
    wuq = jnp.zeros((MLA_HEADS, Q_LORA, LANES), F32).at[:, :, :MLA_QK].set(w_uq.transpose(1, 0, 2))
    wuk = jnp.zeros((KV_LORA, MLA_HEADS, LANES), F32).at[:, :, :MLA_NOPE].set(w_uk)
    front = (win.astype(BF16), g_qa.reshape(1, Q_LORA), wuq.astype(BF16), _pad_lanes(g_qh),
             g_kva.reshape(1, KV_LORA), _pad_lanes(g_kr, MLA_NOPE),
             wuk.reshape(KV_LORA, MLA_HEADS * LANES).astype(BF16),
             w_uv.reshape(KV_LORA, MLA_HEADS * MLA_V).astype(BF16))
    wukt = jnp.zeros((MLA_HEADS, LANES, KV_LORA), F32).at[:, :MLA_NOPE, :].set(w_uk.transpose(1, 2, 0))
    eye = jnp.eye(MLA_HEADS, dtype=F32)
    wuv_blk = jnp.einsum('chd,hg->hcgd', w_uv, eye).reshape(MLA_HEADS * KV_LORA, MLA_HEADS * MLA_V)
    return front, wukt.astype(BF16), wuv_blk.astype(BF16)


def _odd_weights(w_in, w_out):
    nq, nk = SWA_HEADS * SWA_HD, SWA_KV * SWA_HD
    nh = SWA_HEADS + 2 * SWA_KV
    win = jnp.zeros((D_MODEL, nh, LANES), F32).at[:, :, :SWA_HD].set(w_in.reshape(D_MODEL, nh, SWA_HD))
    wout = jnp.zeros((SWA_HEADS, LANES, D_MODEL), F32).at[:, :SWA_HD, :].set(w_out.reshape(SWA_HEADS, SWA_HD, D_MODEL))
    del nq, nk
    return win.reshape(D_MODEL, nh * LANES).astype(BF16), wout.reshape(SWA_HEADS * LANES, D_MODEL).astype(BF16)


def kernel(x_prompt, x_sample, c_prompt, c_sample, cache_ckv, cache_krope, page_table, state_s5_re, state_s5_im, cache_win_k, cache_win_v, w_mod, b_mod, g_norm_mix, g_norm_ffn, w_in_even, w_out_even, s5_lam_re, s5_lam_im, s5_log_step, s5_b_re, s5_b_im, s5_c_re, s5_c_im, s5_d, s5_w_glu, mla_g_qa, mla_w_uq, mla_g_qh, mla_g_kva, mla_g_kr, mla_w_uk, mla_w_uv, w_in_odd, w_out_odd, swa_g_q, swa_g_k, swa_sinks, w_router, b_router, moe_w_gate, moe_w_up, moe_w_down):
    t = x_prompt.shape[1]
    nb = x_sample.shape[0]
    past_len = page_table.shape[1] * PAGE
    row_tile = 512
    s5_seg = 64
    attn_tile, attn_heads = 1024, MLA_HEADS
    mla_scale = MLA_QK ** -0.5
    moe_tile = 512
    sample_sub = min(4096, past_len)
    swa_seq_block = 8

    n_c = 1 + nb
    rp = -(-n_c // SUBLANES) * SUBLANES
    c_all = jnp.concatenate([c_prompt, c_sample, jnp.zeros((rp - n_c, D_MODEL), F32)], axis=0)
    mod = _modulation(c_all, w_mod, b_mod)

    def mods(layer, sample):
        rows = mod[layer, 1:1 + nb] if sample else mod[layer, 0:1]
        return [rows[:, k * D_MODEL:(k + 1) * D_MODEL] for k in range(6)]

    wr = jnp.zeros((D_MODEL, LANES), F32).at[:, :N_EXPERTS].set(w_router)
    wr_hi = wr.astype(BF16)
    wr = jnp.stack([wr_hi, (wr - wr_hi.astype(F32)).astype(BF16)])
    br = _pad_lanes(b_router)
    wg, wu, wd = moe_w_gate.astype(BF16), moe_w_up.astype(BF16), moe_w_down.astype(BF16)

    xp = x_prompt.reshape(t, D_MODEL)
    xs = x_sample.reshape(nb, D_MODEL)
    mla_cs_p, swa_cs_p = _rope_cos_sin(jnp.arange(t))
    mla_cs_s, swa_cs_s = _rope_cos_sin(jnp.full((1,), past_len))
    mla_tabs_p = _rope_tables(mla_cs_p, MLA_NOPE)
    mla_tabs_s = _rope_tables(mla_cs_s, MLA_NOPE)
    swa_tabs_p = _rope_tables(swa_cs_p, 0)
    swa_tabs_s = _rope_tables(swa_cs_s, 0)

    front_w, wukt, wuv_blk = _even_weights(w_in_even[0], mla_g_qa[0], mla_w_uq[0], mla_g_qh[0], mla_g_kva[0],
                                           mla_g_kr[0], mla_w_uk[0], mla_w_uv[0])
    a_re, a_im, bb_re, bb_im, al_re, al_im = _s5_discretize(s5_lam_re[0], s5_lam_im[0], s5_log_step[0],
                                                            s5_b_re[0], s5_b_im[0], s5_seg)
    bre, bim, cre, cim = _s5_layouts(bb_re, bb_im, s5_c_re[0], s5_c_im[0])
    d_skip = s5_d[0].reshape(1, S5_WIDTH)
    wglu = s5_w_glu[0].astype(BF16)
    wo_ssm = w_out_even[0, :S5_WIDTH].astype(BF16)
    wo_att = w_out_even[0, S5_WIDTH:].astype(BF16)
    g_mix0 = g_norm_mix[0].reshape(1, D_MODEL)
    g_ffn0 = g_norm_ffn[0].reshape(1, D_MODEL)

    sh1, sc1, gt1, sh2, sc2, gt2 = mods(0, False)
    u, q, k, v, ckv_p, kr_p = _front_even(xp, sh1, sc1, g_mix0, front_w, mla_tabs_p,
                                          row_tile, mla_scale * math.log2(math.e))
    y_ssm, st = _s5_scan(u, bre, bim, a_re, a_im, al_re, al_im, cre, cim, d_skip, wglu, s5_seg)
    sr_p = st[:S5_CHUNKS].reshape(1, 1, S5_GROUPS, S5_STATE)
    si_p = st[S5_CHUNKS:].reshape(1, 1, S5_GROUPS, S5_STATE)
    y_att = _mla_prompt_attn(q, k, v, attn_tile, attn_heads)
    x1, h2, gate = _post(xp, [y_ssm, y_att], [wo_ssm, wo_att], gt1, sh2, sc2, g_ffn0, wr, br, row_tile)
    xp = _moe(h2, gate, x1, gt2, wg, wu, wd, 0, moe_tile)

    sh1, sc1, gt1, sh2, sc2, gt2 = mods(0, True)
    u, q, _, _, ckv_s, kr_s = _front_even(xs, sh1, sc1, g_mix0, front_w, mla_tabs_s, nb, mla_scale)
    y_ssm, sr_s, si_s = _s5_step(u, state_s5_re[0].reshape(nb, S5_LANES), state_s5_im[0].reshape(nb, S5_LANES),
                                 bre, bim, a_re, a_im, cre, cim, d_skip, wglu)
    y_att = _mla_sample_attn(q, ckv_s, kr_s, cache_ckv, cache_krope, page_table, wukt, wuv_blk, sample_sub)
    x1, h2, gate = _post(xs, [y_ssm, y_att], [wo_ssm, wo_att], gt1, sh2, sc2, g_ffn0, wr, br, nb)
    xs = _moe(h2, gate, x1, gt2, wg, wu, wd, 0, nb)

    win_odd, wout_odd = _odd_weights(w_in_odd[0], w_out_odd[0])
    gq, gk = _pad_lanes(swa_g_q[0]), _pad_lanes(swa_g_k[0])
    g_mix1 = g_norm_mix[1].reshape(1, D_MODEL)
    g_ffn1 = g_norm_ffn[1].reshape(1, D_MODEL)
    sinks = swa_sinks[0]

    sh1, sc1, gt1, sh2, sc2, gt2 = mods(1, False)
    q, k, v, ku, vu = _front_odd(xp, sh1, sc1, g_mix1, win_odd, gq, gk, swa_tabs_p, row_tile)
    o = _swa_prompt_attn(q, k, v, sinks)
    wk_p = ku[t - WINDOW:].reshape(1, 1, WINDOW, SWA_KV, SWA_HD)
    wv_p = vu[t - WINDOW:].reshape(1, 1, WINDOW, SWA_KV, SWA_HD)
    x1, h2, gate = _post(xp, [o], [wout_odd], gt1, sh2, sc2, g_ffn1, wr, br, row_tile)
    xp = _moe(h2, gate, x1, gt2, wg, wu, wd, 1, moe_tile)

    sh1, sc1, gt1, sh2, sc2, gt2 = mods(1, True)
    q, _, _, ku, vu = _front_odd(xs, sh1, sc1, g_mix1, win_odd, gq, gk, swa_tabs_s, nb)
    width = SWA_KV * SWA_HD
    o, wk_s, wv_s = _swa_sample_attn(q, cache_win_k[0].reshape(nb, WINDOW, width),
                                     cache_win_v[0].reshape(nb, WINDOW, width), ku, vu, sinks, swa_seq_block)
    x1, h2, gate = _post(xs, [o.reshape(nb, SWA_HEADS * LANES)], [wout_odd], gt1, sh2, sc2, g_ffn1, wr, br, nb)
    xs = _moe(h2, gate, x1, gt2, wg, wu, wd, 1, nb)

    return (xp.reshape(1, t, D_MODEL), xs.reshape(nb, 1, D_MODEL),
            ckv_p.reshape(1, 1, t, KV_LORA), kr_p.reshape(1, 1, t, MLA_ROPE), sr_p, si_p, wk_p, wv_p,
            ckv_s.reshape(1, nb, 1, KV_LORA), kr_s.reshape(1, nb, 1, MLA_ROPE),
            sr_s.reshape(1, nb, S5_GROUPS, S5_STATE), si_s.reshape(1, nb, S5_GROUPS, S5_STATE),
            wk_s.reshape(1, nb, WINDOW, SWA_KV, SWA_HD), wv_s.reshape(1, nb, WINDOW, SWA_KV, SWA_HD))
```

```python
import functools
import math

import jax
import jax.numpy as jnp
from jax import lax
from jax.experimental import pallas as pl
from jax.experimental.pallas import tpu as pltpu

F32 = jnp.float32
BF16 = jnp.bfloat16
EPS = 1e-6
ROPE_THETA = 500000.0
LANES = 128
SUBLANES = 8
VMEM_LIMIT = 56 * 1024 * 1024

D_MODEL = 1024
PAGE = 128
S5_WIDTH = 512
S5_GROUP = 16
S5_GROUPS = 32
S5_STATE = 64
S5_LANES = S5_GROUPS * S5_STATE
S5_CHUNKS = S5_LANES // LANES
MLA_HEADS = 8
MLA_NOPE = 64
MLA_ROPE = 32
MLA_QK = 96
MLA_V = 64
Q_LORA = 384
KV_LORA = 256
SWA_HEADS = 16
SWA_KV = 4
SWA_GROUP = 4
SWA_HD = 64
SWA_ROT = 16
WINDOW = 128
N_EXPERTS = 16
GROUP_SIZE = 4
D_FF = 256

_NT = (((1,), (1,)), ((), ()))


def _dot(a, b):
    return jnp.dot(a, b, preferred_element_type=F32)


def _dot_nt(a, b):
    return lax.dot_general(a, b, _NT, preferred_element_type=F32)


def _rms(x, g, n=None):
    n = x.shape[-1] if n is None else n
    ss = jnp.sum(x * x, axis=-1, keepdims=True) * (1.0 / n)
    return x * lax.rsqrt(ss + EPS) * g


def _rope(x, rot, rc, rs, exact):
    hi = x.astype(BF16)
    partner = _dot(hi, rot)
    if exact:
        partner = partner + _dot((x - hi.astype(F32)).astype(BF16), rot)
    return x * rc + partner * rs


def _row_blocks(rows):
    n = 2 if rows % (4 * SUBLANES) == 0 else 1
    return [slice(b * rows // n, (b + 1) * rows // n) for b in range(n)]


def _cparams(sem=None, vmem=VMEM_LIMIT):
    return pltpu.CompilerParams(dimension_semantics=sem, vmem_limit_bytes=vmem)


def _full(shape):
    nd = len(shape)
    return pl.BlockSpec(shape, lambda *_: (0,) * nd)


def _rows(tile, width, per_row):
    if per_row:
        return pl.BlockSpec((tile, width), lambda i: (i, 0))
    return pl.BlockSpec((1, width), lambda i: (0, 0))


def _mod_kernel(c_ref, w_ref, b_ref, o_ref):
    c = c_ref[...]
    s = (c * jax.nn.sigmoid(c)).astype(BF16)
    o_ref[0] = _dot(s, w_ref[0].astype(BF16)) + b_ref[0]


def _modulation(c_all, w_mod, b_mod):
    depth, _, n = w_mod.shape
    rp = c_all.shape[0]
    tn = 1536
    return pl.pallas_call(
        _mod_kernel,
        grid=(depth, n // tn),
        in_specs=[pl.BlockSpec((rp, D_MODEL), lambda l, j: (0, 0)),
                  pl.BlockSpec((1, D_MODEL, tn), lambda l, j: (l, 0, j)),
                  pl.BlockSpec((1, 1, tn), lambda l, j: (l, 0, j))],
        out_specs=pl.BlockSpec((1, rp, tn), lambda l, j: (l, 0, j)),
        out_shape=jax.ShapeDtypeStruct((depth, rp, n), F32),
        compiler_params=_cparams(("arbitrary", "arbitrary")),
        name="modulation",
    )(c_all, w_mod, b_mod.reshape(depth, 1, n))


def _front_even_kernel(x_ref, sh_ref, sc_ref, g_ref, win_ref, gqa_ref, wuq_ref, gqh_ref, gkva_ref,
                       gkr_ref, wuk_ref, wuv_ref, rc_ref, rs_ref, rot_ref,
                       u_ref, q_ref, k_ref, v_ref, ckv_ref, kr_ref, *, q_scale):
    blocks = _row_blocks(x_ref.shape[0])
    rows = lambda ref, rs: ref[rs, :] if ref.shape[0] != 1 else ref[...]
    rot = rot_ref[...]
    hs = [_rms(x_ref[rs, :], g_ref[...]) * (1.0 + rows(sc_ref, rs)) + rows(sh_ref, rs) for rs in blocks]
    zs = [_dot(h.astype(BF16), win_ref[...]) for h in hs]
    c_qs = [_rms(z[:, 512:896], gqa_ref[...]).astype(BF16) for z in zs]
    c_kvs = [_rms(z[:, 896:1152], gkva_ref[...]) for z in zs]
    krs = [_rope(_rms(z[:, 1152:1280], gkr_ref[...], MLA_ROPE), rot, rows(rc_ref, rs), rows(rs_ref, rs), True)
           for rs, z in zip(blocks, zs)]
    kns = [_dot(c_kv.astype(BF16), wuk_ref[...]) for c_kv in c_kvs]
    vs = [_dot(c_kv.astype(BF16), wuv_ref[...]) for c_kv in c_kvs]
    for rs, z, c_kv, kr, v in zip(blocks, zs, c_kvs, krs, vs):
        u_ref[rs, :] = z[:, :512]
        ckv_ref[rs, :] = c_kv
        kr_ref[rs, :] = kr[:, MLA_NOPE:MLA_QK]
        v_ref[rs, :] = v.astype(BF16)
    for hd in range(MLA_HEADS):
        sl = slice(hd * LANES, (hd + 1) * LANES)
        qhs = [_dot(c_q, wuq_ref[hd]) for c_q in c_qs]
        qns = [_rope(_rms(qh, gqh_ref[...], MLA_QK), rot, rows(rc_ref, rs), rows(rs_ref, rs), False)
               for rs, qh in zip(blocks, qhs)]
        for rs, qn, kn, kr in zip(blocks, qns, kns, krs):
            q_ref[rs, sl] = (qn * q_scale).astype(BF16)
            k_ref[rs, sl] = (kn[:, sl] + kr).astype(BF16)


def _front_even(x, sh, sc, g, wts, tabs, tile, q_scale):
    r = x.shape[0]
    per_row_mod = sh.shape[0] != 1
    per_row_tab = tabs[0].shape[0] != 1
    win, gqa, wuq, gqh, gkva, gkr, wuk, wuv = wts
    in_specs = [pl.BlockSpec((tile, D_MODEL), lambda i: (i, 0)),
                _rows(tile, D_MODEL, per_row_mod), _rows(tile, D_MODEL, per_row_mod),
                _full(g.shape), _full(win.shape), _full(gqa.shape), _full(wuq.shape), _full(gqh.shape),
                _full(gkva.shape), _full(gkr.shape), _full(wuk.shape), _full(wuv.shape),
                _rows(tile, LANES, per_row_tab), _rows(tile, LANES, per_row_tab), _full(tabs[2].shape)]
    widths = [(S5_WIDTH, F32), (MLA_HEADS * LANES, BF16), (MLA_HEADS * LANES, BF16),
              (MLA_HEADS * MLA_V, BF16), (KV_LORA, F32), (MLA_ROPE, F32)]
    return pl.pallas_call(
        functools.partial(_front_even_kernel, q_scale=q_scale),
        grid=(r // tile,),
        in_specs=in_specs,
        out_specs=[pl.BlockSpec((tile, w), lambda i: (i, 0)) for w, _ in widths],
        out_shape=[jax.ShapeDtypeStruct((r, w), dt) for w, dt in widths],
        compiler_params=_cparams(("arbitrary",)),
        name="front_even",
    )(x, sh, sc, g, win, gqa, wuq, gqh, gkva, gkr, wuk, wuv, *tabs)


def _s5_disc_kernel(lr_ref, li_ref, ls_ref, br_ref, bi_ref, are_ref, aim_ref, bbr_ref, bbi_ref,
                    alr_ref, ali_ref, *, log2_len):
    lr, li = lr_ref[...], li_ref[...]
    dt = jnp.exp(ls_ref[...])
    mag = jnp.exp(lr * dt)
    ang = li * dt
    a_re = mag * jnp.cos(ang)
    a_im = mag * jnp.sin(ang)
    den = lr * lr + li * li
    k_re = ((a_re - 1.0) * lr + a_im * li) / den
    k_im = (a_im * lr - (a_re - 1.0) * li) / den
    are_ref[...] = a_re
    aim_ref[...] = a_im
    for c in range(S5_GROUP):
        br, bi = br_ref[c], bi_ref[c]
        bbr_ref[c] = k_re * br - k_im * bi
        bbi_ref[c] = k_re * bi + k_im * br
    pr, pi = a_re, a_im
    for _ in range(log2_len):
        pr, pi = pr * pr - pi * pi, 2.0 * pr * pi
    alr_ref[...] = pr
    ali_ref[...] = pi


def _s5_discretize(lam_re, lam_im, log_step, b_re, b_im, seg_len):
    g, n = lam_re.shape
    outs = [jax.ShapeDtypeStruct((g, n), F32)] * 2 + [jax.ShapeDtypeStruct((S5_GROUP, g, n), F32)] * 2 \
        + [jax.ShapeDtypeStruct((g, n), F32)] * 2
    return pl.pallas_call(
        functools.partial(_s5_disc_kernel, log2_len=int(math.log2(seg_len))),
        out_shape=outs,
        name="s5_discretize",
    )(lam_re, lam_im, log_step.reshape(g, 1), b_re.transpose(2, 0, 1), b_im.transpose(2, 0, 1))


def _s5_layouts(bb_re, bb_im, c_re, c_im):
    eye8 = jnp.eye(8, dtype=F32)

    def bmat(bb):
        b4 = bb.reshape(S5_GROUP, 4, 8, S5_STATE)
        m = jnp.einsum('cjgn,gh->jgchn', b4, eye8)
        return m.reshape(4, LANES, 512).astype(BF16)

    sel = jax.nn.one_hot((2 * jnp.arange(S5_CHUNKS)[:, None] + jnp.arange(2)[None, :]) % 8, 8, dtype=F32)

    def cmat(c):
        c4 = c.reshape(S5_CHUNKS, 2, S5_GROUP, S5_STATE)
        m = jnp.einsum('asck,asg->askgc', c4, sel)
        return m.reshape(S5_CHUNKS, LANES, LANES).astype(BF16)

    return bmat(bb_re), bmat(bb_im), cmat(c_re), cmat(-c_im)


def _gelu_glu(y, wglu_ref):
    z = jax.nn.gelu(y)
    return z * jax.nn.sigmoid(_dot(z.astype(BF16), wglu_ref[...]))


def _s5_scan_kernel(u0_ref, u1_ref, u2_ref, u3_ref, perm_ref, bre_ref, bim_ref, are_ref, aim_ref, alr_ref,
                    ali_ref, cre_ref, cim_ref, d_ref, wglu_ref, y_ref, st_ref, bu_ref, hs_ref, carry_ref,
                    *, seg_len):
    i = pl.program_id(0)

    @pl.when(i == 0)
    def _():
        carry_ref[...] = jnp.zeros_like(carry_ref)

    us = []
    for j, u_ref in enumerate((u0_ref, u1_ref, u2_ref, u3_ref)):
        uj = jnp.concatenate([u_ref[pl.ds(t, SUBLANES, stride=seg_len), :] for t in range(seg_len)], axis=0)
        us.append(uj)
        ub = uj.astype(BF16)
        re = _dot(ub, bre_ref[j])
        im = _dot(ub, bim_ref[j])
        for q in range(4):
            bu_ref[4 * j + q] = re[:, q * LANES:(q + 1) * LANES]
            bu_ref[S5_CHUNKS + 4 * j + q] = im[:, q * LANES:(q + 1) * LANES]

    per = 4
    for grp in range(S5_CHUNKS // per):
        cs = [grp * per + q for q in range(per)]
        ar = [jnp.broadcast_to(are_ref[c], (SUBLANES, LANES)) for c in cs]
        ai = [jnp.broadcast_to(aim_ref[c], (SUBLANES, LANES)) for c in cs]

        def advance(t, hs, store):
            out = []
            for k, c in enumerate(cs):
                hr, hi = hs[2 * k], hs[2 * k + 1]
                rows = pl.ds(pl.multiple_of(t * SUBLANES, SUBLANES), SUBLANES)
                nr = ar[k] * hr - ai[k] * hi + bu_ref[c, rows, :]
                ni = ar[k] * hi + ai[k] * hr + bu_ref[S5_CHUNKS + c, rows, :]
                if store:
                    hs_ref[c, rows, :] = nr
                    hs_ref[S5_CHUNKS + c, rows, :] = ni
                out += [nr, ni]
            return tuple(out)

        zero = tuple(jnp.zeros((SUBLANES, LANES), F32) for _ in range(2 * per))
        ends = lax.fori_loop(0, seg_len, lambda t, hs: advance(t, hs, False), zero, unroll=2)
        init = []
        for k, c in enumerate(cs):
            er, ei = ends[2 * k], ends[2 * k + 1]
            lr, li = alr_ref[c], ali_ref[c]
            hr, hi = carry_ref[c], carry_ref[S5_CHUNKS + c]
            rows_r, rows_i = [], []
            for s in range(SUBLANES):
                rows_r.append(hr)
                rows_i.append(hi)
                hr, hi = (er[s:s + 1] + lr * hr - li * hi, ei[s:s + 1] + lr * hi + li * hr)
            carry_ref[c] = hr
            carry_ref[S5_CHUNKS + c] = hi
            init += [jnp.concatenate(rows_r, axis=0), jnp.concatenate(rows_i, axis=0)]
        lax.fori_loop(0, seg_len, lambda t, hs: advance(t, hs, True), tuple(init), unroll=2)

    ys = []
    for j in range(4):
        cs = [4 * j + q for q in range(4)]
        hcat = jnp.concatenate([hs_ref[c].astype(BF16) for c in cs]
                               + [hs_ref[S5_CHUNKS + c].astype(BF16) for c in cs], axis=1)
        ccat = jnp.concatenate([cre_ref[c] for c in cs] + [cim_ref[c] for c in cs], axis=0)
        ys.append(_dot(hcat, ccat))
    y = jnp.concatenate(ys, axis=1) + d_ref[...] * jnp.concatenate(us, axis=1)
    gated = _gelu_glu(y, wglu_ref).astype(BF16)
    y_ref[...] = _dot(perm_ref[...], gated).astype(BF16)

    @pl.when(i == pl.num_programs(0) - 1)
    def _():
        st_ref[...] = carry_ref[...]


def _s5_scan(u, bre, bim, a_re, a_im, al_re, al_im, cre, cim, d, wglu, seg_len):
    t = u.shape[0]
    tile = SUBLANES * seg_len
    ch = lambda a: a.reshape(S5_CHUNKS, 1, LANES)
    r = jnp.arange(tile)
    perm = jax.nn.one_hot(SUBLANES * (r % seg_len) + r // seg_len, tile, dtype=BF16)
    args = (u, u, u, u, perm, bre, bim, ch(a_re), ch(a_im), ch(al_re), ch(al_im), cre, cim, d, wglu)
    in_specs = ([pl.BlockSpec((tile, LANES), functools.partial(lambda i, j: (i, j), j=j)) for j in range(4)]
                + [_full(a.shape) for a in args[4:]])
    return pl.pallas_call(
        functools.partial(_s5_scan_kernel, seg_len=seg_len),
        grid=(t // tile,),
        in_specs=in_specs,
        out_specs=[pl.BlockSpec((tile, S5_WIDTH), lambda i: (i, 0)), _full((2 * S5_CHUNKS, 1, LANES))],
        out_shape=[jax.ShapeDtypeStruct((t, S5_WIDTH), BF16),
                   jax.ShapeDtypeStruct((2 * S5_CHUNKS, 1, LANES), F32)],
        scratch_shapes=[pltpu.VMEM((2 * S5_CHUNKS, tile, LANES), F32),
                        pltpu.VMEM((2 * S5_CHUNKS, tile, LANES), F32),
                        pltpu.VMEM((2 * S5_CHUNKS, 1, LANES), F32)],
        compiler_params=_cparams(("arbitrary",)),
        name="s5_scan",
    )(*args)


def _s5_step_kernel(u_ref, h0r_ref, h0i_ref, bre_ref, bim_ref, are_ref, aim_ref, cre_ref, cim_ref,
                    d_ref, wglu_ref, y_ref, sr_ref, si_ref):
    u = u_ref[...]
    ub = u.astype(BF16)
    ys = []
    for j in range(4):
        uj = ub[:, j * LANES:(j + 1) * LANES]
        re = _dot(uj, bre_ref[j])
        im = _dot(uj, bim_ref[j])
        acc = None
        for q in range(4):
            sl = slice((4 * j + q) * LANES, (4 * j + q + 1) * LANES)
            ar, ai = are_ref[:, sl], aim_ref[:, sl]
            h0r, h0i = h0r_ref[:, sl], h0i_ref[:, sl]
            hr = ar * h0r - ai * h0i + re[:, q * LANES:(q + 1) * LANES]
            hi = ar * h0i + ai * h0r + im[:, q * LANES:(q + 1) * LANES]
            sr_ref[:, sl] = hr
            si_ref[:, sl] = hi
            t = _dot(hr.astype(BF16), cre_ref[4 * j + q]) + _dot(hi.astype(BF16), cim_ref[4 * j + q])
            acc = t if acc is None else acc + t
        ys.append(acc)
    y = jnp.concatenate(ys, axis=1) + d_ref[...] * u
    y_ref[...] = _gelu_glu(y, wglu_ref).astype(BF16)


def _s5_step(u, h0r, h0i, bre, bim, a_re, a_im, cre, cim, d, wglu):
    b = u.shape[0]
    return pl.pallas_call(
        _s5_step_kernel,
        out_shape=[jax.ShapeDtypeStruct((b, S5_WIDTH), BF16), jax.ShapeDtypeStruct((b, S5_LANES), F32),
                   jax.ShapeDtypeStruct((b, S5_LANES), F32)],
        compiler_params=_cparams(),
        name="s5_step",
    )(u, h0r, h0i, bre, bim, a_re.reshape(1, S5_LANES), a_im.reshape(1, S5_LANES), cre, cim, d, wglu)


def _mla_prompt_kernel(iq_ref, jk_ref, q_ref, k_ref, v_ref, o_ref, m_ref, acc_ref, *, hps):
    p = pl.program_id(1)
    i, j = iq_ref[p], jk_ref[p]
    tq, tk = q_ref.shape[0], k_ref.shape[0]
    low = lax.broadcasted_iota(jnp.int32, (1, LANES), 1) < MLA_V

    @pl.when(j == 0)
    def _():
        m_ref[...] = jnp.full_like(m_ref, -jnp.inf)
        acc_ref[...] = jnp.zeros_like(acc_ref)

    def scores(hh):
        sl = slice(hh * LANES, (hh + 1) * LANES)
        return _dot_nt(q_ref[:, sl], k_ref[:, sl])

    def values(hh):
        v = v_ref[:, (hh // 2) * LANES:(hh // 2 + 1) * LANES]
        return jnp.where(low, v, jnp.ones_like(v)) if hh % 2 == 0 else jnp.where(low, jnp.ones_like(v), v)

    def block(diagonal):
        ahead = 2
        queue = [scores(hh) for hh in range(min(ahead, hps))]
        pending = []
        for hh in range(hps):
            s = queue.pop(0)
            if hh + ahead < hps:
                queue.append(scores(hh + ahead))
            if diagonal:
                row = lax.broadcasted_iota(jnp.int32, (tq, tk), 0)
                col = lax.broadcasted_iota(jnp.int32, (tq, tk), 1)
                s = jnp.where(col <= row, s, -jnp.inf)
            m_prev = m_ref[hh]
            m_new = jnp.maximum(m_prev, jnp.max(s, axis=-1, keepdims=True))
            pending.append((hh, jnp.exp2(s - m_new).astype(BF16), jnp.exp2(m_prev - m_new)))
            m_ref[hh] = m_new
            for ph, pr, alpha in pending[:-1] if hh + 1 < hps else pending:
                acc_ref[ph] = alpha * acc_ref[ph] + _dot(pr, values(ph))
            pending = pending[-1:] if hh + 1 < hps else []

    @pl.when(j < i)
    def _():
        block(False)

    @pl.when(j == i)
    def _():
        block(True)
        for pair in range(hps // 2):
            a0, a1 = acc_ref[2 * pair], acc_ref[2 * pair + 1]
            o = jnp.where(low, a0 / pltpu.roll(a0, MLA_V, axis=1), a1 / pltpu.roll(a1, MLA_V, axis=1))
            o_ref[:, pair * LANES:(pair + 1) * LANES] = o.astype(BF16)


def _mla_prompt_attn(q, k, v, tile, hps):
    t = q.shape[0]
    nq = t // tile
    pairs = [(i, j) for i in range(nq) for j in range(i + 1)]
    iq = jnp.asarray([p[0] for p in pairs], jnp.int32)
    jk = jnp.asarray([p[1] for p in pairs], jnp.int32)
    grid_spec = pltpu.PrefetchScalarGridSpec(
        num_scalar_prefetch=2,
        grid=(MLA_HEADS // hps, len(pairs)),
        in_specs=[pl.BlockSpec((tile, hps * LANES), lambda h, p, iq, jk: (iq[p], h)),
                  pl.BlockSpec((tile, hps * LANES), lambda h, p, iq, jk: (jk[p], h)),
                  pl.BlockSpec((tile, hps * MLA_V), lambda h, p, iq, jk: (jk[p], h))],
        out_specs=pl.BlockSpec((tile, hps * MLA_V), lambda h, p, iq, jk: (iq[p], h)),
        scratch_shapes=[pltpu.VMEM((hps, tile, 1), F32), pltpu.VMEM((hps, tile, LANES), F32)],
    )
    return pl.pallas_call(
        functools.partial(_mla_prompt_kernel, hps=hps),
        grid_spec=grid_spec,
        out_shape=jax.ShapeDtypeStruct((t, MLA_HEADS * MLA_V), BF16),
        compiler_params=_cparams(("arbitrary", "arbitrary")),
        name="mla_prompt_attn",
    )(iq, jk, q, k, v)


def _qlat_kernel(q_ref, wukt_ref, o_ref):
    for hd in range(MLA_HEADS):
        o_ref[hd] = _dot(q_ref[:, hd * LANES:(hd + 1) * LANES], wukt_ref[hd]).astype(BF16)


def _uv_kernel(o_ref, w_ref, y_ref):
    y_ref[...] = _dot(o_ref[...], w_ref[...]).astype(BF16)


def _mla_sample_kernel(pt_ref, ql_ref, qr_ref, cn_ref, kn_ref, ckv_hbm, krt_hbm, o_ref,
                       cbuf, kbuf, sem, *, npages, sub):
    b = pl.program_id(0)
    nb = pl.num_programs(0)
    slot = lax.rem(b, 2)

    def copies(bb, sl):
        out = []
        for p in range(npages):
            page = pt_ref[bb, p]
            tok = pl.ds(p * PAGE, PAGE)
            out.append(pltpu.make_async_copy(ckv_hbm.at[0, page], cbuf.at[sl, tok], sem.at[sl, 0]))
            out.append(pltpu.make_async_copy(krt_hbm.at[0, page], kbuf.at[sl, :, tok], sem.at[sl, 1]))
        return out

    def start_all(cps):
        for n, cp in enumerate(cps):
            cp.start(priority=(n // 2) % 2)

    @pl.when(b == 0)
    def _():
        start_all(copies(b, slot))

    @pl.when(b + 1 < nb)
    def _():
        start_all(copies(b + 1, 1 - slot))

    ql = ql_ref[0]
    qr = qr_ref[0]
    cn = cn_ref[0].astype(BF16).astype(F32)
    kn = kn_ref[0].astype(BF16).astype(F32)
    s_new = (jnp.sum(ql.astype(F32) * cn, axis=-1, keepdims=True)
             + jnp.sum(qr.astype(F32) * kn, axis=-1, keepdims=True))

    for cp in copies(b, slot):
        cp.wait()

    nsub = npages * PAGE // sub
    cps = [cbuf[slot, pl.ds(sc * sub, sub), :].astype(BF16) for sc in range(nsub)]
    ss = [_dot_nt(ql, cps[sc]) + _dot(qr, kbuf[slot, :, pl.ds(sc * sub, sub)].astype(BF16)) for sc in range(nsub)]
    ms = [jnp.max(s, axis=-1, keepdims=True) for s in ss]
    prs = [jnp.exp(s - m) for s, m in zip(ss, ms)]
    parts = [(m, jnp.sum(pr, axis=-1, keepdims=True), _dot(pr.astype(BF16), cp)) for m, pr, cp in zip(ms, prs, cps)]
    m_all = s_new
    for m, _, _ in parts:
        m_all = jnp.maximum(m_all, m)
    w_new = jnp.exp(s_new - m_all)
    den = w_new
    acc = w_new * cn
    for m, l, o in parts:
        w = jnp.exp(m - m_all)
        den = den + w * l
        acc = acc + w * o
    o_ref[0] = (acc / den).astype(BF16)


def _mla_sample_attn(q, c_new, kr_new, cache_ckv, cache_krope, page_table, wukt, wuv_blk, sub):
    b = q.shape[0]
    npages = page_table.shape[1]
    qlat = pl.pallas_call(
        _qlat_kernel,
        out_shape=jax.ShapeDtypeStruct((MLA_HEADS, b, KV_LORA), BF16),
        name="mla_qlat",
    )(q, wukt)
    ql = qlat.transpose(1, 0, 2)
    qr = q.reshape(b, MLA_HEADS, LANES)[:, :, MLA_NOPE:MLA_QK]
    krt = jnp.swapaxes(cache_krope, 2, 3)
    n = npages * PAGE
    grid_spec = pltpu.PrefetchScalarGridSpec(
        num_scalar_prefetch=1,
        grid=(b,),
        in_specs=[pl.BlockSpec((1, MLA_HEADS, KV_LORA), lambda i, pt: (i, 0, 0)),
                  pl.BlockSpec((1, MLA_HEADS, MLA_ROPE), lambda i, pt: (i, 0, 0)),
                  pl.BlockSpec((1, 1, KV_LORA), lambda i, pt: (i, 0, 0)),
                  pl.BlockSpec((1, 1, MLA_ROPE), lambda i, pt: (i, 0, 0)),
                  pl.BlockSpec(memory_space=pl.ANY),
                  pl.BlockSpec(memory_space=pl.ANY)],
        out_specs=pl.BlockSpec((1, MLA_HEADS, KV_LORA), lambda i, pt: (i, 0, 0)),
        scratch_shapes=[pltpu.VMEM((2, n, KV_LORA), F32), pltpu.VMEM((2, MLA_ROPE, n), F32),
                        pltpu.SemaphoreType.DMA((2, 2))],
    )
    o_lat = pl.pallas_call(
        functools.partial(_mla_sample_kernel, npages=npages, sub=sub),
        grid_spec=grid_spec,
        out_shape=jax.ShapeDtypeStruct((b, MLA_HEADS, KV_LORA), BF16),
        compiler_params=_cparams(("arbitrary",)),
        name="mla_sample_attn",
    )(page_table, ql, qr, c_new.reshape(b, 1, KV_LORA), kr_new.reshape(b, 1, MLA_ROPE), cache_ckv, krt)
    return pl.pallas_call(
        _uv_kernel,
        out_shape=jax.ShapeDtypeStruct((b, MLA_HEADS * MLA_V), BF16),
        name="mla_sample_uv",
    )(o_lat.reshape(b, MLA_HEADS * KV_LORA), wuv_blk)


def _router_gates(h2, wr_ref, br_ref):
    h_hi = h2.astype(BF16)
    h_lo = (h2 - h_hi.astype(F32)).astype(BF16)
    logits = _dot(h_hi, wr_ref[0]) + (_dot(h_lo, wr_ref[0]) + _dot(h_hi, wr_ref[1]))
    scores = jax.nn.sigmoid(logits)
    sel = scores + br_ref[...]
    lane = lax.broadcasted_iota(jnp.int32, (1, LANES), 1)
    pos = lane % GROUP_SIZE
    others, wrapped = [], []
    for r in range(1, GROUP_SIZE):
        wrap = pos + r >= GROUP_SIZE
        fwd = pltpu.roll(sel, LANES - r, axis=1)
        bwd = pltpu.roll(sel, GROUP_SIZE - r, axis=1)
        others.append(jnp.where(wrap, bwd, fwd))
        wrapped.append(wrap)
    a, b, c, d = sel, others[0], others[1], others[2]
    hi1, lo1 = jnp.maximum(a, b), jnp.minimum(a, b)
    hi2, lo2 = jnp.maximum(c, d), jnp.minimum(c, d)
    gscore = jnp.maximum(hi1, hi2) + jnp.maximum(jnp.minimum(hi1, hi2), jnp.maximum(lo1, lo2))
    real = lane < N_EXPERTS
    gscore = jnp.where(real, gscore, -jnp.inf)
    gmax = jnp.max(gscore, axis=-1, keepdims=True)
    gidx = (lane // GROUP_SIZE).astype(F32)
    chosen = jnp.min(jnp.where(gscore == gmax, gidx, float(LANES)), axis=-1, keepdims=True)
    rank = jnp.zeros(sel.shape, F32)
    for o, wrap in zip(others, wrapped):
        rank = rank + jnp.where(wrap, jnp.where(o >= sel, 1.0, 0.0), jnp.where(o > sel, 1.0, 0.0))
    w = jnp.where(gidx == chosen, jnp.where(rank < 2.0, scores, 0.0), 0.0)
    return w / jnp.sum(w, axis=-1, keepdims=True)


def _post_kernel(*refs, n_mix):
    x_ref = refs[0]
    ys = refs[1:1 + n_mix]
    ws = refs[1 + n_mix:1 + 2 * n_mix]
    gt_ref, sh_ref, sc_ref, g_ref, wr_ref, br_ref, x1_ref, h2_ref, gate_ref = refs[1 + 2 * n_mix:]
    blocks = _row_blocks(x_ref.shape[0])
    per_row = gt_ref.shape[0] != 1
    mod = lambda ref, rs: ref[rs, :] if per_row else ref[...]
    mixes = []
    for rs in blocks:
        mix = None
        for y_ref, w_ref in zip(ys, ws):
            t = _dot(y_ref[rs, :], w_ref[...])
            mix = t if mix is None else mix + t
        mixes.append(mix)
    x1s = [x_ref[rs, :] + mod(gt_ref, rs) * mix for rs, mix in zip(blocks, mixes)]
    h2s = [_rms(x1, g_ref[...]) * (1.0 + mod(sc_ref, rs)) + mod(sh_ref, rs) for rs, x1 in zip(blocks, x1s)]
    gates = [_router_gates(h2, wr_ref, br_ref) for h2 in h2s]
    for rs, x1, h2, gate in zip(blocks, x1s, h2s, gates):
        x1_ref[rs, :] = x1
        h2_ref[rs, :] = h2.astype(BF16)
        gate_ref[rs, :] = gate


def _post(x, ys, ws, gt, sh, sc, g, wr, br, tile):
    r = x.shape[0]
    per_row = gt.shape[0] != 1
    n_mix = len(ys)
    in_specs = ([pl.BlockSpec((tile, D_MODEL), lambda i: (i, 0))]
                + [pl.BlockSpec((tile, y.shape[1]), lambda i: (i, 0)) for y in ys]
                + [_full(w.shape) for w in ws]
                + [_rows(tile, D_MODEL, per_row)] * 3
                + [_full(g.shape), _full(wr.shape), _full(br.shape)])
    widths = [(D_MODEL, F32), (D_MODEL, BF16), (LANES, F32)]
    return pl.pallas_call(
        functools.partial(_post_kernel, n_mix=n_mix),
        grid=(r // tile,),
        in_specs=in_specs,
        out_specs=[pl.BlockSpec((tile, w), lambda i: (i, 0)) for w, _ in widths],
        out_shape=[jax.ShapeDtypeStruct((r, w), dt) for w, dt in widths],
        compiler_params=_cparams(("arbitrary",)),
        name="post_mixer",
    )(x, *ys, *ws, gt, sh, sc, g, wr, br)


N_GROUPS = N_EXPERTS // GROUP_SIZE
MOE_CHUNK = 128
_TN = (((0,), (0,)), ((), ()))


def _moe_kernel(h_ref, gate_ref, x1_ref, gt_ref, tri_ref, wg_hbm, wu_hbm, wd_hbm, o_ref,
                wg_v, wu_v, wd_v, sem, hs_ref, ys_ref, *, layer):
    @pl.when(pl.program_id(0) == 0)
    def _():
        copies = [pltpu.make_async_copy(src.at[layer], dst, sem.at[k])
                  for k, (src, dst) in enumerate(((wg_hbm, wg_v), (wu_hbm, wu_v), (wd_hbm, wd_v)))]
        for cp in copies:
            cp.start()
        for cp in copies:
            cp.wait()

    nrows = hs_ref.shape[0]
    gate = gate_ref[...]
    lane = lax.broadcasted_iota(jnp.int32, (1, LANES), 1)
    er = lax.broadcasted_iota(jnp.int32, (LANES, LANES), 0)
    ec = lax.broadcasted_iota(jnp.int32, (LANES, LANES), 1)
    member = jnp.where((er // GROUP_SIZE == ec) & (er < N_EXPERTS), 1.0, 0.0).astype(BF16)
    before = jnp.where(er < ec, 1.0, 0.0).astype(BF16)
    g1 = gate.astype(BF16)
    ind = jnp.where(_dot(g1, member) > 0.5, 1.0, 0.0)
    rank = _dot(tri_ref[...], ind.astype(BF16))
    count = jnp.sum(ind, axis=0, keepdims=True).astype(jnp.int32)
    shift = MOE_CHUNK.bit_length() - 1
    region = (((count + (MOE_CHUNK - 1)) >> shift) << shift).astype(F32)
    start = _dot(jnp.broadcast_to(region, (SUBLANES, LANES)).astype(BF16), before)[0:1]
    pos = jnp.sum(ind * (start + rank), axis=-1, keepdims=True)
    slot = lax.broadcasted_iota(jnp.int32, (1, nrows), 1).astype(F32)
    place = jnp.where(pos == slot, 1.0, 0.0).astype(BF16)

    r1 = gate - g1.astype(F32)
    g2 = r1.astype(BF16)
    g3 = (r1 - g2.astype(F32)).astype(BF16)
    moved = lax.dot_general(place, jnp.concatenate([h_ref[...], g1, g2, g3], axis=1), _TN,
                            preferred_element_type=F32)
    hs_ref[...] = moved[:, :D_MODEL].astype(BF16)
    gs = (moved[:, D_MODEL:D_MODEL + LANES] + moved[:, D_MODEL + LANES:D_MODEL + 2 * LANES]
          + moved[:, D_MODEL + 2 * LANES:])

    ends = [jnp.sum(jnp.where(lane == g, start + region, 0.0)).astype(jnp.int32) for g in range(N_GROUPS)]
    for c in range(nrows // MOE_CHUNK):
        first = c * MOE_CHUNK
        rows = pl.ds(first, MOE_CHUNK)
        gs_c = gs[first:first + MOE_CHUNK]

        @pl.when(first < ends[-1])
        def _():
            grp = sum((first >= e).astype(jnp.int32) for e in ends[:-1])
            hs_c = hs_ref[rows, :]
            experts = [grp * GROUP_SIZE + k for k in range(GROUP_SIZE)]
            gates = [_dot(hs_c, wg_v[e]) for e in experts]
            ups = [_dot(hs_c, wu_v[e]) for e in experts]
            acc = None
            for e, a, u in zip(experts, gates, ups):
                gcol = jnp.sum(jnp.where(lane == e, gs_c, 0.0), axis=-1, keepdims=True)
                t = _dot(((a * jax.nn.sigmoid(a)) * u * gcol).astype(BF16), wd_v[e])
                acc = t if acc is None else acc + t
            ys_ref[rows, :] = acc.astype(BF16)

        @pl.when(first >= ends[-1])
        def _():
            ys_ref[rows, :] = jnp.zeros((MOE_CHUNK, D_MODEL), BF16)

    o_ref[...] = x1_ref[...] + gt_ref[...] * _dot(place, ys_ref[...])


def _moe(h2, gate, x1, gt, wg, wu, wd, layer, tile):
    r = h2.shape[0]
    per_row = gt.shape[0] != 1
    nrows = (-(-tile // MOE_CHUNK) + N_GROUPS) * MOE_CHUNK
    idx = jnp.arange(tile)
    tri = (idx[None, :] < idx[:, None]).astype(BF16)
    return pl.pallas_call(
        functools.partial(_moe_kernel, layer=layer),
        grid=(r // tile,),
        in_specs=[pl.BlockSpec((tile, D_MODEL), lambda i: (i, 0)),
                  pl.BlockSpec((tile, LANES), lambda i: (i, 0)),
                  pl.BlockSpec((tile, D_MODEL), lambda i: (i, 0)),
                  _rows(tile, D_MODEL, per_row),
                  _full(tri.shape),
                  pl.BlockSpec(memory_space=pl.ANY),
                  pl.BlockSpec(memory_space=pl.ANY),
                  pl.BlockSpec(memory_space=pl.ANY)],
        out_specs=pl.BlockSpec((tile, D_MODEL), lambda i: (i, 0)),
        out_shape=jax.ShapeDtypeStruct((r, D_MODEL), F32),
        scratch_shapes=[pltpu.VMEM((N_EXPERTS, D_MODEL, D_FF), BF16),
                        pltpu.VMEM((N_EXPERTS, D_MODEL, D_FF), BF16),
                        pltpu.VMEM((N_EXPERTS, D_FF, D_MODEL), BF16),
                        pltpu.SemaphoreType.DMA((3,)),
                        pltpu.VMEM((nrows, D_MODEL), BF16),
                        pltpu.VMEM((nrows, D_MODEL), BF16)],
        compiler_params=_cparams(("arbitrary",)),
        name="moe",
    )(h2, gate, x1, gt, tri, wg, wu, wd)


def _front_odd_kernel(x_ref, sh_ref, sc_ref, g_ref, win_ref, gq_ref, gk_ref, rc_ref, rs_ref, rot_ref,
                      q_ref, k_ref, v_ref, ku_ref, vu_ref):
    x = x_ref[...]
    h = _rms(x, g_ref[...]) * (1.0 + sc_ref[...]) + sh_ref[...]
    z = _dot(h.astype(BF16), win_ref[...])
    rc, rs, rot = rc_ref[...], rs_ref[...], rot_ref[...]
    scale = SWA_HD ** -0.5
    for hd in range(SWA_HEADS):
        sl = slice(hd * LANES, (hd + 1) * LANES)
        qn = _rope(_rms(z[:, sl], gq_ref[...], SWA_HD), rot, rc, rs, False)
        q_ref[:, sl] = (qn * scale).astype(BF16)
    ks, vs = [], []
    for kh in range(SWA_KV):
        sl = slice(kh * LANES, (kh + 1) * LANES)
        zk = z[:, (SWA_HEADS + kh) * LANES:(SWA_HEADS + kh + 1) * LANES]
        kn = _rope(_rms(zk, gk_ref[...], SWA_HD), rot, rc, rs, True)
        vv = z[:, (SWA_HEADS + SWA_KV + kh) * LANES:(SWA_HEADS + SWA_KV + kh + 1) * LANES]
        k_ref[:, sl] = kn.astype(BF16)
        v_ref[:, sl] = vv.astype(BF16)
        ks.append(kn)
        vs.append(vv)
    for j in range(SWA_KV // 2):
        sl = slice(j * LANES, (j + 1) * LANES)
        ku_ref[:, sl] = ks[2 * j] + pltpu.roll(ks[2 * j + 1], SWA_HD, axis=1)
        vu_ref[:, sl] = vs[2 * j] + pltpu.roll(vs[2 * j + 1], SWA_HD, axis=1)


def _front_odd(x, sh, sc, g, win, gq, gk, tabs, tile):
    r = x.shape[0]
    per_row_mod = sh.shape[0] != 1
    per_row_tab = tabs[0].shape[0] != 1
    in_specs = [pl.BlockSpec((tile, D_MODEL), lambda i: (i, 0)),
                _rows(tile, D_MODEL, per_row_mod), _rows(tile, D_MODEL, per_row_mod),
                _full(g.shape), _full(win.shape), _full(gq.shape), _full(gk.shape),
                _rows(tile, LANES, per_row_tab), _rows(tile, LANES, per_row_tab), _full(tabs[2].shape)]
    widths = [(SWA_HEADS * LANES, BF16), (SWA_KV * LANES, BF16), (SWA_KV * LANES, BF16),
              (SWA_KV * SWA_HD, F32), (SWA_KV * SWA_HD, F32)]
    return pl.pallas_call(
        _front_odd_kernel,
        grid=(r // tile,),
        in_specs=in_specs,
        out_specs=[pl.BlockSpec((tile, w), lambda i: (i, 0)) for w, _ in widths],
        out_shape=[jax.ShapeDtypeStruct((r, w), dt) for w, dt in widths],
        compiler_params=_cparams(("arbitrary",)),
        name="front_odd",
    )(x, sh, sc, g, win, gq, gk, *tabs)


def _swa_prompt_kernel(sink_ref, q_ref, kp_ref, kc_ref, vp_ref, vc_ref, o_ref):
    n = pl.program_id(0)
    w = WINDOW
    qi = lax.broadcasted_iota(jnp.int32, (w, 2 * w), 0) + w
    kj = lax.broadcasted_iota(jnp.int32, (w, 2 * w), 1)
    valid = (kj <= qi) & (qi - kj < w) & (n * w - w + kj >= 0)
    for kh in range(SWA_KV):
        sl = slice(kh * LANES, (kh + 1) * LANES)
        kk = jnp.concatenate([kp_ref[:, sl], kc_ref[:, sl]], axis=0)
        vv = jnp.concatenate([vp_ref[:, sl], vc_ref[:, sl]], axis=0)
        for gi in range(SWA_GROUP):
            hd = kh * SWA_GROUP + gi
            hsl = slice(hd * LANES, (hd + 1) * LANES)
            s = jnp.where(valid, _dot_nt(q_ref[:, hsl], kk), -jnp.inf)
            sink = sink_ref[hd]
            m = jnp.maximum(jnp.max(s, axis=-1, keepdims=True), sink)
            pr = jnp.exp(s - m)
            den = jnp.sum(pr, axis=-1, keepdims=True) + jnp.exp(sink - m)
            o_ref[:, hsl] = (_dot(pr.astype(BF16), vv) / den).astype(BF16)


def _swa_prompt_attn(q, k, v, sinks):
    t = q.shape[0]
    w = WINDOW
    prev = lambda n, s: (jnp.maximum(n - 1, 0), 0)
    cur = lambda n, s: (n, 0)
    grid_spec = pltpu.PrefetchScalarGridSpec(
        num_scalar_prefetch=1,
        grid=(t // w,),
        in_specs=[pl.BlockSpec((w, SWA_HEADS * LANES), cur),
                  pl.BlockSpec((w, SWA_KV * LANES), prev), pl.BlockSpec((w, SWA_KV * LANES), cur),
                  pl.BlockSpec((w, SWA_KV * LANES), prev), pl.BlockSpec((w, SWA_KV * LANES), cur)],
        out_specs=pl.BlockSpec((w, SWA_HEADS * LANES), cur),
    )
    return pl.pallas_call(
        _swa_prompt_kernel,
        grid_spec=grid_spec,
        out_shape=jax.ShapeDtypeStruct((t, SWA_HEADS * LANES), BF16),
        compiler_params=_cparams(("arbitrary",)),
        name="swa_prompt_attn",
    )(sinks, q, k, k, v, v)


def _swa_sample_kernel(q_ref, kc_ref, vc_ref, kn_ref, vn_ref, sink_ref, o_ref, wk_ref, wv_ref, *, sb):
    w = WINDOW
    width = SWA_KV * SWA_HD
    rowgrp = lax.broadcasted_iota(jnp.int32, (SWA_HEADS, 1), 0) // SWA_GROUP
    col = lax.broadcasted_iota(jnp.int32, (1, w), 1)
    low = lax.broadcasted_iota(jnp.int32, (1, LANES), 1) < SWA_HD
    sink = sink_ref[...]
    for b in range(sb):
        qf = q_ref[b].astype(F32)
        a = jnp.concatenate([qf, jnp.zeros_like(qf)], axis=1)
        qx = a
        for kh in range(1, SWA_KV):
            qx = jnp.where(rowgrp == kh, pltpu.roll(a, kh * SWA_HD, axis=1), qx)
        kc = kc_ref[b]
        vc = vc_ref[b]
        kn = kn_ref[b]
        vn = vn_ref[b]
        s = _dot_nt(qx.astype(BF16), kc.astype(BF16))
        s = jnp.where(col >= 1, s, -jnp.inf)
        s_new = jnp.sum(qx.astype(BF16).astype(F32) * kn.astype(BF16).astype(F32), axis=-1, keepdims=True)
        m = jnp.maximum(jnp.maximum(jnp.max(s, axis=-1, keepdims=True), s_new), sink)
        pr = jnp.exp(s - m)
        pn = jnp.exp(s_new - m)
        den = jnp.sum(pr, axis=-1, keepdims=True) + pn + jnp.exp(sink - m)
        o = (_dot(pr.astype(BF16), vc.astype(BF16)) + pn * vn.astype(BF16).astype(F32)) / den
        ox = o
        for kh in range(1, SWA_KV):
            ox = jnp.where(rowgrp == kh, pltpu.roll(o, width - kh * SWA_HD, axis=1), ox)
        o_ref[b] = jnp.where(low, ox[:, :LANES], 0.0).astype(BF16)
        wk_ref[b, pl.ds(0, w - 1), :] = kc_ref[b, pl.ds(1, w - 1), :]
        wk_ref[b, pl.ds(w - 1, 1), :] = kn
        wv_ref[b, pl.ds(0, w - 1), :] = vc_ref[b, pl.ds(1, w - 1), :]
        wv_ref[b, pl.ds(w - 1, 1), :] = vn


def _swa_sample_attn(q, kc, vc, kn, vn, sinks, sb):
    b = q.shape[0]
    w = WINDOW
    width = SWA_KV * SWA_HD
    blk3 = lambda s1, s2: pl.BlockSpec((sb, s1, s2), lambda i: (i, 0, 0))
    return pl.pallas_call(
        functools.partial(_swa_sample_kernel, sb=sb),
        grid=(b // sb,),
        in_specs=[blk3(SWA_HEADS, LANES), blk3(w, width), blk3(w, width), blk3(1, width), blk3(1, width),
                  _full((SWA_HEADS, 1))],
        out_specs=[blk3(SWA_HEADS, LANES), blk3(w, width), blk3(w, width)],
        out_shape=[jax.ShapeDtypeStruct((b, SWA_HEADS, LANES), BF16),
                   jax.ShapeDtypeStruct((b, w, width), F32), jax.ShapeDtypeStruct((b, w, width), F32)],
        compiler_params=_cparams(("arbitrary",)),
        name="swa_sample_attn",
    )(q.reshape(b, SWA_HEADS, LANES), kc, vc, kn.reshape(b, 1, width), vn.reshape(b, 1, width),
      sinks.reshape(SWA_HEADS, 1))


def _rope_cos_sin(pos):
    inv = jnp.concatenate([ROPE_THETA ** (-jnp.arange(h, dtype=F32) / h) for h in (MLA_ROPE // 2, SWA_ROT // 2)])
    ang = pos.astype(F32)[:, None] * inv[None, :]
    cos, sin = jnp.cos(ang), jnp.sin(ang)
    cut = MLA_ROPE // 2
    return (cos[:, :cut], sin[:, :cut]), (cos[:, cut:], sin[:, cut:])


def _rope_tables(cos_sin, offset):
    cos, sin = cos_sin
    n, half = cos.shape
    rest = LANES - offset - 2 * half
    rc = jnp.concatenate([jnp.ones((n, offset), F32), cos, cos, jnp.ones((n, rest), F32)], axis=1)
    rs = jnp.concatenate([jnp.zeros((n, offset), F32), sin, sin, jnp.zeros((n, rest), F32)], axis=1)
    lane = jnp.arange(LANES)
    first = (lane >= offset) & (lane < offset + half)
    second = (lane >= offset + half) & (lane < offset + 2 * half)
    src = lane[:, None]
    rot = (jnp.where(first[None, :] & (src == lane[None, :] + half), -1.0, 0.0)
           + jnp.where(second[None, :] & (src == lane[None, :] - half), 1.0, 0.0))
    return rc, rs, rot.astype(BF16)


def _pad_lanes(v, offset=0):
    return jnp.zeros((1, LANES), F32).at[0, offset:offset + v.shape[0]].set(v)


def _even_weights(w_in, g_qa, w_uq, g_qh, g_kva, g_kr, w_uk, w_uv):
    o3 = S5_WIDTH + Q_LORA + KV_LORA
    win = jnp.zeros((D_MODEL, 1280), F32).at[:, :o3].set(w_in[:, :o3])
    win = win.at[:, o3 + MLA_NOPE:o3 + MLA_QK].set(w_in[:, o3:])
    wuq = jnp.zeros((MLA_HEADS, Q_LORA, LANES), F32).at[:, :, :MLA_QK].set(w_uq.transpose(1, 0, 2))
    wuk = jnp.zeros((KV_LORA, MLA_HEADS, LANES), F32).at[:, :, :MLA_NOPE].set(w_uk)
    front = (win.astype(BF16), g_qa.reshape(1, Q_LORA), wuq.astype(BF16), _pad_lanes(g_qh),
             g_kva.reshape(1, KV_LORA), _pad_lanes(g_kr, MLA_NOPE),
             wuk.reshape(KV_LORA, MLA_HEADS * LANES).astype(BF16),
             w_uv.reshape(KV_LORA, MLA_HEADS * MLA_V).astype(BF16))
    wukt = jnp.zeros((MLA_HEADS, LANES, KV_LORA), F32).at[:, :MLA_NOPE, :].set(w_uk.transpose(1, 2, 0))
    eye = jnp.eye(MLA_HEADS, dtype=F32)
    wuv_blk = jnp.einsum('chd,hg->hcgd', w_uv, eye).reshape(MLA_HEADS * KV_LORA, MLA_HEADS * MLA_V)
    return front, wukt.astype(BF16), wuv_blk.astype(BF16)


def _odd_weights(w_in, w_out):
    nq, nk = SWA_HEADS * SWA_HD, SWA_KV * SWA_HD
    nh = SWA_HEADS + 2 * SWA_KV
    win = jnp.zeros((D_MODEL, nh, LANES), F32).at[:, :, :SWA_HD].set(w_in.reshape(D_MODEL, nh, SWA_HD))
    wout = jnp.zeros((SWA_HEADS, LANES, D_MODEL), F32).at[:, :SWA_HD, :].set(w_out.reshape(SWA_HEADS, SWA_HD, D_MODEL))
    del nq, nk
    return win.reshape(D_MODEL, nh * LANES).astype(BF16), wout.reshape(SWA_HEADS * LANES, D_MODEL).astype(BF16)


def kernel(x_prompt, x_sample, c_prompt, c_sample, cache_ckv, cache_krope, page_table, state_s5_re, state_s5_im, cache_win_k, cache_win_v, w_mod, b_mod, g_norm_mix, g_norm_ffn, w_in_even, w_out_even, s5_lam_re, s5_lam_im, s5_log_step, s5_b_re, s5_b_im, s5_c_re, s5_c_im, s5_d, s5_w_glu, mla_g_qa, mla_w_uq, mla_g_qh, mla_g_kva, mla_g_kr, mla_w_uk, mla_w_uv, w_in_odd, w_out_odd, swa_g_q, swa_g_k, swa_sinks, w_router, b_router, moe_w_gate, moe_w_up, moe_w_down):
    t = x_prompt.shape[1]
    nb = x_sample.shape[0]
    past_len = page_table.shape[1] * PAGE
    row_tile = 512
    s5_seg = 64
    attn_tile, attn_heads = 1024, MLA_HEADS
    mla_scale = MLA_QK ** -0.5
    moe_tile = 512
    sample_sub = min(4096, past_len)
    swa_seq_block = 8

    n_c = 1 + nb
    rp = -(-n_c // SUBLANES) * SUBLANES
    c_all = jnp.concatenate([c_prompt, c_sample, jnp.zeros((rp - n_c, D_MODEL), F32)], axis=0)
    mod = _modulation(c_all, w_mod, b_mod)

    def mods(layer, sample):
        rows = mod[layer, 1:1 + nb] if sample else mod[layer, 0:1]
        return [rows[:, k * D_MODEL:(k + 1) * D_MODEL] for k in range(6)]

    wr = jnp.zeros((D_MODEL, LANES), F32).at[:, :N_EXPERTS].set(w_router)
    wr_hi = wr.astype(BF16)
    wr = jnp.stack([wr_hi, (wr - wr_hi.astype(F32)).astype(BF16)])
    br = _pad_lanes(b_router)
    wg, wu, wd = moe_w_gate.astype(BF16), moe_w_up.astype(BF16), moe_w_down.astype(BF16)

    xp = x_prompt.reshape(t, D_MODEL)
    xs = x_sample.reshape(nb, D_MODEL)
    mla_cs_p, swa_cs_p = _rope_cos_sin(jnp.arange(t))
    mla_cs_s, swa_cs_s = _rope_cos_sin(jnp.full((1,), past_len))
    mla_tabs_p = _rope_tables(mla_cs_p, MLA_NOPE)
    mla_tabs_s = _rope_tables(mla_cs_s, MLA_NOPE)
    swa_tabs_p = _rope_tables(swa_cs_p, 0)
    swa_tabs_s = _rope_tables(swa_cs_s, 0)

    front_w, wukt, wuv_blk = _even_weights(w_in_even[0], mla_g_qa[0], mla_w_uq[0], mla_g_qh[0], mla_g_kva[0],
                                           mla_g_kr[0], mla_w_uk[0], mla_w_uv[0])
    a_re, a_im, bb_re, bb_im, al_re, al_im = _s5_discretize(s5_lam_re[0], s5_lam_im[0], s5_log_step[0],
                                                            s5_b_re[0], s5_b_im[0], s5_seg)
    bre, bim, cre, cim = _s5_layouts(bb_re, bb_im, s5_c_re[0], s5_c_im[0])
    d_skip = s5_d[0].reshape(1, S5_WIDTH)
    wglu = s5_w_glu[0].astype(BF16)
    wo_ssm = w_out_even[0, :S5_WIDTH].astype(BF16)
    wo_att = w_out_even[0, S5_WIDTH:].astype(BF16)
    g_mix0 = g_norm_mix[0].reshape(1, D_MODEL)
    g_ffn0 = g_norm_ffn[0].reshape(1, D_MODEL)

    sh1, sc1, gt1, sh2, sc2, gt2 = mods(0, False)
    u, q, k, v, ckv_p, kr_p = _front_even(xp, sh1, sc1, g_mix0, front_w, mla_tabs_p,
                                          row_tile, mla_scale * math.log2(math.e))
    y_ssm, st = _s5_scan(u, bre, bim, a_re, a_im, al_re, al_im, cre, cim, d_skip, wglu, s5_seg)
    sr_p = st[:S5_CHUNKS].reshape(1, 1, S5_GROUPS, S5_STATE)
    si_p = st[S5_CHUNKS:].reshape(1, 1, S5_GROUPS, S5_STATE)
    y_att = _mla_prompt_attn(q, k, v, attn_tile, attn_heads)
    x1, h2, gate = _post(xp, [y_ssm, y_att], [wo_ssm, wo_att], gt1, sh2, sc2, g_ffn0, wr, br, row_tile)
    xp = _moe(h2, gate, x1, gt2, wg, wu, wd, 0, moe_tile)

    sh1, sc1, gt1, sh2, sc2, gt2 = mods(0, True)
    u, q, _, _, ckv_s, kr_s = _front_even(xs, sh1, sc1, g_mix0, front_w, mla_tabs_s, nb, mla_scale)
    y_ssm, sr_s, si_s = _s5_step(u, state_s5_re[0].reshape(nb, S5_LANES), state_s5_im[0].reshape(nb, S5_LANES),
                                 bre, bim, a_re, a_im, cre, cim, d_skip, wglu)
    y_att = _mla_sample_attn(q, ckv_s, kr_s, cache_ckv, cache_krope, page_table, wukt, wuv_blk, sample_sub)
    x1, h2, gate = _post(xs, [y_ssm, y_att], [wo_ssm, wo_att], gt1, sh2, sc2, g_ffn0, wr, br, nb)
    xs = _moe(h2, gate, x1, gt2, wg, wu, wd, 0, nb)

    win_odd, wout_odd = _odd_weights(w_in_odd[0], w_out_odd[0])
    gq, gk = _pad_lanes(swa_g_q[0]), _pad_lanes(swa_g_k[0])
    g_mix1 = g_norm_mix[1].reshape(1, D_MODEL)
    g_ffn1 = g_norm_ffn[1].reshape(1, D_MODEL)
    sinks = swa_sinks[0]

    sh1, sc1, gt1, sh2, sc2, gt2 = mods(1, False)
    q, k, v, ku, vu = _front_odd(xp, sh1, sc1, g_mix1, win_odd, gq, gk, swa_tabs_p, row_tile)
    o = _swa_prompt_attn(q, k, v, sinks)
    wk_p = ku[t - WINDOW:].reshape(1, 1, WINDOW, SWA_KV, SWA_HD)
    wv_p = vu[t - WINDOW:].reshape(1, 1, WINDOW, SWA_KV, SWA_HD)
    x1, h2, gate = _post(xp, [o], [wout_odd], gt1, sh2, sc2, g_ffn1, wr, br, row_tile)
    xp = _moe(h2, gate, x1, gt2, wg, wu, wd, 1, moe_tile)

    sh1, sc1, gt1, sh2, sc2, gt2 = mods(1, True)
    q, _, _, ku, vu = _front_odd(xs, sh1, sc1, g_mix1, win_odd, gq, gk, swa_tabs_s, nb)
    width = SWA_KV * SWA_HD
    o, wk_s, wv_s = _swa_sample_attn(q, cache_win_k[0].reshape(nb, WINDOW, width),
                                     cache_win_v[0].reshape(nb, WINDOW, width), ku, vu, sinks, swa_seq_block)
    x1, h2, gate = _post(xs, [o.reshape(nb, SWA_HEADS * LANES)], [wout_odd], gt1, sh2, sc2, g_ffn1, wr, br, nb)
    xs = _moe(h2, gate, x1, gt2, wg, wu, wd, 1, nb)

    return (xp.reshape(1, t, D_MODEL), xs.reshape(nb, 1, D_MODEL),
            ckv_p.reshape(1, 1, t, KV_LORA), kr_p.reshape(1, 1, t, MLA_ROPE), sr_p, si_p, wk_p, wv_p,
            ckv_s.reshape(1, nb, 1, KV_LORA), kr_s.reshape(1, nb, 1, MLA_ROPE),
            sr_s.reshape(1, nb, S5_GROUPS, S5_STATE), si_s.reshape(1, nb, S5_GROUPS, S5_STATE),
            wk_s.reshape(1, nb, WINDOW, SWA_KV, SWA_HD), wv_s.reshape(1, nb, WINDOW, SWA_KV, SWA_HD))
```
